```python
import math
import jax, jax.numpy as jnp
from jax import lax
import numpy as np

D_MODEL = 1024
BATCH = 8
SEQ = 2048
DEPTH = 4
DEC_BATCH = 32
DEC_SEQ = 1
PAST_LEN = 8192
PAGE_SIZE = 128

H_A = 8
DH_QK = D_MODEL // 32
DH_V = 2 * DH_QK
D_QK = 2 * H_A * DH_QK
D_A = H_A * DH_V
D_B = D_MODEL // 2
W_B = 3
D_C = D_MODEL
W_C = 31
D_FF = 2816
N_BUCKETS = 32
MAX_EXACT = N_BUCKETS // 2
MAX_DISTANCE = 128
Q_BLOCK = 128
N_EVEN = (DEPTH + 1) // 2
N_ODD = DEPTH // 2
D_IN_EVEN = 2 * D_QK + D_A + 3 * D_B
EPS = 1e-6

kernel_name = "diffattn_shortconv_conformer_macaron_step"


def rmsnorm(x, g):
    xf = x.astype(jnp.float32)
    y = xf * lax.rsqrt(jnp.mean(xf * xf, axis=-1, keepdims=True) + EPS) * g.astype(jnp.float32)
    return y.astype(x.dtype)


def layernorm(x, g, b):
    xf = x.astype(jnp.float32)
    mu = jnp.mean(xf, axis=-1, keepdims=True)
    xc = xf - mu
    y = xc * lax.rsqrt(jnp.mean(xc * xc, axis=-1, keepdims=True) + EPS) * g.astype(jnp.float32) + b.astype(jnp.float32)
    return y.astype(x.dtype)


def half_ffn(x, g, w_gate, w_up, w_down):
    h = rmsnorm(x, g)
    return 0.5 * ((jax.nn.silu(h @ w_gate) * (h @ w_up)) @ w_down)


def t5_bucket(rel):
    n = jnp.maximum(rel, 0)
    nf = jnp.maximum(n, 1).astype(jnp.float32)
    large = MAX_EXACT + (jnp.log(nf / MAX_EXACT) / math.log(MAX_DISTANCE / MAX_EXACT)
                         * (N_BUCKETS - MAX_EXACT)).astype(jnp.int32)
    large = jnp.minimum(large, N_BUCKETS - 1)
    return jnp.where(n < MAX_EXACT, n, large)


def causal_dwconv(u, prev, w):
    width = w.shape[0]
    u_ext = jnp.concatenate([prev.astype(u.dtype), u], axis=1)
    out = lax.conv_general_dilated(u_ext, w[:, None, :].astype(u.dtype), window_strides=(1,), padding='VALID',
                                   dimension_numbers=('NWC', 'WIO', 'NWC'), feature_group_count=u.shape[-1])
    return out, u_ext[:, u_ext.shape[1] - (width - 1):]


def diff_attn_core(q, k, v, q_pos, k_pos, lam, rel_bias):
    s = jnp.einsum('bqhcd,bkhcd->bhcqk', q, k).astype(jnp.float32) * (DH_QK ** -0.5)
    bias = rel_bias[t5_bucket(q_pos[:, None] - k_pos[None, :])].astype(jnp.float32)
    s = s + jnp.transpose(bias, (2, 0, 1))[None, :, None]
    mask = k_pos[None, :] <= q_pos[:, None]
    s = jnp.where(mask, s, -1e30)
    p = jax.nn.softmax(s, axis=-1)
    a = p[:, :, 0] - lam * p[:, :, 1]
    return jnp.einsum('bhqk,bkhe->bqhe', a.astype(v.dtype), v)


def even_mixer(h, k_past, v_past, conv_prev, pos0, w_in, w_out, q_gain, k_gain,
               lq1, lk1, lq2, lk2, subln_gain, conv_w, rel_bias, lam_init):
    b, t, _ = h.shape
    z = h @ w_in
    q = rmsnorm(z[..., :D_QK].reshape(b, t, H_A, 2, DH_QK), q_gain)
    k = rmsnorm(z[..., D_QK:2 * D_QK].reshape(b, t, H_A, 2, DH_QK), k_gain)
    v = z[..., 2 * D_QK:2 * D_QK + D_A].reshape(b, t, H_A, DH_V)
    o = 2 * D_QK + D_A
    gate_b = z[..., o:o + D_B]
    gate_c = z[..., o + D_B:o + 2 * D_B]
    x_in = z[..., o + 2 * D_B:]
    f32 = jnp.float32
    lam = (jnp.exp(jnp.sum(lq1.astype(f32) * lk1.astype(f32)))
           - jnp.exp(jnp.sum(lq2.astype(f32) * lk2.astype(f32))) + lam_init)
    if k_past is None:
        k_all, v_all = k, v
        k_pos = jnp.arange(t)
        nb = t // Q_BLOCK
        q_blocks = q.reshape(b, nb, Q_BLOCK, H_A, 2, DH_QK).swapaxes(0, 1)
        starts = pos0 + jnp.arange(nb) * Q_BLOCK

        def blk(args):
            qb, s0 = args
            return diff_attn_core(qb, k_all, v_all, s0 + jnp.arange(Q_BLOCK), k_pos, lam, rel_bias)

        att = lax.map(blk, (q_blocks, starts)).swapaxes(0, 1).reshape(b, t, H_A, DH_V)
    else:
        k_all = jnp.concatenate([k_past.astype(k.dtype), k], axis=1)
        v_all = jnp.concatenate([v_past.astype(v.dtype), v], axis=1)
        k_pos = jnp.arange(k_all.shape[1])
        att = diff_attn_core(q, k_all, v_all, pos0 + jnp.arange(t), k_pos, lam, rel_bias)
    att = rmsnorm(att, subln_gain) * (1.0 - lam_init)
    conv_out, conv_state = causal_dwconv(gate_c * x_in, conv_prev, conv_w)
    y = jnp.concatenate([att.reshape(b, t, D_A), gate_b * conv_out], axis=-1) @ w_out
    return y, k.reshape(b, t, 2 * H_A, DH_QK), v, conv_state


def conformer_conv(h, conv_prev, w_pw1, b_pw1, conv_w, conv_b, ln_g, ln_b, w_pw2, b_pw2):
    z = h @ w_pw1 + b_pw1
    u = z[..., :D_C] * jax.nn.sigmoid(z[..., D_C:])
    c, st = causal_dwconv(u, conv_prev, conv_w)
    c = layernorm(c + conv_b, ln_g, ln_b)
    return jax.nn.silu(c) @ w_pw2 + b_pw2, st


def setup_inputs(seed: int = 0) -> dict:
    key = jax.random.key(seed)
    ks = jax.random.split(key, 40)
    n_pages = PAST_LEN // PAGE_SIZE
    n_used = DEC_BATCH * n_pages
    n_phys = n_used + n_used // 4
    nrm = lambda k, shape, s=1.0: jax.random.normal(k, shape, jnp.float32) * s
    gain = lambda k, shape: 1.0 + 0.02 * jax.random.normal(k, shape, jnp.float32)
    page_table = jax.random.permutation(ks[6], n_phys)[:n_used].reshape(DEC_BATCH, n_pages).astype(jnp.int32)
    return {
        'x_prompt': nrm(ks[0], (BATCH, SEQ, D_MODEL)),
        'x_sample': nrm(ks[1], (DEC_BATCH, DEC_SEQ, D_MODEL)),
        'cache_k': nrm(ks[2], (N_EVEN, n_phys, PAGE_SIZE, 2 * H_A, DH_QK)),
        'cache_v': nrm(ks[3], (N_EVEN, n_phys, PAGE_SIZE, H_A, DH_V)),
        'state_conv_b': nrm(ks[4], (N_EVEN, DEC_BATCH, W_B - 1, D_B)),
        'state_conv_c': nrm(ks[5], (N_ODD, DEC_BATCH, W_C - 1, D_C), 0.5),
        'page_table': page_table,
        'rel_bias': nrm(ks[7], (N_BUCKETS, H_A), 0.5),
        'norm_ffn1': gain(ks[8], (DEPTH, D_MODEL)),
        'ffn1_w_gate': nrm(ks[9], (DEPTH, D_MODEL, D_FF), D_MODEL ** -0.5),
        'ffn1_w_up': nrm(ks[10], (DEPTH, D_MODEL, D_FF), D_MODEL ** -0.5),
        'ffn1_w_down': nrm(ks[11], (DEPTH, D_FF, D_MODEL), D_FF ** -0.5),
        'norm_mix': gain(ks[12], (DEPTH, D_MODEL)),
        'norm_ffn2': gain(ks[13], (DEPTH, D_MODEL)),
        'ffn2_w_gate': nrm(ks[14], (DEPTH, D_MODEL, D_FF), D_MODEL ** -0.5),
        'ffn2_w_up': nrm(ks[15], (DEPTH, D_MODEL, D_FF), D_MODEL ** -0.5),
        'ffn2_w_down': nrm(ks[16], (DEPTH, D_FF, D_MODEL), D_FF ** -0.5),
        'w_in_even': nrm(ks[17], (N_EVEN, D_MODEL, D_IN_EVEN), D_MODEL ** -0.5),
        'w_out_even': nrm(ks[18], (N_EVEN, D_A + D_B, D_MODEL), (D_A + D_B) ** -0.5),
        'q_norm_gain': gain(ks[19], (N_EVEN, DH_QK)),
        'k_norm_gain': gain(ks[20], (N_EVEN, DH_QK)),
        'lambda_q1': nrm(ks[21], (N_EVEN, DH_QK), 0.1),
        'lambda_k1': nrm(ks[22], (N_EVEN, DH_QK), 0.1),
        'lambda_q2': nrm(ks[23], (N_EVEN, DH_QK), 0.1),
        'lambda_k2': nrm(ks[24], (N_EVEN, DH_QK), 0.1),
        'subln_gain': gain(ks[25], (N_EVEN, DH_V)),
        'conv_b_w': nrm(ks[26], (N_EVEN, W_B, D_B), W_B ** -0.5),
        'w_pw1': nrm(ks[27], (N_ODD, D_MODEL, 2 * D_C), D_MODEL ** -0.5),
        'b_pw1': nrm(ks[28], (N_ODD, 2 * D_C), 0.02),
        'conv_c_w': nrm(ks[29], (N_ODD, W_C, D_C), W_C ** -0.5),
        'conv_c_b': nrm(ks[30], (N_ODD, D_C), 0.02),
        'ln_c_gain': gain(ks[31], (N_ODD, D_C)),
        'ln_c_bias': nrm(ks[32], (N_ODD, D_C), 0.02),
        'w_pw2': nrm(ks[33], (N_ODD, D_C, D_MODEL), D_C ** -0.5),
        'b_pw2': nrm(ks[34], (N_ODD, D_MODEL), 0.02),
    }


def reference(x_prompt, x_sample, cache_k, cache_v, state_conv_b, state_conv_c, page_table, rel_bias,
              norm_ffn1, ffn1_w_gate, ffn1_w_up, ffn1_w_down, norm_mix, norm_ffn2,
              ffn2_w_gate, ffn2_w_up, ffn2_w_down, w_in_even, w_out_even, q_norm_gain, k_norm_gain,
              lambda_q1, lambda_k1, lambda_q2, lambda_k2, subln_gain, conv_b_w,
              w_pw1, b_pw1, conv_c_w, conv_c_b, ln_c_gain, ln_c_bias, w_pw2, b_pw2):
    bp, tp, _ = x_prompt.shape
    bs, ts, _ = x_sample.shape
    past_len = page_table.shape[1] * cache_k.shape[2]
    xp, xs = x_prompt, x_sample
    kp_l, vp_l, ks_l, vs_l, cbp_l, cbs_l, ccp_l, ccs_l = [], [], [], [], [], [], [], []
    for li in range(DEPTH):
        xp = xp + half_ffn(xp, norm_ffn1[li], ffn1_w_gate[li], ffn1_w_up[li], ffn1_w_down[li])
        xs = xs + half_ffn(xs, norm_ffn1[li], ffn1_w_gate[li], ffn1_w_up[li], ffn1_w_down[li])
        hp = rmsnorm(xp, norm_mix[li])
        hs = rmsnorm(xs, norm_mix[li])
        if li % 2 == 0:
            e = li // 2
            lam_init = 0.8 - 0.6 * math.exp(-0.3 * li)
            prm = (w_in_even[e], w_out_even[e], q_norm_gain[e], k_norm_gain[e], lambda_q1[e], lambda_k1[e],
                   lambda_q2[e], lambda_k2[e], subln_gain[e], conv_b_w[e], rel_bias, lam_init)
            yp, kp, vp, cbp = even_mixer(hp, None, None, jnp.zeros((bp, W_B - 1, D_B), hp.dtype), 0, *prm)
            k_past = cache_k[e][page_table].reshape(bs, past_len, H_A, 2, DH_QK)
            v_past = cache_v[e][page_table].reshape(bs, past_len, H_A, DH_V)
            ys, ks_, vs_, cbs = even_mixer(hs, k_past, v_past, state_conv_b[e], past_len, *prm)
            kp_l.append(kp); vp_l.append(vp); ks_l.append(ks_); vs_l.append(vs_)
            cbp_l.append(cbp); cbs_l.append(cbs)
        else:
            o = li // 2
            prm = (w_pw1[o], b_pw1[o], conv_c_w[o], conv_c_b[o], ln_c_gain[o], ln_c_bias[o], w_pw2[o], b_pw2[o])
            yp, ccp = conformer_conv(hp, jnp.zeros((bp, W_C - 1, D_C), hp.dtype), *prm)
            ys, ccs = conformer_conv(hs, state_conv_c[o], *prm)
            ccp_l.append(ccp); ccs_l.append(ccs)
        xp = xp + yp
        xs = xs + ys
        xp = xp + half_ffn(xp, norm_ffn2[li], ffn2_w_gate[li], ffn2_w_up[li], ffn2_w_down[li])
        xs = xs + half_ffn(xs, norm_ffn2[li], ffn2_w_gate[li], ffn2_w_up[li], ffn2_w_down[li])
    return (xp, xs, jnp.stack(kp_l), jnp.stack(vp_l), jnp.stack(ks_l), jnp.stack(vs_l),
            jnp.stack(cbp_l), jnp.stack(cbs_l), jnp.stack(ccp_l), jnp.stack(ccs_l))
```

```python
import functools
import math

import jax
import jax.numpy as jnp
from jax import lax
from jax.experimental import pallas as pl
from jax.experimental.pallas import tpu as pltpu

F32 = jnp.float32
BF16 = jnp.bfloat16
EPS = 1e-6
NEG = -1e30

LANE = 128
SUBLANE = 8
VMEM_LIMIT_BYTES = 56 * 1024 * 1024

DH_QK = 32
DH_V = 64
N_BUCKETS = 32
MAX_EXACT = N_BUCKETS // 2
MAX_DISTANCE = 128
FAR = MAX_DISTANCE

TM_ROWS = 512
TM_ODD = 256
TQ = 256
HALO = 32
CONV_ROWS = 16
PAGES_PER_STEP = 8


def _params(sem):
    return pltpu.CompilerParams(dimension_semantics=sem, vmem_limit_bytes=VMEM_LIMIT_BYTES)


def _resident(shape):
    n = len(shape)
    return pl.BlockSpec(shape, lambda *_: (0,) * n, pipeline_mode=pl.Buffered(1))


def _rms(x, g):
    return x * lax.rsqrt(jnp.mean(x * x, axis=-1, keepdims=True) + EPS) * g


def _dot(a, b):
    return jnp.dot(a, b, preferred_element_type=F32)


def _group_rms(z, p_ref, gain):
    sq = z * z
    hi = sq.astype(BF16)
    lo = (sq - hi.astype(F32)).astype(BF16)
    ms = _dot(hi, p_ref[...]) + _dot(lo, p_ref[...])
    return z * lax.rsqrt(ms + EPS) * gain


def _ffn_kernel(x_ref, g_ref, wg_ref, wu_ref, wd_ref, o_ref, *, n_chunks):
    x = x_ref[...]
    h = _rms(x, g_ref[...]).astype(BF16)
    fc = wg_ref.shape[1] // n_chunks
    y = jnp.zeros(x.shape, F32)
    for c in range(n_chunks):
        gate = _dot(h, wg_ref[:, c * fc:(c + 1) * fc])
        up = _dot(h, wu_ref[:, c * fc:(c + 1) * fc])
        a = (jax.nn.silu(gate) * up).astype(BF16)
        y = y + _dot(a, wd_ref[c * fc:(c + 1) * fc, :])
    o_ref[...] = x + 0.5 * y


def _ffn(x, g, wg, wu, wd):
    m, d = x.shape
    f = wg.shape[1]
    tm = min(TM_ROWS, m)
    row = pl.BlockSpec((tm, d), lambda i: (i, 0))
    return pl.pallas_call(
        functools.partial(_ffn_kernel, n_chunks=2),
        grid=(m // tm,),
        in_specs=[row, _resident((1, d)), _resident((d, f)), _resident((d, f)), _resident((f, d))],
        out_specs=row,
        out_shape=jax.ShapeDtypeStruct((m, d), F32),
        compiler_params=_params(("parallel",)),
        name="ffn",
    )(x, g.reshape(1, d), wg, wu, wd)


def _even_in_prompt_kernel(x_ref, g_ref, w_ref, p_ref, qg_ref, kg_ref, cw_ref,
                           kt_ref, vt_ref, qtb_ref, kb_ref, vtb_ref, gc_ref, st_ref,
                           u_scr, *, tiles_per_seq, d_qk, d_a, d_b, scale):
    i = pl.program_id(0)
    tm = x_ref.shape[0]
    h = _rms(x_ref[...], g_ref[...]).astype(BF16)
    z = _dot(h, w_ref[...])
    q = _group_rms(z[:, :d_qk], p_ref, qg_ref[...]) * scale
    k = _group_rms(z[:, d_qk:2 * d_qk], p_ref, kg_ref[...])
    o = 2 * d_qk
    v = z[:, o:o + d_a]
    o += d_a
    gate_b = z[:, o:o + d_b]
    u = z[:, o + d_b:o + 2 * d_b] * z[:, o + 2 * d_b:o + 3 * d_b]

    kt = k.T
    vt = v.T
    kt_ref[...] = kt
    vt_ref[...] = vt
    qtb_ref[...] = q.T.astype(BF16)
    kb_ref[...] = k.astype(BF16)
    vtb_ref[...] = vt.astype(BF16)

    @pl.when(i % tiles_per_seq == 0)
    def _():
        u_scr[0:SUBLANE, :] = jnp.zeros((SUBLANE, d_b), F32)

    u_scr[SUBLANE:SUBLANE + tm, :] = u
    conv = (cw_ref[0:1, :] * u_scr[SUBLANE - 2:SUBLANE - 2 + tm, :]
            + cw_ref[1:2, :] * u_scr[SUBLANE - 1:SUBLANE - 1 + tm, :]
            + cw_ref[2:3, :] * u)
    gc_ref[...] = (gate_b * conv).astype(BF16)
    st_ref[...] = u_scr[tm + SUBLANE - 2:tm + SUBLANE, :]
    u_scr[0:SUBLANE, :] = u_scr[tm:tm + SUBLANE, :]


def _even_in_prompt(x, g, w_in, p_avg, qg, kg, cw, n_seq, seq):
    m, d = x.shape
    d_qk = d_a = d_b = p_avg.shape[0]
    tm = TM_ROWS
    tps = seq // tm
    row = lambda width, dt=None: pl.BlockSpec((tm, width), lambda i: (i, 0))
    col = pl.BlockSpec((d_qk, tm), lambda i: (0, i))
    seq_t = pl.BlockSpec((None, d_qk, tm), lambda i: (i // tps, 0, i % tps))
    kern = functools.partial(_even_in_prompt_kernel, tiles_per_seq=tps, d_qk=d_qk, d_a=d_a, d_b=d_b,
                             scale=DH_QK ** -0.5)
    return pl.pallas_call(
        kern,
        grid=(m // tm,),
        in_specs=[row(d), _resident((1, d)), _resident(w_in.shape), _resident(p_avg.shape),
                  _resident((1, d_qk)), _resident((1, d_qk)), _resident(cw.shape)],
        out_specs=[seq_t, seq_t, col, row(d_qk), col, row(d_b),
                   pl.BlockSpec((None, 2, d_b), lambda i: (i // tps, 0, 0))],
        out_shape=[jax.ShapeDtypeStruct((n_seq, d_qk, seq), F32),
                   jax.ShapeDtypeStruct((n_seq, d_a, seq), F32),
                   jax.ShapeDtypeStruct((d_qk, m), BF16),
                   jax.ShapeDtypeStruct((m, d_qk), BF16),
                   jax.ShapeDtypeStruct((d_a, m), BF16),
                   jax.ShapeDtypeStruct((m, d_b), BF16),
                   jax.ShapeDtypeStruct((n_seq, 2, d_b), F32)],
        scratch_shapes=[pltpu.VMEM((tm + SUBLANE, d_b), F32)],
        compiler_params=_params(("arbitrary",)),
        name="even_in_prompt",
    )(x, g.reshape(1, d), w_in, p_avg, qg, kg, cw)


def _even_in_sample_kernel(x_ref, g_ref, w_ref, p_ref, qg_ref, kg_ref, cw_ref, s0_ref, s1_ref,
                           q_ref, k_ref, v_ref, gc_ref, u_ref, *, d_qk, d_a, d_b, scale):
    h = _rms(x_ref[...], g_ref[...]).astype(BF16)
    z = _dot(h, w_ref[...])
    q_ref[...] = _group_rms(z[:, :d_qk], p_ref, qg_ref[...]) * scale
    k_ref[...] = _group_rms(z[:, d_qk:2 * d_qk], p_ref, kg_ref[...])
    o = 2 * d_qk
    v_ref[...] = z[:, o:o + d_a]
    o += d_a
    gate_b = z[:, o:o + d_b]
    u = z[:, o + d_b:o + 2 * d_b] * z[:, o + 2 * d_b:o + 3 * d_b]
    conv = cw_ref[0:1, :] * s0_ref[...] + cw_ref[1:2, :] * s1_ref[...] + cw_ref[2:3, :] * u
    gc_ref[...] = (gate_b * conv).astype(BF16)
    u_ref[...] = u


def _even_in_sample(x, g, w_in, p_avg, qg, kg, cw, s0, s1):
    m, d = x.shape
    d_qk = d_a = d_b = p_avg.shape[0]
    kern = functools.partial(_even_in_sample_kernel, d_qk=d_qk, d_a=d_a, d_b=d_b, scale=DH_QK ** -0.5)
    f = lambda dt: jax.ShapeDtypeStruct((m, d_qk), dt)
    return pl.pallas_call(
        kern,
        out_shape=[f(F32), f(F32), f(F32), f(BF16), f(F32)],
        compiler_params=pltpu.CompilerParams(vmem_limit_bytes=VMEM_LIMIT_BYTES),
        name="even_in_sample",
    )(x, g.reshape(1, d), w_in, p_avg, qg, kg, cw, s0, s1)


def _attn_prompt_kernel(cfar_ref, lam_ref, sg_ref, qt_ref, k_ref, vt_ref, bias_ref, o_ref,
                        m_scr, l_scr, acc_scr, *, lam_init):
    hp = pl.program_id(1)
    i = pl.program_id(2)
    tq = qt_ref.shape[1]
    n_sub = qt_ref.shape[0] // DH_QK

    qt = qt_ref[...]
    grp = lax.broadcasted_iota(jnp.int32, qt.shape, 0) // DH_QK
    q_sub = [jnp.where(grp == c, qt, jnp.zeros_like(qt)) for c in range(n_sub)]

    m_scr[...] = jnp.full(m_scr.shape, NEG, F32)
    l_scr[...] = jnp.zeros(l_scr.shape, F32)
    acc_scr[...] = jnp.zeros(acc_scr.shape, F32)

    def update(kt, vt, c, bias, cfar):
        head = c // 2
        s = _dot(kt, q_sub[c])
        if bias is not None:
            s = s + bias
        m_old = m_scr[c]
        smax = jnp.max(s, axis=0, keepdims=True)
        if cfar is not None:
            smax = smax + cfar
        m_new = jnp.maximum(m_old, smax)
        shift = m_new if cfar is None else m_new - cfar
        p = jnp.exp(s - shift)
        alpha = jnp.exp(m_old - m_new)
        l_scr[c] = alpha * l_scr[c] + jnp.sum(p, axis=0, keepdims=True)
        pv = _dot(vt[head * DH_V:(head + 1) * DH_V, :], p.astype(BF16))
        acc_scr[c] = alpha * acc_scr[c] + pv
        m_scr[c] = m_new

    def far_tile(j, carry):
        off = pl.multiple_of(j * tq, tq)
        kt = k_ref[pl.ds(off, tq), :]
        vt = vt_ref[:, pl.ds(off, tq)]
        for c in range(n_sub):
            update(kt, vt, c, None, cfar_ref[hp * 2 + c // 2])
        return carry

    lax.fori_loop(0, jnp.maximum(i - 1, 0), far_tile, 0)

    @pl.when(i >= 1)
    def _():
        off = pl.multiple_of((i - 1) * tq, tq)
        kt = k_ref[pl.ds(off, tq), :]
        vt = vt_ref[:, pl.ds(off, tq)]
        for c in range(n_sub):
            update(kt, vt, c, bias_ref[c // 2, 0], None)

    off = pl.multiple_of(i * tq, tq)
    kt = k_ref[pl.ds(off, tq), :]
    vt = vt_ref[:, pl.ds(off, tq)]
    for c in range(n_sub):
        update(kt, vt, c, bias_ref[c // 2, 1], None)

    lam_p = lam_ref[...]
    lam = (jnp.exp(jnp.sum(lam_p[0:1] * lam_p[1:2], axis=1, keepdims=True))
           - jnp.exp(jnp.sum(lam_p[2:3] * lam_p[3:4], axis=1, keepdims=True)) + lam_init)
    outs = []
    for head in range(n_sub // 2):
        o = acc_scr[2 * head] / l_scr[2 * head] - lam * (acc_scr[2 * head + 1] / l_scr[2 * head + 1])
        ms = jnp.mean(o * o, axis=0, keepdims=True)
        outs.append(o * lax.rsqrt(ms + EPS))
    o = jnp.concatenate(outs, axis=0) * sg_ref[...] * (1.0 - lam_init)
    o_ref[...] = o.T.astype(BF16)


def _attn_prompt(cfar, lam_p, sg_col, qt, k, vt, bias_tiles, n_seq, seq, lam_init):
    d_qk, m = qt.shape
    pair = 2 * 2 * DH_QK
    n_pair = d_qk // pair
    nq = seq // TQ
    kern = functools.partial(_attn_prompt_kernel, lam_init=lam_init)
    return pl.pallas_call(
        kern,
        grid=(n_seq, n_pair, nq),
        in_specs=[pl.BlockSpec(memory_space=pltpu.SMEM),
                  pl.BlockSpec(lam_p.shape, lambda b, h, i: (0, 0)),
                  pl.BlockSpec((pair, 1), lambda b, h, i: (0, 0)),
                  pl.BlockSpec((pair, TQ), lambda b, h, i: (h, b * nq + i)),
                  pl.BlockSpec((seq, pair), lambda b, h, i: (b, h)),
                  pl.BlockSpec((pair, seq), lambda b, h, i: (h, b)),
                  pl.BlockSpec((2, 2, TQ, TQ), lambda b, h, i: (h, 0, 0, 0))],
        out_specs=pl.BlockSpec((TQ, pair), lambda b, h, i: (b * nq + i, h)),
        out_shape=jax.ShapeDtypeStruct((m, d_qk), BF16),
        scratch_shapes=[pltpu.VMEM((4, 1, TQ), F32), pltpu.VMEM((4, 1, TQ), F32),
                        pltpu.VMEM((4, DH_V, TQ), F32)],
        compiler_params=_params(("parallel", "parallel", "arbitrary")),
        name="attn_prompt",
    )(cfar, lam_p, sg_col, qt, k, vt, bias_tiles)


def _attn_decode_kernel(pt_ref, q_ref, kn_ref, vn_ref, bt_ref, b0_ref, lam_ref, sg_ref, *rest,
                        pp, lam_init):
    k_refs = rest[:pp]
    v_refs = rest[pp:2 * pp]
    o_ref = rest[2 * pp]
    m_scr, l_scr, acc_scr = rest[2 * pp + 1:]
    del pt_ref
    step = pl.program_id(1)
    n_sub, d = acc_scr.shape
    page = k_refs[0].shape[1]

    row = lax.broadcasted_iota(jnp.int32, (n_sub, d), 0)
    col = lax.broadcasted_iota(jnp.int32, (n_sub, d), 1)
    qexp = jnp.where(col // DH_QK == row, jnp.broadcast_to(q_ref[...], (n_sub, d)), 0.0)

    @pl.when(step == 0)
    def _():
        m_scr[...] = jnp.sum(qexp * kn_ref[...], axis=1, keepdims=True) + b0_ref[...]
        l_scr[...] = jnp.ones(l_scr.shape, F32)
        acc_scr[...] = jnp.broadcast_to(vn_ref[...], (n_sub, d))

    qb = qexp.astype(BF16)
    s = jnp.concatenate([_dot(qb, k_refs[i][...].astype(BF16)) for i in range(pp)], axis=1)
    s = s + bt_ref[...]
    m_old = m_scr[...]
    m_new = jnp.maximum(m_old, jnp.max(s, axis=1, keepdims=True))
    p = jnp.exp(s - m_new)
    alpha = jnp.exp(m_old - m_new)
    l_scr[...] = alpha * l_scr[...] + jnp.sum(p, axis=1, keepdims=True)
    pb = p.astype(BF16)
    pv = jnp.zeros((n_sub, d), F32)
    for i in range(pp):
        pv = pv + lax.dot_general(pb[:, i * page:(i + 1) * page], v_refs[i][...].astype(BF16),
                                  (((1,), (1,)), ((), ())), preferred_element_type=F32)
    acc_scr[...] = alpha * acc_scr[...] + pv
    m_scr[...] = m_new

    @pl.when(step == pl.num_programs(1) - 1)
    def _():
        lam_p = lam_ref[...]
        lam = (jnp.exp(jnp.sum(lam_p[0:1] * lam_p[1:2], axis=1, keepdims=True))
               - jnp.exp(jnp.sum(lam_p[2:3] * lam_p[3:4], axis=1, keepdims=True)) + lam_init)
        o = acc_scr[...] / l_scr[...]
        head = col // DH_V
        a0 = jnp.sum(jnp.where(row == 2 * head, o, 0.0), axis=0, keepdims=True)
        a1 = jnp.sum(jnp.where(row == 2 * head + 1, o, 0.0), axis=0, keepdims=True)
        att = a0 - lam * a1
        sq = att * att
        head1 = head[0:1]
        r = jnp.zeros_like(att)
        for hh in range(d // DH_V):
            ms = jnp.sum(jnp.where(head1 == hh, sq, 0.0), axis=1, keepdims=True) * (1.0 / DH_V)
            r = jnp.where(head1 == hh, lax.rsqrt(ms + EPS), r)
        o_ref[...] = att * r * sg_ref[...] * (1.0 - lam_init)


def _attn_decode(page_table, q, kn, vn, bias_t, bias0, lam_p, sg_row, cache_kt, cache_vt, layer, lam_init):
    n_seq, n_pages = page_table.shape
    d = q.shape[-1]
    page = cache_kt.shape[-1]
    n_sub = d // DH_QK
    pp = PAGES_PER_STEP
    steps = n_pages // pp
    kern = functools.partial(_attn_decode_kernel, pp=pp, lam_init=lam_init)

    def page_spec(i):
        return pl.BlockSpec((None, None, d, page),
                            lambda b, s, pt: (layer, pt[b * n_pages + s * pp + i], 0, 0))

    tok = pl.BlockSpec((None, 1, d), lambda b, s, pt: (b, 0, 0))
    grid_spec = pltpu.PrefetchScalarGridSpec(
        num_scalar_prefetch=1,
        grid=(n_seq, steps),
        in_specs=[tok, tok, tok,
                  pl.BlockSpec((n_sub, pp * page), lambda b, s, pt: (0, s)),
                  pl.BlockSpec((n_sub, 1), lambda b, s, pt: (0, 0)),
                  pl.BlockSpec(lam_p.shape, lambda b, s, pt: (0, 0)),
                  pl.BlockSpec((1, d), lambda b, s, pt: (0, 0))]
                 + [page_spec(i) for i in range(pp)] + [page_spec(i) for i in range(pp)],
        out_specs=tok,
        scratch_shapes=[pltpu.VMEM((n_sub, 1), F32), pltpu.VMEM((n_sub, 1), F32),
                        pltpu.VMEM((n_sub, d), F32)],
    )
    out = pl.pallas_call(
        kern,
        grid_spec=grid_spec,
        out_shape=jax.ShapeDtypeStruct((n_seq, 1, d), F32),
        compiler_params=_params(("parallel", "arbitrary")),
        name="attn_decode",
    )(page_table.reshape(-1), q.reshape(n_seq, 1, d), kn.reshape(n_seq, 1, d), vn.reshape(n_seq, 1, d),
      bias_t, bias0, lam_p, sg_row, *([cache_kt] * pp), *([cache_vt] * pp))
    return out.reshape(n_seq, d)


def _even_out_kernel(x_ref, a_ref, gc_ref, w_ref, o_ref):
    d_a = a_ref.shape[1]
    y = _dot(a_ref[...].astype(BF16), w_ref[0:d_a, :]) + _dot(gc_ref[...], w_ref[d_a:, :])
    o_ref[...] = x_ref[...] + y


def _even_out(x, att, gc, w_out):
    m, d = x.shape
    tm = min(TM_ROWS, m)
    row = lambda width: pl.BlockSpec((tm, width), lambda i: (i, 0))
    return pl.pallas_call(
        _even_out_kernel,
        grid=(m // tm,),
        in_specs=[row(d), row(att.shape[1]), row(gc.shape[1]), _resident(w_out.shape)],
        out_specs=row(d),
        out_shape=jax.ShapeDtypeStruct((m, d), F32),
        compiler_params=_params(("parallel",)),
        name="even_out",
    )(x, att, gc, w_out)


def _layernorm_swish(c, g, b):
    mu = jnp.mean(c, axis=-1, keepdims=True)
    xc = c - mu
    y = xc * lax.rsqrt(jnp.mean(xc * xc, axis=-1, keepdims=True) + EPS) * g + b
    return jax.nn.silu(y)


def _odd_prompt_kernel(x_ref, g_ref, w1_ref, b1_ref, cw_ref, cb_ref, lg_ref, lb_ref, w2_ref, b2_ref,
                       o_ref, st_ref, u_scr, c_scr, *, tiles_per_seq, width):
    i = pl.program_id(0)
    tm, d = x_ref.shape
    x = x_ref[...]
    h = _rms(x, g_ref[...]).astype(BF16)
    z = _dot(h, w1_ref[...]) + b1_ref[...]
    u = z[:, :d] * jax.nn.sigmoid(z[:, d:])

    @pl.when(i % tiles_per_seq == 0)
    def _():
        u_scr[0:HALO, :] = jnp.zeros((HALO, d), F32)

    u_scr[HALO:HALO + tm, :] = u
    first = HALO - (width - 1)

    for base in range(0, tm, CONV_ROWS):
        acc = jnp.zeros((CONV_ROWS, d), F32)
        for j in range(width):
            acc = acc + cw_ref[j:j + 1, :] * u_scr[base + first + j:base + first + j + CONV_ROWS, :]
        c_scr[base:base + CONV_ROWS, :] = acc

    @pl.when(i % tiles_per_seq == tiles_per_seq - 1)
    def _():
        st_ref[...] = u_scr[tm + first:tm + HALO, :]

    u_scr[0:HALO, :] = u_scr[tm:tm + HALO, :]
    a = _layernorm_swish(c_scr[...] + cb_ref[...], lg_ref[...], lb_ref[...]).astype(BF16)
    o_ref[...] = x + _dot(a, w2_ref[...]) + b2_ref[...]


def _odd_prompt(x, g, w1, b1, cw, cb, lg, lb, w2, b2, n_seq, seq):
    m, d = x.shape
    width = cw.shape[0]
    tm = TM_ODD
    tps = seq // tm
    row = pl.BlockSpec((tm, d), lambda i: (i, 0))
    vec = lambda a: a.reshape(1, -1)
    kern = functools.partial(_odd_prompt_kernel, tiles_per_seq=tps, width=width)
    return pl.pallas_call(
        kern,
        grid=(m // tm,),
        in_specs=[row, _resident((1, d)), _resident(w1.shape), _resident((1, 2 * d)), _resident(cw.shape),
                  _resident((1, d)), _resident((1, d)), _resident((1, d)), _resident(w2.shape),
                  _resident((1, d))],
        out_specs=[row, pl.BlockSpec((None, width - 1, d), lambda i: (i // tps, 0, 0))],
        out_shape=[jax.ShapeDtypeStruct((m, d), F32),
                   jax.ShapeDtypeStruct((n_seq, width - 1, d), F32)],
        scratch_shapes=[pltpu.VMEM((tm + HALO, d), F32), pltpu.VMEM((tm, d), F32)],
        compiler_params=_params(("arbitrary",)),
        name="odd_prompt",
    )(x, vec(g), w1, vec(b1), cw, vec(cb), vec(lg), vec(lb), w2, vec(b2))


def _odd_sample_kernel(x_ref, g_ref, w1_ref, b1_ref, cw_ref, cb_ref, lg_ref, lb_ref, w2_ref, b2_ref,
                       st_ref, o_ref, nst_ref):
    n_hist = st_ref.shape[0]
    d = x_ref.shape[1]
    x = x_ref[...]
    h = _rms(x, g_ref[...]).astype(BF16)
    z = _dot(h, w1_ref[...]) + b1_ref[...]
    u = z[:, :d] * jax.nn.sigmoid(z[:, d:])
    c = cw_ref[n_hist:n_hist + 1, :] * u
    for j in range(n_hist):
        c = c + cw_ref[j:j + 1, :] * st_ref[j]
    for j in range(n_hist - 1):
        nst_ref[j] = st_ref[j + 1]
    nst_ref[n_hist - 1] = u
    a = _layernorm_swish(c + cb_ref[...], lg_ref[...], lb_ref[...]).astype(BF16)
    o_ref[...] = x + _dot(a, w2_ref[...]) + b2_ref[...]


def _odd_sample(x, g, w1, b1, cw, cb, lg, lb, w2, b2, state_t):
    m, d = x.shape
    vec = lambda a: a.reshape(1, -1)
    return pl.pallas_call(
        _odd_sample_kernel,
        out_shape=[jax.ShapeDtypeStruct((m, d), F32), jax.ShapeDtypeStruct(state_t.shape, F32)],
        compiler_params=pltpu.CompilerParams(vmem_limit_bytes=VMEM_LIMIT_BYTES),
        name="odd_sample",
    )(x, vec(g), w1, vec(b1), cw, vec(cb), vec(lg), vec(lb), w2, vec(b2), state_t)


def _bias_by_distance(rel_bias, dist):
    n = jnp.maximum(dist, 0)
    nf = jnp.maximum(n, 1).astype(F32)
    large = MAX_EXACT + (jnp.log(nf / MAX_EXACT) / math.log(MAX_DISTANCE / MAX_EXACT)
                         * (N_BUCKETS - MAX_EXACT)).astype(jnp.int32)
    large = jnp.minimum(large, N_BUCKETS - 1)
    return rel_bias[jnp.where(n < MAX_EXACT, n, large)]


def _prompt_bias_tiles(rel_bias):
    assert TQ >= FAR - 1
    kk = jnp.arange(TQ)[:, None]
    qq = jnp.arange(TQ)[None, :]
    near = _bias_by_distance(rel_bias, TQ + qq - kk)
    diag = jnp.where((qq >= kk)[..., None], _bias_by_distance(rel_bias, qq - kk), NEG)
    tiles = jnp.transpose(jnp.stack([near, diag]), (3, 0, 1, 2))
    far = _bias_by_distance(rel_bias, jnp.array([TQ + 1]))[0]
    return tiles, far


def kernel(x_prompt, x_sample, cache_k, cache_v, state_conv_b, state_conv_c, page_table, rel_bias, norm_ffn1, ffn1_w_gate, ffn1_w_up, ffn1_w_down, norm_mix, norm_ffn2, ffn2_w_gate, ffn2_w_up, ffn2_w_down, w_in_even, w_out_even, q_norm_gain, k_norm_gain, lambda_q1, lambda_k1, lambda_q2, lambda_k2, subln_gain, conv_b_w, w_pw1, b_pw1, conv_c_w, conv_c_b, ln_c_gain, ln_c_bias, w_pw2, b_pw2):
    bp, tp, d = x_prompt.shape
    bs = x_sample.shape[0]
    depth = norm_ffn1.shape[0]
    n_even, n_phys, page, n_sub, dh_qk = cache_k.shape
    assert dh_qk == DH_QK and cache_v.shape[-1] == DH_V and x_sample.shape[1] == 1
    d_qk = n_sub * dh_qk
    d_a = cache_v.shape[3] * DH_V
    past_len = page_table.shape[1] * page
    assert tp % TM_ROWS == 0 and tp % TM_ODD == 0 and tp % TQ == 0 and tp >= 2 * TQ
    assert page_table.shape[1] % PAGES_PER_STEP == 0

    bf = lambda w: w.astype(BF16)
    xp = x_prompt.reshape(bp * tp, d)
    xs = x_sample.reshape(bs, d)

    cache_kt = jnp.transpose(cache_k, (0, 1, 3, 4, 2)).reshape(n_even, n_phys, d_qk, page)
    cache_vt = jnp.transpose(cache_v, (0, 1, 3, 4, 2)).reshape(n_even, n_phys, d_a, page)
    state_c_t = jnp.transpose(state_conv_c, (0, 2, 1, 3))

    p_avg = jnp.kron(jnp.eye(d_qk // DH_QK, dtype=F32), jnp.full((DH_QK, DH_QK), 1.0 / DH_QK, F32)).astype(BF16)
    bias_tiles, bias_far = _prompt_bias_tiles(rel_bias)
    k_pos = jnp.arange(past_len + 1)
    bias_dec = jnp.repeat(_bias_by_distance(rel_bias, past_len - k_pos), 2, axis=1).T
    bias_dec_past, bias_dec_new = bias_dec[:, :past_len], bias_dec[:, past_len:]

    kp_l, vp_l, ks_l, vs_l, cbp_l, cbs_l, ccp_l, ccs_l = [], [], [], [], [], [], [], []
    for li in range(depth):
        w1 = (bf(ffn1_w_gate[li]), bf(ffn1_w_up[li]), bf(ffn1_w_down[li]))
        xp = _ffn(xp, norm_ffn1[li], *w1)
        xs = _ffn(xs, norm_ffn1[li], *w1)
        if li % 2 == 0:
            e = li // 2
            lam_init = 0.8 - 0.6 * math.exp(-0.3 * li)
            w_in, w_out = bf(w_in_even[e]), bf(w_out_even[e])
            qg = jnp.tile(q_norm_gain[e], d_qk // DH_QK).reshape(1, d_qk)
            kg = jnp.tile(k_norm_gain[e], d_qk // DH_QK).reshape(1, d_qk)
            lam_p = jnp.stack([lambda_q1[e], lambda_k1[e], lambda_q2[e], lambda_k2[e]])
            sg = jnp.tile(subln_gain[e], d_a // DH_V)

            kt, vt, qtb, kb, vtb, gcp, cbp = _even_in_prompt(
                xp, norm_mix[li], w_in, p_avg, qg, kg, conv_b_w[e], bp, tp)
            attp = _attn_prompt(bias_far, lam_p, sg[:2 * DH_V].reshape(-1, 1), qtb, kb, vtb, bias_tiles,
                                bp, tp, lam_init)
            xp = _even_out(xp, attp, gcp, w_out)

            qs, kn, vn, gcs, us = _even_in_sample(
                xs, norm_mix[li], w_in, p_avg, qg, kg, conv_b_w[e],
                state_conv_b[e, :, 0], state_conv_b[e, :, 1])
            atts = _attn_decode(page_table, qs, kn, vn, bias_dec_past, bias_dec_new, lam_p,
                                sg.reshape(1, -1), cache_kt, cache_vt, e, lam_init)
            xs = _even_out(xs, atts, gcs, w_out)

            kp_l.append(kt)
            vp_l.append(vt)
            ks_l.append(kn)
            vs_l.append(vn)
            cbp_l.append(cbp)
            cbs_l.append(jnp.stack([state_conv_b[e, :, 1], us], axis=1))
        else:
            o = li // 2
            prm = (norm_mix[li], bf(w_pw1[o]), b_pw1[o], conv_c_w[o], conv_c_b[o], ln_c_gain[o], ln_c_bias[o],
                   bf(w_pw2[o]), b_pw2[o])
            xp, ccp = _odd_prompt(xp, *prm, bp, tp)
            xs, ccs = _odd_sample(xs, *prm, state_c_t[o])
            ccp_l.append(ccp)
            ccs_l.append(ccs)
        w2 = (bf(ffn2_w_gate[li]), bf(ffn2_w_up[li]), bf(ffn2_w_down[li]))
        xp = _ffn(xp, norm_ffn2[li], *w2)
        xs = _ffn(xs, norm_ffn2[li], *w2)

    n_h = d_a // DH_V
    new_k_prompt = jnp.transpose(jnp.stack(kp_l).reshape(n_even, bp, n_sub, DH_QK, tp), (0, 1, 4, 2, 3))
    new_v_prompt = jnp.transpose(jnp.stack(vp_l).reshape(n_even, bp, n_h, DH_V, tp), (0, 1, 4, 2, 3))
    return (xp.reshape(bp, tp, d), xs.reshape(bs, 1, d),
            new_k_prompt, new_v_prompt,
            jnp.stack(ks_l).reshape(n_even, bs, 1, n_sub, DH_QK),
            jnp.stack(vs_l).reshape(n_even, bs, 1, n_h, DH_V),
            jnp.stack(cbp_l), jnp.stack(cbs_l), jnp.stack(ccp_l),
            jnp.transpose(jnp.stack(ccs_l), (0, 2, 1, 3)))
```

```python
import functools
import math

import jax
import jax.numpy as jnp
from jax import lax
from jax.experimental import pallas as pl
from jax.experimental.pallas import tpu as pltpu

F32 = jnp.float32
BF16 = jnp.bfloat16
EPS = 1e-6
NEG = -1e30

LANE = 128
SUBLANE = 8
VMEM_LIMIT_BYTES = 56 * 1024 * 1024

DH_QK = 32
DH_V = 64
N_BUCKETS = 32
MAX_EXACT = N_BUCKETS // 2
MAX_DISTANCE = 128
FAR = MAX_DISTANCE

TM_ROWS = 512
TM_ODD = 256
TQ = 256
SOFTMAX_SUM_ROWS = 16
HALO = 32
CONV_UNROLL = 8
LOG2E = math.log2(math.e)
PAGES_PER_STEP = 8


def _params(sem):
    return pltpu.CompilerParams(dimension_semantics=sem, vmem_limit_bytes=VMEM_LIMIT_BYTES)


def _resident(shape):
    n = len(shape)
    return pl.BlockSpec(shape, lambda *_: (0,) * n, pipeline_mode=pl.Buffered(1))


def _rms(x, g):
    return x * lax.rsqrt(jnp.mean(x * x, axis=-1, keepdims=True) + EPS) * g


def _dot(a, b):
    return jnp.dot(a, b, preferred_element_type=F32)


def _group_rms(z, p_ref, gain):
    sq = z * z
    hi = sq.astype(BF16)
    lo = (sq - hi.astype(F32)).astype(BF16)
    ms = _dot(hi, p_ref[...]) + _dot(lo, p_ref[...])
    return z * lax.rsqrt(ms + EPS) * gain


def _ffn_kernel(x_ref, g_ref, wg_ref, wu_ref, wd_ref, o_ref, *, n_chunks):
    x = x_ref[...]
    h = _rms(x, g_ref[...]).astype(BF16)
    fc = wg_ref.shape[1] // n_chunks
    y = jnp.zeros(x.shape, F32)
    for c in range(n_chunks):
        gate = _dot(h, wg_ref[:, c * fc:(c + 1) * fc])
        up = _dot(h, wu_ref[:, c * fc:(c + 1) * fc])
        a = (jax.nn.silu(gate) * up).astype(BF16)
        y = y + _dot(a, wd_ref[c * fc:(c + 1) * fc, :])
    o_ref[...] = x + 0.5 * y


def _ffn(x, g, wg, wu, wd):
    m, d = x.shape
    f = wg.shape[1]
    tm = min(TM_ROWS, m)
    row = pl.BlockSpec((tm, d), lambda i: (i, 0))
    return pl.pallas_call(
        functools.partial(_ffn_kernel, n_chunks=2),
        grid=(m // tm,),
        in_specs=[row, _resident((1, d)), _resident((d, f)), _resident((d, f)), _resident((f, d))],
        out_specs=row,
        out_shape=jax.ShapeDtypeStruct((m, d), F32),
        compiler_params=_params(("parallel",)),
        name="ffn",
    )(x, g.reshape(1, d), wg, wu, wd)


def _even_in_prompt_kernel(x_ref, g_ref, w_ref, p_ref, qg_ref, kg_ref, cw_ref,
                           kt_ref, vt_ref, qtb_ref, kb_ref, vtb_ref, gc_ref, st_ref,
                           u_scr, *, tiles_per_seq, d_qk, d_a, d_b, scale):
    i = pl.program_id(0)
    tm = x_ref.shape[0]
    h = _rms(x_ref[...], g_ref[...]).astype(BF16)
    z = _dot(h, w_ref[...])
    q = _group_rms(z[:, :d_qk], p_ref, qg_ref[...]) * scale
    k = _group_rms(z[:, d_qk:2 * d_qk], p_ref, kg_ref[...])
    o = 2 * d_qk
    v = z[:, o:o + d_a]
    o += d_a
    gate_b = z[:, o:o + d_b]
    u = z[:, o + d_b:o + 2 * d_b] * z[:, o + 2 * d_b:o + 3 * d_b]

    kt = k.T
    vt = v.T
    kt_ref[...] = kt
    vt_ref[...] = vt
    qtb_ref[...] = q.T.astype(BF16)
    kb_ref[...] = k.astype(BF16)
    vtb_ref[...] = vt.astype(BF16)

    @pl.when(i % tiles_per_seq == 0)
    def _():
        u_scr[0:SUBLANE, :] = jnp.zeros((SUBLANE, d_b), F32)

    u_scr[SUBLANE:SUBLANE + tm, :] = u
    conv = (cw_ref[0:1, :] * u_scr[SUBLANE - 2:SUBLANE - 2 + tm, :]
            + cw_ref[1:2, :] * u_scr[SUBLANE - 1:SUBLANE - 1 + tm, :]
            + cw_ref[2:3, :] * u)
    gc_ref[...] = (gate_b * conv).astype(BF16)
    st_ref[...] = u_scr[tm + SUBLANE - 2:tm + SUBLANE, :]
    u_scr[0:SUBLANE, :] = u_scr[tm:tm + SUBLANE, :]


def _even_in_prompt(x, g, w_in, p_avg, qg, kg, cw, n_seq, seq):
    m, d = x.shape
    d_qk = d_a = d_b = p_avg.shape[0]
    tm = TM_ROWS
    tps = seq // tm
    row = lambda width, dt=None: pl.BlockSpec((tm, width), lambda i: (i, 0))
    col = pl.BlockSpec((d_qk, tm), lambda i: (0, i))
    seq_t = pl.BlockSpec((None, d_qk, tm), lambda i: (i // tps, 0, i % tps))
    kern = functools.partial(_even_in_prompt_kernel, tiles_per_seq=tps, d_qk=d_qk, d_a=d_a, d_b=d_b,
                             scale=DH_QK ** -0.5 * LOG2E)
    return pl.pallas_call(
        kern,
        grid=(m // tm,),
        in_specs=[row(d), _resident((1, d)), _resident(w_in.shape), _resident(p_avg.shape),
                  _resident((1, d_qk)), _resident((1, d_qk)), _resident(cw.shape)],
        out_specs=[seq_t, seq_t, col, row(d_qk), col, row(d_b),
                   pl.BlockSpec((None, 2, d_b), lambda i: (i // tps, 0, 0))],
        out_shape=[jax.ShapeDtypeStruct((n_seq, d_qk, seq), F32),
                   jax.ShapeDtypeStruct((n_seq, d_a, seq), F32),
                   jax.ShapeDtypeStruct((d_qk, m), BF16),
                   jax.ShapeDtypeStruct((m, d_qk), BF16),
                   jax.ShapeDtypeStruct((d_a, m), BF16),
                   jax.ShapeDtypeStruct((m, d_b), BF16),
                   jax.ShapeDtypeStruct((n_seq, 2, d_b), F32)],
        scratch_shapes=[pltpu.VMEM((tm + SUBLANE, d_b), F32)],
        compiler_params=_params(("arbitrary",)),
        name="even_in_prompt",
    )(x, g.reshape(1, d), w_in, p_avg, qg, kg, cw)


def _even_in_sample_kernel(x_ref, g_ref, w_ref, p_ref, qg_ref, kg_ref, cw_ref, s0_ref, s1_ref,
                           q_ref, k_ref, v_ref, gc_ref, u_ref, *, d_qk, d_a, d_b, scale):
    h = _rms(x_ref[...], g_ref[...]).astype(BF16)
    z = _dot(h, w_ref[...])
    q_ref[...] = _group_rms(z[:, :d_qk], p_ref, qg_ref[...]) * scale
    k_ref[...] = _group_rms(z[:, d_qk:2 * d_qk], p_ref, kg_ref[...])
    o = 2 * d_qk
    v_ref[...] = z[:, o:o + d_a]
    o += d_a
    gate_b = z[:, o:o + d_b]
    u = z[:, o + d_b:o + 2 * d_b] * z[:, o + 2 * d_b:o + 3 * d_b]
    conv = cw_ref[0:1, :] * s0_ref[...] + cw_ref[1:2, :] * s1_ref[...] + cw_ref[2:3, :] * u
    gc_ref[...] = (gate_b * conv).astype(BF16)
    u_ref[...] = u


def _even_in_sample(x, g, w_in, p_avg, qg, kg, cw, s0, s1):
    m, d = x.shape
    d_qk = d_a = d_b = p_avg.shape[0]
    kern = functools.partial(_even_in_sample_kernel, d_qk=d_qk, d_a=d_a, d_b=d_b, scale=DH_QK ** -0.5)
    f = lambda dt: jax.ShapeDtypeStruct((m, d_qk), dt)
    return pl.pallas_call(
        kern,
        out_shape=[f(F32), f(F32), f(F32), f(BF16), f(F32)],
        compiler_params=pltpu.CompilerParams(vmem_limit_bytes=VMEM_LIMIT_BYTES),
        name="even_in_sample",
    )(x, g.reshape(1, d), w_in, p_avg, qg, kg, cw, s0, s1)


def _attn_prompt_kernel(cfar_ref, lam_ref, sg_ref, qt_ref, k_ref, vt_ref, bias_ref, o_ref,
                        s_scr, p_scr, *, lam_init, tq):
    hp = pl.program_id(1)
    seq = k_ref.shape[0]
    n_sub = qt_ref.shape[0] // DH_QK
    n_head = n_sub // 2

    lam_p = lam_ref[...]
    lam = (jnp.exp(jnp.sum(lam_p[0:1] * lam_p[1:2], axis=1, keepdims=True))
           - jnp.exp(jnp.sum(lam_p[2:3] * lam_p[3:4], axis=1, keepdims=True)) + lam_init)
    cfar = jnp.concatenate([jnp.full((1, 2 * tq), cfar_ref[hp * n_head + h], F32) for h in range(n_head)],
                           axis=1)
    grp = lax.broadcasted_iota(jnp.int32, (qt_ref.shape[0], tq), 0) // DH_QK

    for n in range(seq // tq):
        buf = n % 2
        qt = qt_ref[:, n * tq:(n + 1) * tq]
        q_blk = jnp.concatenate([jnp.where(grp == c, qt, jnp.zeros_like(qt)) for c in range(n_sub)], axis=1)

        m = None
        for t in range(n + 1):
            rows = slice(t * tq, (t + 1) * tq)
            s = _dot(k_ref[rows, :], q_blk)
            if t >= n - 1:
                kind = 1 if t == n else 0
                s = jnp.concatenate([s[:, c * tq:(c + 1) * tq] + bias_ref[c // 2, kind] for c in range(n_sub)],
                                    axis=1)
                smax = jnp.max(s, axis=0, keepdims=True)
            else:
                smax = jnp.max(s, axis=0, keepdims=True) + cfar
            m = smax if m is None else jnp.maximum(m, smax)
            s_scr[buf, rows, :] = s

        m_far = m - cfar
        for t in range(n + 1):
            rows = slice(t * tq, (t + 1) * tq)
            shift = m if t >= n - 1 else m_far
            p_scr[buf, rows, :] = jnp.exp2(s_scr[buf, rows, :] - shift).astype(BF16)

        keys = (n + 1) * tq
        ones = jnp.ones((SOFTMAX_SUM_ROWS, keys), BF16)
        outs = []
        for h in range(n_head):
            vte = jnp.concatenate([vt_ref[h * DH_V:(h + 1) * DH_V, 0:keys], ones], axis=0)
            pv = _dot(vte, p_scr[buf, 0:keys, h * 2 * tq:(h + 1) * 2 * tq])
            sub = [pv[0:DH_V, c * tq:(c + 1) * tq] / pv[DH_V:DH_V + 1, c * tq:(c + 1) * tq] for c in range(2)]
            o = sub[0] - lam * sub[1]
            ms = jnp.mean(o * o, axis=0, keepdims=True)
            outs.append(o * lax.rsqrt(ms + EPS))
        o = jnp.concatenate(outs, axis=0) * sg_ref[...] * (1.0 - lam_init)
        o_ref[n * tq:(n + 1) * tq, :] = o.T.astype(BF16)


def _attn_prompt(cfar, lam_p, sg_col, qt, k, vt, bias_tiles, n_seq, seq, lam_init):
    d_qk, m = qt.shape
    pair = 2 * 2 * DH_QK
    n_pair = d_qk // pair
    n_sub = pair // DH_QK
    kern = functools.partial(_attn_prompt_kernel, lam_init=lam_init, tq=TQ)
    return pl.pallas_call(
        kern,
        grid=(n_seq, n_pair),
        in_specs=[pl.BlockSpec(memory_space=pltpu.SMEM),
                  pl.BlockSpec(lam_p.shape, lambda b, h: (0, 0)),
                  pl.BlockSpec((pair, 1), lambda b, h: (0, 0)),
                  pl.BlockSpec((pair, seq), lambda b, h: (h, b)),
                  pl.BlockSpec((seq, pair), lambda b, h: (b, h)),
                  pl.BlockSpec((pair, seq), lambda b, h: (h, b)),
                  pl.BlockSpec((2, 2, TQ, TQ), lambda b, h: (h, 0, 0, 0))],
        out_specs=pl.BlockSpec((seq, pair), lambda b, h: (b, h)),
        out_shape=jax.ShapeDtypeStruct((m, d_qk), BF16),
        scratch_shapes=[pltpu.VMEM((2, seq, n_sub * TQ), F32), pltpu.VMEM((2, seq, n_sub * TQ), BF16)],
        compiler_params=_params(("parallel", "parallel")),
        name="attn_prompt",
    )(cfar, lam_p, sg_col, qt, k, vt, bias_tiles)


def _attn_decode_kernel(pt_ref, q_ref, kn_ref, vn_ref, bt_ref, b0_ref, lam_ref, sg_ref, *rest,
                        pp, lam_init):
    k_refs = rest[:pp]
    v_refs = rest[pp:2 * pp]
    o_ref = rest[2 * pp]
    m_scr, l_scr, acc_scr = rest[2 * pp + 1:]
    del pt_ref
    step = pl.program_id(1)
    n_sub, d = acc_scr.shape
    page = k_refs[0].shape[1]

    row = lax.broadcasted_iota(jnp.int32, (n_sub, d), 0)
    col = lax.broadcasted_iota(jnp.int32, (n_sub, d), 1)
    qexp = jnp.where(col // DH_QK == row, jnp.broadcast_to(q_ref[...], (n_sub, d)), 0.0)

    @pl.when(step == 0)
    def _():
        m_scr[...] = jnp.sum(qexp * kn_ref[...], axis=1, keepdims=True) + b0_ref[...]
        l_scr[...] = jnp.ones(l_scr.shape, F32)
        acc_scr[...] = jnp.broadcast_to(vn_ref[...], (n_sub, d))

    qb = qexp.astype(BF16)
    s = jnp.concatenate([_dot(qb, k_refs[i][...].astype(BF16)) for i in range(pp)], axis=1)
    s = s + bt_ref[...]
    m_old = m_scr[...]
    m_new = jnp.maximum(m_old, jnp.max(s, axis=1, keepdims=True))
    p = jnp.exp(s - m_new)
    alpha = jnp.exp(m_old - m_new)
    l_scr[...] = alpha * l_scr[...] + jnp.sum(p, axis=1, keepdims=True)
    pb = p.astype(BF16)
    pv = jnp.zeros((n_sub, d), F32)
    for i in range(pp):
        pv = pv + lax.dot_general(pb[:, i * page:(i + 1) * page], v_refs[i][...].astype(BF16),
                                  (((1,), (1,)), ((), ())), preferred_element_type=F32)
    acc_scr[...] = alpha * acc_scr[...] + pv
    m_scr[...] = m_new

    @pl.when(step == pl.num_programs(1) - 1)
    def _():
        lam_p = lam_ref[...]
        lam = (jnp.exp(jnp.sum(lam_p[0:1] * lam_p[1:2], axis=1, keepdims=True))
               - jnp.exp(jnp.sum(lam_p[2:3] * lam_p[3:4], axis=1, keepdims=True)) + lam_init)
        o = acc_scr[...] / l_scr[...]
        head = col // DH_V
        a0 = jnp.sum(jnp.where(row == 2 * head, o, 0.0), axis=0, keepdims=True)
        a1 = jnp.sum(jnp.where(row == 2 * head + 1, o, 0.0), axis=0, keepdims=True)
        att = a0 - lam * a1
        sq = att * att
        head1 = head[0:1]
        r = jnp.zeros_like(att)
        for hh in range(d // DH_V):
            ms = jnp.sum(jnp.where(head1 == hh, sq, 0.0), axis=1, keepdims=True) * (1.0 / DH_V)
            r = jnp.where(head1 == hh, lax.rsqrt(ms + EPS), r)
        o_ref[...] = att * r * sg_ref[...] * (1.0 - lam_init)


def _attn_decode(page_table, q, kn, vn, bias_t, bias0, lam_p, sg_row, cache_kt, cache_vt, layer, lam_init):
    n_seq, n_pages = page_table.shape
    d = q.shape[-1]
    page = cache_kt.shape[-1]
    n_sub = d // DH_QK
    pp = PAGES_PER_STEP
    steps = n_pages // pp
    kern = functools.partial(_attn_decode_kernel, pp=pp, lam_init=lam_init)

    def page_spec(i):
        return pl.BlockSpec((None, None, d, page),
                            lambda b, s, pt: (layer, pt[b * n_pages + s * pp + i], 0, 0))

    tok = pl.BlockSpec((None, 1, d), lambda b, s, pt: (b, 0, 0))
    grid_spec = pltpu.PrefetchScalarGridSpec(
        num_scalar_prefetch=1,
        grid=(n_seq, steps),
        in_specs=[tok, tok, tok,
                  pl.BlockSpec((n_sub, pp * page), lambda b, s, pt: (0, s)),
                  pl.BlockSpec((n_sub, 1), lambda b, s, pt: (0, 0)),
                  pl.BlockSpec(lam_p.shape, lambda b, s, pt: (0, 0)),
                  pl.BlockSpec((1, d), lambda b, s, pt: (0, 0))]
                 + [page_spec(i) for i in range(pp)] + [page_spec(i) for i in range(pp)],
        out_specs=tok,
        scratch_shapes=[pltpu.VMEM((n_sub, 1), F32), pltpu.VMEM((n_sub, 1), F32),
                        pltpu.VMEM((n_sub, d), F32)],
    )
    out = pl.pallas_call(
        kern,
        grid_spec=grid_spec,
        out_shape=jax.ShapeDtypeStruct((n_seq, 1, d), F32),
        compiler_params=_params(("parallel", "arbitrary")),
        name="attn_decode",
    )(page_table.reshape(-1), q.reshape(n_seq, 1, d), kn.reshape(n_seq, 1, d), vn.reshape(n_seq, 1, d),
      bias_t, bias0, lam_p, sg_row, *([cache_kt] * pp), *([cache_vt] * pp))
    return out.reshape(n_seq, d)


def _even_out_kernel(x_ref, a_ref, gc_ref, w_ref, o_ref):
    d_a = a_ref.shape[1]
    y = _dot(a_ref[...].astype(BF16), w_ref[0:d_a, :]) + _dot(gc_ref[...], w_ref[d_a:, :])
    o_ref[...] = x_ref[...] + y


def _even_out(x, att, gc, w_out):
    m, d = x.shape
    tm = min(TM_ROWS, m)
    row = lambda width: pl.BlockSpec((tm, width), lambda i: (i, 0))
    return pl.pallas_call(
        _even_out_kernel,
        grid=(m // tm,),
        in_specs=[row(d), row(att.shape[1]), row(gc.shape[1]), _resident(w_out.shape)],
        out_specs=row(d),
        out_shape=jax.ShapeDtypeStruct((m, d), F32),
        compiler_params=_params(("parallel",)),
        name="even_out",
    )(x, att, gc, w_out)


def _layernorm_swish(c, g, b):
    mu = jnp.mean(c, axis=-1, keepdims=True)
    xc = c - mu
    y = xc * lax.rsqrt(jnp.mean(xc * xc, axis=-1, keepdims=True) + EPS) * g + b
    return jax.nn.silu(y)


def _odd_prompt_kernel(x_ref, g_ref, w1_ref, b1_ref, cw_ref, cb_ref, lg_ref, lb_ref, w2_ref, b2_ref,
                       o_ref, st_ref, u_scr, c_scr, *, tiles_per_seq, width):
    i = pl.program_id(0)
    tm, d = x_ref.shape
    nl = d // LANE
    x = x_ref[...]
    h = _rms(x, g_ref[...]).astype(BF16)
    z = _dot(h, w1_ref[...]) + b1_ref[...]
    u = z[:, :d] * jax.nn.sigmoid(z[:, d:])

    @pl.when(i % tiles_per_seq == 0)
    def _():
        u_scr[0:HALO * nl, :] = jnp.zeros((HALO * nl, LANE), F32)

    for l in range(nl):
        u_scr[pl.ds(HALO * nl + l, tm, stride=nl), :] = u[:, l * LANE:(l + 1) * LANE]

    @pl.when(i % tiles_per_seq == tiles_per_seq - 1)
    def _():
        st_ref[...] = u[tm - (width - 1):, :]

    first = HALO - (width - 1)
    taps = [cw_ref[j] for j in range(width)]

    def chunk(r, carry):
        t0 = r * CONV_UNROLL
        slab = lambda t: pl.ds(pl.multiple_of(t * nl, nl), nl)
        rows = [u_scr[slab(t0 + first + k), :] for k in range(CONV_UNROLL + width - 1)]
        for o in range(CONV_UNROLL):
            acc = taps[0] * rows[o]
            for j in range(1, width):
                acc = acc + taps[j] * rows[o + j]
            c_scr[slab(t0 + o), :] = acc
        return carry

    lax.fori_loop(0, tm // CONV_UNROLL, chunk, 0)

    u_scr[0:HALO * nl, :] = u_scr[tm * nl:(tm + HALO) * nl, :]
    c = jnp.concatenate([c_scr[pl.ds(l, tm, stride=nl), :] for l in range(nl)], axis=1)
    a = _layernorm_swish(c + cb_ref[...], lg_ref[...], lb_ref[...]).astype(BF16)
    o_ref[...] = x + _dot(a, w2_ref[...]) + b2_ref[...]


def _odd_prompt(x, g, w1, b1, cw, cb, lg, lb, w2, b2, n_seq, seq):
    m, d = x.shape
    width = cw.shape[0]
    tm = TM_ODD
    tps = seq // tm
    row = pl.BlockSpec((tm, d), lambda i: (i, 0))
    vec = lambda a: a.reshape(1, -1)
    nl = d // LANE
    assert tm >= width - 1 and tm % CONV_UNROLL == 0 and HALO >= width - 1
    kern = functools.partial(_odd_prompt_kernel, tiles_per_seq=tps, width=width)
    return pl.pallas_call(
        kern,
        grid=(m // tm,),
        in_specs=[row, _resident((1, d)), _resident(w1.shape), _resident((1, 2 * d)),
                  _resident((width, nl, LANE)),
                  _resident((1, d)), _resident((1, d)), _resident((1, d)), _resident(w2.shape),
                  _resident((1, d))],
        out_specs=[row, pl.BlockSpec((None, width - 1, d), lambda i: (i // tps, 0, 0))],
        out_shape=[jax.ShapeDtypeStruct((m, d), F32),
                   jax.ShapeDtypeStruct((n_seq, width - 1, d), F32)],
        scratch_shapes=[pltpu.VMEM(((tm + HALO) * nl, LANE), F32), pltpu.VMEM((tm * nl, LANE), F32)],
        compiler_params=_params(("arbitrary",)),
        name="odd_prompt",
    )(x, vec(g), w1, vec(b1), cw.reshape(width, nl, LANE), vec(cb), vec(lg), vec(lb), w2, vec(b2))


def _odd_sample_kernel(x_ref, g_ref, w1_ref, b1_ref, cw_ref, cb_ref, lg_ref, lb_ref, w2_ref, b2_ref,
                       st_ref, o_ref, nst_ref):
    n_hist = st_ref.shape[0]
    d = x_ref.shape[1]
    x = x_ref[...]
    h = _rms(x, g_ref[...]).astype(BF16)
    z = _dot(h, w1_ref[...]) + b1_ref[...]
    u = z[:, :d] * jax.nn.sigmoid(z[:, d:])
    c = cw_ref[n_hist:n_hist + 1, :] * u
    for j in range(n_hist):
        c = c + cw_ref[j:j + 1, :] * st_ref[j]
    for j in range(n_hist - 1):
        nst_ref[j] = st_ref[j + 1]
    nst_ref[n_hist - 1] = u
    a = _layernorm_swish(c + cb_ref[...], lg_ref[...], lb_ref[...]).astype(BF16)
    o_ref[...] = x + _dot(a, w2_ref[...]) + b2_ref[...]


def _odd_sample(x, g, w1, b1, cw, cb, lg, lb, w2, b2, state_t):
    m, d = x.shape
    vec = lambda a: a.reshape(1, -1)
    return pl.pallas_call(
        _odd_sample_kernel,
        out_shape=[jax.ShapeDtypeStruct((m, d), F32), jax.ShapeDtypeStruct(state_t.shape, F32)],
        compiler_params=pltpu.CompilerParams(vmem_limit_bytes=VMEM_LIMIT_BYTES),
        name="odd_sample",
    )(x, vec(g), w1, vec(b1), cw, vec(cb), vec(lg), vec(lb), w2, vec(b2), state_t)


def _bias_by_distance(rel_bias, dist):
    n = jnp.maximum(dist, 0)
    nf = jnp.maximum(n, 1).astype(F32)
    large = MAX_EXACT + (jnp.log(nf / MAX_EXACT) / math.log(MAX_DISTANCE / MAX_EXACT)
                         * (N_BUCKETS - MAX_EXACT)).astype(jnp.int32)
    large = jnp.minimum(large, N_BUCKETS - 1)
    return rel_bias[jnp.where(n < MAX_EXACT, n, large)]


def _prompt_bias_tiles(rel_bias):
    assert TQ >= FAR - 1
    n_head = rel_bias.shape[1]
    by_dist = _bias_by_distance(rel_bias, jnp.arange(2 * TQ)).T * LOG2E
    diag = jnp.concatenate([jnp.full((n_head, TQ - 1), NEG, F32), by_dist[:, :TQ]], axis=1)
    near = by_dist[:, 1:]

    def toeplitz(a):
        period = a.shape[1]
        rows = jnp.tile(a, (1, TQ + 1))[:, :TQ * (period + 1)].reshape(n_head, TQ, period + 1)
        return rows[:, ::-1, :TQ]

    tiles = jnp.stack([toeplitz(near), toeplitz(diag)], axis=1)
    return tiles, by_dist[:, TQ + 1]


def kernel(x_prompt, x_sample, cache_k, cache_v, state_conv_b, state_conv_c, page_table, rel_bias, norm_ffn1, ffn1_w_gate, ffn1_w_up, ffn1_w_down, norm_mix, norm_ffn2, ffn2_w_gate, ffn2_w_up, ffn2_w_down, w_in_even, w_out_even, q_norm_gain, k_norm_gain, lambda_q1, lambda_k1, lambda_q2, lambda_k2, subln_gain, conv_b_w, w_pw1, b_pw1, conv_c_w, conv_c_b, ln_c_gain, ln_c_bias, w_pw2, b_pw2):
    bp, tp, d = x_prompt.shape
    bs = x_sample.shape[0]
    depth = norm_ffn1.shape[0]
    n_even, n_phys, page, n_sub, dh_qk = cache_k.shape
    assert dh_qk == DH_QK and cache_v.shape[-1] == DH_V and x_sample.shape[1] == 1
    d_qk = n_sub * dh_qk
    d_a = cache_v.shape[3] * DH_V
    past_len = page_table.shape[1] * page
    assert tp % TM_ROWS == 0 and tp % TM_ODD == 0 and tp % TQ == 0 and tp >= 2 * TQ
    assert page_table.shape[1] % PAGES_PER_STEP == 0

    bf = lambda w: w.astype(BF16)
    xp = x_prompt.reshape(bp * tp, d)
    xs = x_sample.reshape(bs, d)

    cache_kt = jnp.transpose(cache_k, (0, 1, 3, 4, 2)).reshape(n_even, n_phys, d_qk, page)
    cache_vt = jnp.transpose(cache_v, (0, 1, 3, 4, 2)).reshape(n_even, n_phys, d_a, page)
    state_c_t = jnp.transpose(state_conv_c, (0, 2, 1, 3))

    p_avg = jnp.kron(jnp.eye(d_qk // DH_QK, dtype=F32), jnp.full((DH_QK, DH_QK), 1.0 / DH_QK, F32)).astype(BF16)
    bias_tiles, bias_far = _prompt_bias_tiles(rel_bias)
    k_pos = jnp.arange(past_len + 1)
    bias_dec = jnp.repeat(_bias_by_distance(rel_bias, past_len - k_pos), 2, axis=1).T
    bias_dec_past, bias_dec_new = bias_dec[:, :past_len], bias_dec[:, past_len:]

    kp_l, vp_l, ks_l, vs_l, cbp_l, cbs_l, ccp_l, ccs_l = [], [], [], [], [], [], [], []
    for li in range(depth):
        w1 = (bf(ffn1_w_gate[li]), bf(ffn1_w_up[li]), bf(ffn1_w_down[li]))
        xp = _ffn(xp, norm_ffn1[li], *w1)
        xs = _ffn(xs, norm_ffn1[li], *w1)
        if li % 2 == 0:
            e = li // 2
            lam_init = 0.8 - 0.6 * math.exp(-0.3 * li)
            w_in, w_out = bf(w_in_even[e]), bf(w_out_even[e])
            qg = jnp.tile(q_norm_gain[e], d_qk // DH_QK).reshape(1, d_qk)
            kg = jnp.tile(k_norm_gain[e], d_qk // DH_QK).reshape(1, d_qk)
            lam_p = jnp.stack([lambda_q1[e], lambda_k1[e], lambda_q2[e], lambda_k2[e]])
            sg = jnp.tile(subln_gain[e], d_a // DH_V)

            kt, vt, qtb, kb, vtb, gcp, cbp = _even_in_prompt(
                xp, norm_mix[li], w_in, p_avg, qg, kg, conv_b_w[e], bp, tp)
            attp = _attn_prompt(bias_far, lam_p, sg[:2 * DH_V].reshape(-1, 1), qtb, kb, vtb, bias_tiles,
                                bp, tp, lam_init)
            xp = _even_out(xp, attp, gcp, w_out)

            qs, kn, vn, gcs, us = _even_in_sample(
                xs, norm_mix[li], w_in, p_avg, qg, kg, conv_b_w[e],
                state_conv_b[e, :, 0], state_conv_b[e, :, 1])
            atts = _attn_decode(page_table, qs, kn, vn, bias_dec_past, bias_dec_new, lam_p,
                                sg.reshape(1, -1), cache_kt, cache_vt, e, lam_init)
            xs = _even_out(xs, atts, gcs, w_out)

            kp_l.append(kt)
            vp_l.append(vt)
            ks_l.append(kn)
            vs_l.append(vn)
            cbp_l.append(cbp)
            cbs_l.append(jnp.stack([state_conv_b[e, :, 1], us], axis=1))
        else:
            o = li // 2
            prm = (norm_mix[li], bf(w_pw1[o]), b_pw1[o], conv_c_w[o], conv_c_b[o], ln_c_gain[o], ln_c_bias[o],
                   bf(w_pw2[o]), b_pw2[o])
            xp, ccp = _odd_prompt(xp, *prm, bp, tp)
            xs, ccs = _odd_sample(xs, *prm, state_c_t[o])
            ccp_l.append(ccp)
            ccs_l.append(ccs)
        w2 = (bf(ffn2_w_gate[li]), bf(ffn2_w_up[li]), bf(ffn2_w_down[li]))
        xp = _ffn(xp, norm_ffn2[li], *w2)
        xs = _ffn(xs, norm_ffn2[li], *w2)

    n_h = d_a // DH_V
    new_k_prompt = jnp.transpose(jnp.stack(kp_l).reshape(n_even, bp, n_sub, DH_QK, tp), (0, 1, 4, 2, 3))
    new_v_prompt = jnp.transpose(jnp.stack(vp_l).reshape(n_even, bp, n_h, DH_V, tp), (0, 1, 4, 2, 3))
    return (xp.reshape(bp, tp, d), xs.reshape(bs, 1, d),
            new_k_prompt, new_v_prompt,
            jnp.stack(ks_l).reshape(n_even, bs, 1, n_sub, DH_QK),
            jnp.stack(vs_l).reshape(n_even, bs, 1, n_h, DH_V),
            jnp.stack(cbp_l), jnp.stack(cbs_l), jnp.stack(ccp_l),
            jnp.transpose(jnp.stack(ccs_l), (0, 2, 1, 3)))
```

```python
import functools
import math

import jax
import jax.numpy as jnp
from jax import lax
from jax.experimental import pallas as pl
from jax.experimental.pallas import tpu as pltpu

F32 = jnp.float32
BF16 = jnp.bfloat16
EPS = 1e-6
NEG = -1e30

LANE = 128
SUBLANE = 8
VMEM_LIMIT_BYTES = 56 * 1024 * 1024

DH_QK = 32
DH_V = 64
N_BUCKETS = 32
MAX_EXACT = N_BUCKETS // 2
MAX_DISTANCE = 128
FAR = MAX_DISTANCE

TM_ROWS = 512
TM_FFN = 512
FFN_CHUNKS = 2
TM_ODD = 256
TQ = 256
SOFTMAX_SUM_ROWS = 16
HALO = 32
CONV_UNROLL = 8
LOG2E = math.log2(math.e)


def _params(sem):
    return pltpu.CompilerParams(dimension_semantics=sem, vmem_limit_bytes=VMEM_LIMIT_BYTES)


def _resident(shape):
    n = len(shape)
    return pl.BlockSpec(shape, lambda *_: (0,) * n, pipeline_mode=pl.Buffered(1))


def _rms(x, g):
    return x * lax.rsqrt(jnp.mean(x * x, axis=-1, keepdims=True) + EPS) * g


def _dot(a, b):
    return jnp.dot(a, b, preferred_element_type=F32)


def _group_rms(z, p_ref, gain):
    sq = z * z
    hi = sq.astype(BF16)
    lo = (sq - hi.astype(F32)).astype(BF16)
    ms = _dot(hi, p_ref[...]) + _dot(lo, p_ref[...])
    return z * lax.rsqrt(ms + EPS) * gain


def _ffn_kernel(x_ref, g_ref, wg_ref, wu_ref, wd_ref, *rest, n_chunks):
    o_ref = rest[-1]
    x = x_ref[...]
    if len(rest) > 1:
        a_ref, gc_ref, wo_ref = rest[:-1]
        d_a = a_ref.shape[1]
        x = x + _dot(a_ref[...].astype(BF16), wo_ref[0:d_a, :]) + _dot(gc_ref[...], wo_ref[d_a:, :])
    h = _rms(x, g_ref[...]).astype(BF16)
    fc = wg_ref.shape[1] // n_chunks
    y = jnp.zeros(x.shape, F32)
    for c in range(n_chunks):
        gate = _dot(h, wg_ref[:, c * fc:(c + 1) * fc])
        up = _dot(h, wu_ref[:, c * fc:(c + 1) * fc])
        a = (jax.nn.silu(gate) * up).astype(BF16)
        y = y + _dot(a, wd_ref[c * fc:(c + 1) * fc, :])
    o_ref[...] = x + 0.5 * y


def _ffn(x, g, wg, wu, wd, mixer=()):
    m, d = x.shape
    f = wg.shape[1]
    tm = min(TM_FFN, m)
    row = lambda width: pl.BlockSpec((tm, width), lambda i: (i, 0))
    mixer_specs = [row(mixer[0].shape[1]), row(mixer[1].shape[1]), _resident(mixer[2].shape)] if mixer else []
    return pl.pallas_call(
        functools.partial(_ffn_kernel, n_chunks=FFN_CHUNKS),
        grid=(m // tm,),
        in_specs=[row(d), _resident((1, d)), _resident((d, f)), _resident((d, f)), _resident((f, d))] + mixer_specs,
        out_specs=row(d),
        out_shape=jax.ShapeDtypeStruct((m, d), F32),
        compiler_params=_params(("parallel",)),
        name="ffn_mixer" if mixer else "ffn",
    )(x, g.reshape(1, d), wg, wu, wd, *mixer)


def _even_in_prompt_kernel(x_ref, g_ref, w_ref, p_ref, qg_ref, kg_ref, cw_ref,
                           kt_ref, vt_ref, qtb_ref, kb_ref, vtb_ref, gc_ref, st_ref,
                           u_scr, *, tiles_per_seq, d_qk, d_a, d_b, scale):
    i = pl.program_id(0)
    tm = x_ref.shape[0]
    h = _rms(x_ref[...], g_ref[...]).astype(BF16)
    z = _dot(h, w_ref[...])
    q = _group_rms(z[:, :d_qk], p_ref, qg_ref[...]) * scale
    k = _group_rms(z[:, d_qk:2 * d_qk], p_ref, kg_ref[...])
    o = 2 * d_qk
    v = z[:, o:o + d_a]
    o += d_a
    gate_b = z[:, o:o + d_b]
    u = z[:, o + d_b:o + 2 * d_b] * z[:, o + 2 * d_b:o + 3 * d_b]

    kt = k.T
    vt = v.T
    kt_ref[...] = kt
    vt_ref[...] = vt
    qtb_ref[...] = q.T.astype(BF16)
    kb_ref[...] = k.astype(BF16)
    vtb_ref[...] = vt.astype(BF16)

    @pl.when(i % tiles_per_seq == 0)
    def _():
        u_scr[0:SUBLANE, :] = jnp.zeros((SUBLANE, d_b), F32)

    u_scr[SUBLANE:SUBLANE + tm, :] = u
    conv = (cw_ref[0:1, :] * u_scr[SUBLANE - 2:SUBLANE - 2 + tm, :]
            + cw_ref[1:2, :] * u_scr[SUBLANE - 1:SUBLANE - 1 + tm, :]
            + cw_ref[2:3, :] * u)
    gc_ref[...] = (gate_b * conv).astype(BF16)
    st_ref[...] = u_scr[tm + SUBLANE - 2:tm + SUBLANE, :]
    u_scr[0:SUBLANE, :] = u_scr[tm:tm + SUBLANE, :]


def _even_in_prompt(x, g, w_in, p_avg, qg, kg, cw, n_seq, seq):
    m, d = x.shape
    d_qk = d_a = d_b = p_avg.shape[0]
    tm = TM_ROWS
    tps = seq // tm
    row = lambda width, dt=None: pl.BlockSpec((tm, width), lambda i: (i, 0))
    col = pl.BlockSpec((d_qk, tm), lambda i: (0, i))
    seq_t = pl.BlockSpec((None, d_qk, tm), lambda i: (i // tps, 0, i % tps))
    kern = functools.partial(_even_in_prompt_kernel, tiles_per_seq=tps, d_qk=d_qk, d_a=d_a, d_b=d_b,
                             scale=DH_QK ** -0.5 * LOG2E)
    return pl.pallas_call(
        kern,
        grid=(m // tm,),
        in_specs=[row(d), _resident((1, d)), _resident(w_in.shape), _resident(p_avg.shape),
                  _resident((1, d_qk)), _resident((1, d_qk)), _resident(cw.shape)],
        out_specs=[seq_t, seq_t, col, row(d_qk), col, row(d_b),
                   pl.BlockSpec((None, 2, d_b), lambda i: (i // tps, 0, 0))],
        out_shape=[jax.ShapeDtypeStruct((n_seq, d_qk, seq), F32),
                   jax.ShapeDtypeStruct((n_seq, d_a, seq), F32),
                   jax.ShapeDtypeStruct((d_qk, m), BF16),
                   jax.ShapeDtypeStruct((m, d_qk), BF16),
                   jax.ShapeDtypeStruct((d_a, m), BF16),
                   jax.ShapeDtypeStruct((m, d_b), BF16),
                   jax.ShapeDtypeStruct((n_seq, 2, d_b), F32)],
        scratch_shapes=[pltpu.VMEM((tm + SUBLANE, d_b), F32)],
        compiler_params=_params(("arbitrary",)),
        name="even_in_prompt",
    )(x, g.reshape(1, d), w_in, p_avg, qg, kg, cw)


def _even_in_sample_kernel(x_ref, g_ref, w_ref, p_ref, qg_ref, kg_ref, cw_ref, s0_ref, s1_ref,
                           q_ref, k_ref, v_ref, gc_ref, u_ref, *, d_qk, d_a, d_b, scale):
    h = _rms(x_ref[...], g_ref[...]).astype(BF16)
    z = _dot(h, w_ref[...])
    q_ref[...] = _group_rms(z[:, :d_qk], p_ref, qg_ref[...]) * scale
    k_ref[...] = _group_rms(z[:, d_qk:2 * d_qk], p_ref, kg_ref[...])
    o = 2 * d_qk
    v_ref[...] = z[:, o:o + d_a]
    o += d_a
    gate_b = z[:, o:o + d_b]
    u = z[:, o + d_b:o + 2 * d_b] * z[:, o + 2 * d_b:o + 3 * d_b]
    conv = cw_ref[0:1, :] * s0_ref[...] + cw_ref[1:2, :] * s1_ref[...] + cw_ref[2:3, :] * u
    gc_ref[...] = (gate_b * conv).astype(BF16)
    u_ref[...] = u


def _even_in_sample(x, g, w_in, p_avg, qg, kg, cw, s0, s1):
    m, d = x.shape
    d_qk = d_a = d_b = p_avg.shape[0]
    kern = functools.partial(_even_in_sample_kernel, d_qk=d_qk, d_a=d_a, d_b=d_b, scale=DH_QK ** -0.5)
    f = lambda dt: jax.ShapeDtypeStruct((m, d_qk), dt)
    return pl.pallas_call(
        kern,
        out_shape=[f(F32), f(F32), f(F32), f(BF16), f(F32)],
        compiler_params=pltpu.CompilerParams(vmem_limit_bytes=VMEM_LIMIT_BYTES),
        name="even_in_sample",
    )(x, g.reshape(1, d), w_in, p_avg, qg, kg, cw, s0, s1)


def _lambda(lam_ref, lam_init):
    lam_p = lam_ref[...]
    return (jnp.exp(jnp.sum(lam_p[0:1] * lam_p[1:2], axis=1, keepdims=True))
            - jnp.exp(jnp.sum(lam_p[2:3] * lam_p[3:4], axis=1, keepdims=True)) + lam_init)


def _prompt_tiles(tiles, hp, lam, cfar_ref, sg_ref, qt_ref, k_ref, vt_ref, bias_ref, o_ref, p_scr, *,
                  lam_init, tq):
    n_sub = qt_ref.shape[0] // DH_QK
    n_head = n_sub // 2
    cfar = jnp.concatenate([jnp.full((1, 2 * tq), cfar_ref[hp * n_head + h], F32) for h in range(n_head)],
                           axis=1)
    grp = lax.broadcasted_iota(jnp.int32, (qt_ref.shape[0], tq), 0) // DH_QK

    for n in tiles:
        qt = qt_ref[:, n * tq:(n + 1) * tq]
        q_blk = jnp.concatenate([jnp.where(grp == c, qt, jnp.zeros_like(qt)) for c in range(n_sub)], axis=1)

        m = None
        s_all = _dot(k_ref[0:(n + 1) * tq, :], q_blk)
        s_tiles = []
        for t in range(n + 1):
            s = s_all[t * tq:(t + 1) * tq, :]
            if t >= n - 1:
                kind = 1 if t == n else 0
                s = jnp.concatenate([s[:, c * tq:(c + 1) * tq] + bias_ref[c // 2, kind] for c in range(n_sub)],
                                    axis=1)
                smax = jnp.max(s, axis=0, keepdims=True)
            else:
                smax = jnp.max(s, axis=0, keepdims=True) + cfar
            m = smax if m is None else jnp.maximum(m, smax)
            s_tiles.append(s)

        m_far = m - cfar
        for t in range(n + 1):
            shift = m if t >= n - 1 else m_far
            p_scr[t * tq:(t + 1) * tq, :] = jnp.exp2(s_tiles[t] - shift).astype(BF16)

        keys = (n + 1) * tq
        ones = jnp.ones((SOFTMAX_SUM_ROWS, keys), BF16)
        outs = []
        for h in range(n_head):
            vte = jnp.concatenate([vt_ref[h * DH_V:(h + 1) * DH_V, 0:keys], ones], axis=0)
            pv = _dot(vte, p_scr[0:keys, h * 2 * tq:(h + 1) * 2 * tq])
            sub = [pv[0:DH_V, c * tq:(c + 1) * tq] / pv[DH_V:DH_V + 1, c * tq:(c + 1) * tq] for c in range(2)]
            o = sub[0] - lam * sub[1]
            ms = jnp.mean(o * o, axis=0, keepdims=True)
            outs.append(o * lax.rsqrt(ms + EPS))
        o = jnp.concatenate(outs, axis=0) * sg_ref[...] * (1.0 - lam_init)
        o_ref[n * tq:(n + 1) * tq, :] = o.T.astype(BF16)


def _decode_pages(first, last, lam, q_ref, kn_ref, vn_ref, bt_ref, b0_ref, sg_ref, k_refs, v_refs, o_ref,
                  m_scr, l_scr, acc_scr, *, lam_init):
    pp = len(k_refs)
    n_sub, d = acc_scr.shape
    page = k_refs[0].shape[1]

    row = lax.broadcasted_iota(jnp.int32, (n_sub, d), 0)
    col = lax.broadcasted_iota(jnp.int32, (n_sub, d), 1)
    qexp = jnp.where(col // DH_QK == row, jnp.broadcast_to(q_ref[...], (n_sub, d)), 0.0)

    if first:
        m_scr[...] = jnp.sum(qexp * kn_ref[...], axis=1, keepdims=True) + b0_ref[...]
        l_scr[...] = jnp.ones(l_scr.shape, F32)
        acc_scr[...] = jnp.broadcast_to(vn_ref[...], (n_sub, d))

    qb = qexp.astype(BF16)
    s = jnp.concatenate([_dot(qb, k_refs[i][...].astype(BF16)) for i in range(pp)], axis=1)
    s = s + bt_ref[...]
    m_old = m_scr[...]
    m_new = jnp.maximum(m_old, jnp.max(s, axis=1, keepdims=True))
    p = jnp.exp(s - m_new)
    alpha = jnp.exp(m_old - m_new)
    l_scr[...] = alpha * l_scr[...] + jnp.sum(p, axis=1, keepdims=True)
    pb = p.astype(BF16)
    pv = jnp.zeros((n_sub, d), F32)
    for i in range(pp):
        pv = pv + lax.dot_general(pb[:, i * page:(i + 1) * page], v_refs[i][...].astype(BF16),
                                  (((1,), (1,)), ((), ())), preferred_element_type=F32)
    acc_scr[...] = alpha * acc_scr[...] + pv
    m_scr[...] = m_new

    if last:
        o = acc_scr[...] / l_scr[...]
        head = col // DH_V
        a0 = jnp.sum(jnp.where(row == 2 * head, o, 0.0), axis=0, keepdims=True)
        a1 = jnp.sum(jnp.where(row == 2 * head + 1, o, 0.0), axis=0, keepdims=True)
        att = a0 - lam * a1
        sq = att * att
        head1 = head[0:1]
        r = jnp.zeros_like(att)
        for hh in range(d // DH_V):
            ms = jnp.sum(jnp.where(head1 == hh, sq, 0.0), axis=1, keepdims=True) * (1.0 / DH_V)
            r = jnp.where(head1 == hh, lax.rsqrt(ms + EPS), r)
        o_ref[...] = att * r * sg_ref[...] * (1.0 - lam_init)


def _attention_kernel(pt_ref, cfar_ref, lam_ref, sgc_ref, qt_ref, k_ref, vt_ref, bias_ref,
                      q_ref, kn_ref, vn_ref, bt_ref, b0_ref, sgr_ref, *rest, pp, tiles, lam_init, tq):
    del pt_ref
    k_refs, v_refs = rest[:pp], rest[pp:2 * pp]
    o_ref, od_ref, p_scr, m_scr, l_scr, acc_scr = rest[2 * pp:]
    hp = pl.program_id(1)
    half = pl.program_id(2)
    lam = _lambda(lam_ref, lam_init)
    for hv in range(2):
        @pl.when(half == hv)
        def _(hv=hv):
            _prompt_tiles(tiles[hv], hp, lam, cfar_ref, sgc_ref, qt_ref, k_ref, vt_ref, bias_ref, o_ref, p_scr,
                          lam_init=lam_init, tq=tq)
            _decode_pages(hv == 0, hv == 1, lam, q_ref, kn_ref, vn_ref, bt_ref, b0_ref, sgr_ref, k_refs, v_refs,
                          od_ref, m_scr, l_scr, acc_scr, lam_init=lam_init)


def _attention(page_table, cfar, lam_p, sg, qt, k, vt, bias_tiles, q, kn, vn, bias_t, bias0,
               cache_kt, cache_vt, layer, n_seq, seq, lam_init):
    d_qk, m = qt.shape
    n_dec, n_pages = page_table.shape
    d = q.shape[-1]
    page = cache_kt.shape[-1]
    pair = 2 * 2 * DH_QK
    n_pair = d_qk // pair
    nq = seq // TQ
    order = sorted(range(nq), reverse=True)
    tiles = ([n for i, n in enumerate(order) if i % 4 in (0, 3)], [n for i, n in enumerate(order) if i % 4 in (1, 2)])
    halves = 2
    pp = n_pages // halves
    assert n_pages % halves == 0 and n_seq * n_pair == n_dec, "one sample sequence per (prompt sequence, head pair)"
    kern = functools.partial(_attention_kernel, pp=pp, tiles=tiles, lam_init=lam_init, tq=TQ)

    def page_spec(i):
        return pl.BlockSpec((None, None, d, page),
                            lambda b, h, s, pt: (layer, pt[(b * n_pair + h) * n_pages + s * pp + i], 0, 0))

    const = lambda shape: pl.BlockSpec(shape, lambda b, h, s, pt: (0,) * len(shape))
    tok = pl.BlockSpec((None, 1, d), lambda b, h, s, pt: (b * n_pair + h, 0, 0))
    grid_spec = pltpu.PrefetchScalarGridSpec(
        num_scalar_prefetch=1,
        grid=(n_seq, n_pair, halves),
        in_specs=[pl.BlockSpec(memory_space=pltpu.SMEM),
                  const(lam_p.shape),
                  const((pair, 1)),
                  pl.BlockSpec((pair, seq), lambda b, h, s, pt: (h, b)),
                  pl.BlockSpec((seq, pair), lambda b, h, s, pt: (b, h)),
                  pl.BlockSpec((pair, seq), lambda b, h, s, pt: (h, b)),
                  pl.BlockSpec((2, 2, TQ, TQ), lambda b, h, s, pt: (h, 0, 0, 0)),
                  tok, tok, tok,
                  pl.BlockSpec((d // DH_QK, pp * page), lambda b, h, s, pt: (0, s)),
                  const((d // DH_QK, 1)),
                  const((1, d))]
                 + [page_spec(i) for i in range(pp)] + [page_spec(i) for i in range(pp)],
        out_specs=[pl.BlockSpec((seq, pair), lambda b, h, s, pt: (b, h)), tok],
        scratch_shapes=[pltpu.VMEM((seq, (pair // DH_QK) * TQ), BF16),
                        pltpu.VMEM((d // DH_QK, 1), F32), pltpu.VMEM((d // DH_QK, 1), F32),
                        pltpu.VMEM((d // DH_QK, d), F32)],
    )
    att_p, att_s = pl.pallas_call(
        kern,
        grid_spec=grid_spec,
        out_shape=[jax.ShapeDtypeStruct((m, d_qk), BF16), jax.ShapeDtypeStruct((n_dec, 1, d), F32)],
        compiler_params=_params(("parallel", "parallel", "arbitrary")),
        name="attention",
    )(page_table.reshape(-1), cfar, lam_p, sg[:pair].reshape(-1, 1), qt, k, vt, bias_tiles,
      q.reshape(n_dec, 1, d), kn.reshape(n_dec, 1, d), vn.reshape(n_dec, 1, d), bias_t, bias0, sg.reshape(1, -1),
      *([cache_kt] * pp), *([cache_vt] * pp))
    return att_p, att_s.reshape(n_dec, d)


def _layernorm_swish(c, g, b):
    mu = jnp.mean(c, axis=-1, keepdims=True)
    xc = c - mu
    y = xc * lax.rsqrt(jnp.mean(xc * xc, axis=-1, keepdims=True) + EPS) * g + b
    return jax.nn.silu(y)


def _odd_prompt_kernel(x_ref, g_ref, w1_ref, b1_ref, cw_ref, cb_ref, lg_ref, lb_ref, w2_ref, b2_ref,
                       o_ref, st_ref, u_scr, c_scr, *, tiles_per_seq, width):
    i = pl.program_id(0)
    tm, d = x_ref.shape
    nl = d // LANE
    x = x_ref[...]
    h = _rms(x, g_ref[...]).astype(BF16)
    z = _dot(h, w1_ref[...]) + b1_ref[...]
    u = z[:, :d] * jax.nn.sigmoid(z[:, d:])

    @pl.when(i % tiles_per_seq == 0)
    def _():
        u_scr[0:HALO * nl, :] = jnp.zeros((HALO * nl, LANE), F32)

    for l in range(nl):
        u_scr[pl.ds(HALO * nl + l, tm, stride=nl), :] = u[:, l * LANE:(l + 1) * LANE]

    @pl.when(i % tiles_per_seq == tiles_per_seq - 1)
    def _():
        st_ref[...] = u[tm - (width - 1):, :]

    first = HALO - (width - 1)
    taps = [cw_ref[j] for j in range(width)]

    def chunk(r, carry):
        t0 = r * CONV_UNROLL
        slab = lambda t: pl.ds(pl.multiple_of(t * nl, nl), nl)
        rows = [u_scr[slab(t0 + first + k), :] for k in range(CONV_UNROLL + width - 1)]
        for o in range(CONV_UNROLL):
            acc = taps[0] * rows[o]
            for j in range(1, width):
                acc = acc + taps[j] * rows[o + j]
            c_scr[slab(t0 + o), :] = acc
        return carry

    lax.fori_loop(0, tm // CONV_UNROLL, chunk, 0)

    u_scr[0:HALO * nl, :] = u_scr[tm * nl:(tm + HALO) * nl, :]
    c = jnp.concatenate([c_scr[pl.ds(l, tm, stride=nl), :] for l in range(nl)], axis=1)
    a = _layernorm_swish(c + cb_ref[...], lg_ref[...], lb_ref[...]).astype(BF16)
    o_ref[...] = x + _dot(a, w2_ref[...]) + b2_ref[...]


def _odd_prompt(x, g, w1, b1, cw, cb, lg, lb, w2, b2, n_seq, seq):
    m, d = x.shape
    width = cw.shape[0]
    tm = TM_ODD
    tps = seq // tm
    row = pl.BlockSpec((tm, d), lambda i: (i, 0))
    vec = lambda a: a.reshape(1, -1)
    nl = d // LANE
    assert tm >= width - 1 and tm % CONV_UNROLL == 0 and HALO >= width - 1
    kern = functools.partial(_odd_prompt_kernel, tiles_per_seq=tps, width=width)
    return pl.pallas_call(
        kern,
        grid=(m // tm,),
        in_specs=[row, _resident((1, d)), _resident(w1.shape), _resident((1, 2 * d)),
                  _resident((width, nl, LANE)),
                  _resident((1, d)), _resident((1, d)), _resident((1, d)), _resident(w2.shape),
                  _resident((1, d))],
        out_specs=[row, pl.BlockSpec((None, width - 1, d), lambda i: (i // tps, 0, 0))],
        out_shape=[jax.ShapeDtypeStruct((m, d), F32),
                   jax.ShapeDtypeStruct((n_seq, width - 1, d), F32)],
        scratch_shapes=[pltpu.VMEM(((tm + HALO) * nl, LANE), F32), pltpu.VMEM((tm * nl, LANE), F32)],
        compiler_params=_params(("arbitrary",)),
        name="odd_prompt",
    )(x, vec(g), w1, vec(b1), cw.reshape(width, nl, LANE), vec(cb), vec(lg), vec(lb), w2, vec(b2))


def _odd_sample_kernel(x_ref, g_ref, w1_ref, b1_ref, cw_ref, cb_ref, lg_ref, lb_ref, w2_ref, b2_ref,
                       st_ref, o_ref, nst_ref):
    n_hist = st_ref.shape[0]
    d = x_ref.shape[1]
    x = x_ref[...]
    h = _rms(x, g_ref[...]).astype(BF16)
    z = _dot(h, w1_ref[...]) + b1_ref[...]
    u = z[:, :d] * jax.nn.sigmoid(z[:, d:])
    c = cw_ref[n_hist:n_hist + 1, :] * u
    for j in range(n_hist):
        c = c + cw_ref[j:j + 1, :] * st_ref[j]
    for j in range(n_hist - 1):
        nst_ref[j] = st_ref[j + 1]
    nst_ref[n_hist - 1] = u
    a = _layernorm_swish(c + cb_ref[...], lg_ref[...], lb_ref[...]).astype(BF16)
    o_ref[...] = x + _dot(a, w2_ref[...]) + b2_ref[...]


def _odd_sample(x, g, w1, b1, cw, cb, lg, lb, w2, b2, state_t):
    m, d = x.shape
    vec = lambda a: a.reshape(1, -1)
    return pl.pallas_call(
        _odd_sample_kernel,
        out_shape=[jax.ShapeDtypeStruct((m, d), F32), jax.ShapeDtypeStruct(state_t.shape, F32)],
        compiler_params=pltpu.CompilerParams(vmem_limit_bytes=VMEM_LIMIT_BYTES),
        name="odd_sample",
    )(x, vec(g), w1, vec(b1), cw, vec(cb), vec(lg), vec(lb), w2, vec(b2), state_t)


def _bias_by_distance(rel_bias, dist):
    n = jnp.maximum(dist, 0)
    nf = jnp.maximum(n, 1).astype(F32)
    large = MAX_EXACT + (jnp.log(nf / MAX_EXACT) / math.log(MAX_DISTANCE / MAX_EXACT)
                         * (N_BUCKETS - MAX_EXACT)).astype(jnp.int32)
    large = jnp.minimum(large, N_BUCKETS - 1)
    return rel_bias[jnp.where(n < MAX_EXACT, n, large)]


def _prompt_bias_tiles(rel_bias):
    assert TQ >= FAR - 1
    n_head = rel_bias.shape[1]
    by_dist = _bias_by_distance(rel_bias, jnp.arange(2 * TQ)).T * LOG2E
    diag = jnp.concatenate([jnp.full((n_head, TQ - 1), NEG, F32), by_dist[:, :TQ]], axis=1)
    near = by_dist[:, 1:]

    def toeplitz(a):
        period = a.shape[1]
        rows = jnp.tile(a, (1, TQ + 1))[:, :TQ * (period + 1)].reshape(n_head, TQ, period + 1)
        return rows[:, ::-1, :TQ]

    tiles = jnp.stack([toeplitz(near), toeplitz(diag)], axis=1)
    return tiles, by_dist[:, TQ + 1]


def kernel(x_prompt, x_sample, cache_k, cache_v, state_conv_b, state_conv_c, page_table, rel_bias, norm_ffn1, ffn1_w_gate, ffn1_w_up, ffn1_w_down, norm_mix, norm_ffn2, ffn2_w_gate, ffn2_w_up, ffn2_w_down, w_in_even, w_out_even, q_norm_gain, k_norm_gain, lambda_q1, lambda_k1, lambda_q2, lambda_k2, subln_gain, conv_b_w, w_pw1, b_pw1, conv_c_w, conv_c_b, ln_c_gain, ln_c_bias, w_pw2, b_pw2):
    bp, tp, d = x_prompt.shape
    bs = x_sample.shape[0]
    depth = norm_ffn1.shape[0]
    n_even, n_phys, page, n_sub, dh_qk = cache_k.shape
    assert dh_qk == DH_QK and cache_v.shape[-1] == DH_V and x_sample.shape[1] == 1
    d_qk = n_sub * dh_qk
    d_a = cache_v.shape[3] * DH_V
    past_len = page_table.shape[1] * page
    assert tp % TM_ROWS == 0 and tp % TM_ODD == 0 and tp % TQ == 0 and tp >= 2 * TQ

    bf = lambda w: w.astype(BF16)
    xp = x_prompt.reshape(bp * tp, d)
    xs = x_sample.reshape(bs, d)

    cache_kt = jnp.transpose(cache_k, (0, 1, 3, 4, 2)).reshape(n_even, n_phys, d_qk, page)
    cache_vt = jnp.transpose(cache_v, (0, 1, 3, 4, 2)).reshape(n_even, n_phys, d_a, page)
    state_c_t = jnp.transpose(state_conv_c, (0, 2, 1, 3))

    p_avg = jnp.kron(jnp.eye(d_qk // DH_QK, dtype=F32), jnp.full((DH_QK, DH_QK), 1.0 / DH_QK, F32)).astype(BF16)
    bias_tiles, bias_far = _prompt_bias_tiles(rel_bias)
    k_pos = jnp.arange(past_len + 1)
    bias_dec = jnp.repeat(_bias_by_distance(rel_bias, past_len - k_pos), 2, axis=1).T
    bias_dec_past, bias_dec_new = bias_dec[:, :past_len], bias_dec[:, past_len:]

    kp_l, vp_l, ks_l, vs_l, cbp_l, cbs_l, ccp_l, ccs_l = [], [], [], [], [], [], [], []
    for li in range(depth):
        w1 = (bf(ffn1_w_gate[li]), bf(ffn1_w_up[li]), bf(ffn1_w_down[li]))
        xp = _ffn(xp, norm_ffn1[li], *w1)
        xs = _ffn(xs, norm_ffn1[li], *w1)
        if li % 2 == 0:
            e = li // 2
            lam_init = 0.8 - 0.6 * math.exp(-0.3 * li)
            w_in, w_out = bf(w_in_even[e]), bf(w_out_even[e])
            qg = jnp.tile(q_norm_gain[e], d_qk // DH_QK).reshape(1, d_qk)
            kg = jnp.tile(k_norm_gain[e], d_qk // DH_QK).reshape(1, d_qk)
            lam_p = jnp.stack([lambda_q1[e], lambda_k1[e], lambda_q2[e], lambda_k2[e]])
            sg = jnp.tile(subln_gain[e], d_a // DH_V)

            kt, vt, qtb, kb, vtb, gcp, cbp = _even_in_prompt(
                xp, norm_mix[li], w_in, p_avg, qg, kg, conv_b_w[e], bp, tp)
            qs, kn, vn, gcs, us = _even_in_sample(
                xs, norm_mix[li], w_in, p_avg, qg, kg, conv_b_w[e],
                state_conv_b[e, :, 0], state_conv_b[e, :, 1])
            attp, atts = _attention(page_table, bias_far, lam_p, sg, qtb, kb, vtb, bias_tiles,
                                    qs, kn, vn, bias_dec_past, bias_dec_new, cache_kt, cache_vt, e,
                                    bp, tp, lam_init)
            mix_p, mix_s = (attp, gcp, w_out), (atts, gcs, w_out)

            kp_l.append(kt)
            vp_l.append(vt)
            ks_l.append(kn)
            vs_l.append(vn)
            cbp_l.append(cbp)
            cbs_l.append(jnp.stack([state_conv_b[e, :, 1], us], axis=1))
        else:
            o = li // 2
            prm = (norm_mix[li], bf(w_pw1[o]), b_pw1[o], conv_c_w[o], conv_c_b[o], ln_c_gain[o], ln_c_bias[o],
                   bf(w_pw2[o]), b_pw2[o])
            xp, ccp = _odd_prompt(xp, *prm, bp, tp)
            xs, ccs = _odd_sample(xs, *prm, state_c_t[o])
            ccp_l.append(ccp)
            ccs_l.append(ccs)
            mix_p = mix_s = ()
        w2 = (bf(ffn2_w_gate[li]), bf(ffn2_w_up[li]), bf(ffn2_w_down[li]))
        xp = _ffn(xp, norm_ffn2[li], *w2, mixer=mix_p)
        xs = _ffn(xs, norm_ffn2[li], *w2, mixer=mix_s)

    n_h = d_a // DH_V
    new_k_prompt = jnp.transpose(jnp.stack(kp_l).reshape(n_even, bp, n_sub, DH_QK, tp), (0, 1, 4, 2, 3))
    new_v_prompt = jnp.transpose(jnp.stack(vp_l).reshape(n_even, bp, n_h, DH_V, tp), (0, 1, 4, 2, 3))
    return (xp.reshape(bp, tp, d), xs.reshape(bs, 1, d),
            new_k_prompt, new_v_prompt,
            jnp.stack(ks_l).reshape(n_even, bs, 1, n_sub, DH_QK),
            jnp.stack(vs_l).reshape(n_even, bs, 1, n_h, DH_V),
            jnp.stack(cbp_l), jnp.stack(cbs_l), jnp.stack(ccp_l),
            jnp.transpose(jnp.stack(ccs_l), (0, 2, 1, 3)))
```

```python
import functools
import math

import jax
import jax.numpy as jnp
from jax import lax
from jax.experimental import pallas as pl
from jax.experimental.pallas import tpu as pltpu

F32 = jnp.float32
BF16 = jnp.bfloat16
EPS = 1e-6
NEG = -1e30

LANE = 128
SUBLANE = 8
VMEM_LIMIT_BYTES = 56 * 1024 * 1024

DH_QK = 32
DH_V = 64
N_BUCKETS = 32
MAX_EXACT = N_BUCKETS // 2
MAX_DISTANCE = 128
FAR = MAX_DISTANCE

TM_ROWS = 512
TM_FFN = 512
FFN_CHUNKS = 2
TM_ODD = 256
TQ = 256
SOFTMAX_SUM_ROWS = 16
HALO = 32
CONV_UNROLL = 8
LOG2E = math.log2(math.e)


def _params(sem):
    return pltpu.CompilerParams(dimension_semantics=sem, vmem_limit_bytes=VMEM_LIMIT_BYTES)


def _resident(shape):
    n = len(shape)
    return pl.BlockSpec(shape, lambda *_: (0,) * n, pipeline_mode=pl.Buffered(1))


def _resident_layer(stacked, layer):
    n = stacked.ndim - 1
    return pl.BlockSpec((None,) + stacked.shape[1:], lambda *_: (layer,) + (0,) * n,
                        pipeline_mode=pl.Buffered(1))


def _rms(x, g):
    return x * lax.rsqrt(jnp.mean(x * x, axis=-1, keepdims=True) + EPS) * g


def _dot(a, b):
    return jnp.dot(a, b, preferred_element_type=F32)


def _group_rms(z, p_ref, gain):
    sq = z * z
    hi = sq.astype(BF16)
    lo = (sq - hi.astype(F32)).astype(BF16)
    ms = _dot(hi, p_ref[...]) + _dot(lo, p_ref[...])
    return z * lax.rsqrt(ms + EPS) * gain


def _ffn_kernel(x_ref, g_ref, wg_ref, wu_ref, wd_ref, *rest, n_chunks):
    o_ref = rest[-1]
    x = x_ref[...]
    if len(rest) > 1:
        a_ref, gc_ref, wo_ref = rest[:-1]
        d_a = a_ref.shape[1]
        x = x + _dot(a_ref[...].astype(BF16), wo_ref[0:d_a, :]) + _dot(gc_ref[...], wo_ref[d_a:, :])
    h = _rms(x, g_ref[...]).astype(BF16)
    fc = wg_ref.shape[1] // n_chunks
    y = jnp.zeros(x.shape, F32)
    for c in range(n_chunks):
        gate = _dot(h, wg_ref[:, c * fc:(c + 1) * fc])
        up = _dot(h, wu_ref[:, c * fc:(c + 1) * fc])
        a = (jax.nn.silu(gate) * up).astype(BF16)
        y = y + _dot(a, wd_ref[c * fc:(c + 1) * fc, :])
    o_ref[...] = x + 0.5 * y


def _ffn(x, g, layer, wg, wu, wd, mixer=()):
    m, d = x.shape
    tm = min(TM_FFN, m)
    row = lambda width: pl.BlockSpec((tm, width), lambda i: (i, 0))
    mixer_specs = ([row(mixer[0].shape[1]), row(mixer[1].shape[1]), _resident_layer(mixer[2], mixer[3])]
                   if mixer else [])
    return pl.pallas_call(
        functools.partial(_ffn_kernel, n_chunks=FFN_CHUNKS),
        grid=(m // tm,),
        in_specs=[row(d), _resident((1, d)), _resident_layer(wg, layer), _resident_layer(wu, layer),
                  _resident_layer(wd, layer)] + mixer_specs,
        out_specs=row(d),
        out_shape=jax.ShapeDtypeStruct((m, d), F32),
        compiler_params=_params(("parallel",)),
        name="ffn_mixer" if mixer else "ffn",
    )(x, g.reshape(1, d), wg, wu, wd, *mixer[:3])


def _even_in_prompt_kernel(x_ref, g_ref, w_ref, p_ref, qg_ref, kg_ref, cw_ref, *rest,
                           tiles_per_seq, d_qk, d_a, d_b, scale):
    kt_ref, vt_ref, qtb_ref, kb_ref, vtb_ref, gc_ref, st_ref, u_scr = rest[-8:]
    i = pl.program_id(0)
    tm = x_ref.shape[0]
    h = _rms(x_ref[...], g_ref[...]).astype(BF16)
    z = _dot(h, w_ref[...])
    q = _group_rms(z[:, :d_qk], p_ref, qg_ref[...]) * scale
    k = _group_rms(z[:, d_qk:2 * d_qk], p_ref, kg_ref[...])
    o = 2 * d_qk
    v = z[:, o:o + d_a]
    o += d_a
    gate_b = z[:, o:o + d_b]
    u = z[:, o + d_b:o + 2 * d_b] * z[:, o + 2 * d_b:o + 3 * d_b]

    kt = k.T
    vt = v.T
    kt_ref[...] = kt
    vt_ref[...] = vt
    qtb_ref[...] = q.T.astype(BF16)
    kb_ref[...] = k.astype(BF16)
    vtb_ref[...] = vt.astype(BF16)

    @pl.when(i % tiles_per_seq == 0)
    def _():
        u_scr[0:SUBLANE, :] = jnp.zeros((SUBLANE, d_b), F32)

    u_scr[SUBLANE:SUBLANE + tm, :] = u
    conv = (cw_ref[0:1, :] * u_scr[SUBLANE - 2:SUBLANE - 2 + tm, :]
            + cw_ref[1:2, :] * u_scr[SUBLANE - 1:SUBLANE - 1 + tm, :]
            + cw_ref[2:3, :] * u)
    gc_ref[...] = (gate_b * conv).astype(BF16)
    st_ref[...] = u_scr[tm + SUBLANE - 2:tm + SUBLANE, :]
    u_scr[0:SUBLANE, :] = u_scr[tm:tm + SUBLANE, :]


def _even_in_prompt(x, g, w_in, layer, n_layers, kv_prev, p_avg, qg, kg, cw, n_seq, seq):
    m, d = x.shape
    d_qk = d_a = d_b = p_avg.shape[0]
    tm = TM_ROWS
    tps = seq // tm
    row = lambda width, dt=None: pl.BlockSpec((tm, width), lambda i: (i, 0))
    col = pl.BlockSpec((d_qk, tm), lambda i: (0, i))
    seq_t = pl.BlockSpec((None, None, d_qk, tm), lambda i: (layer, i // tps, 0, i % tps))
    n_in = 7
    kern = functools.partial(_even_in_prompt_kernel, tiles_per_seq=tps, d_qk=d_qk, d_a=d_a, d_b=d_b,
                             scale=DH_QK ** -0.5 * LOG2E)
    return pl.pallas_call(
        kern,
        grid=(m // tm,),
        in_specs=[row(d), _resident((1, d)), _resident_layer(w_in, layer), _resident(p_avg.shape),
                  _resident((1, d_qk)), _resident((1, d_qk)), _resident(cw.shape)]
                 + [pl.BlockSpec(memory_space=pl.ANY)] * len(kv_prev),
        out_specs=[seq_t, seq_t, col, row(d_qk), col, row(d_b),
                   pl.BlockSpec((None, 2, d_b), lambda i: (i // tps, 0, 0))],
        out_shape=[jax.ShapeDtypeStruct((n_layers, n_seq, d_qk, seq), F32),
                   jax.ShapeDtypeStruct((n_layers, n_seq, d_a, seq), F32),
                   jax.ShapeDtypeStruct((d_qk, m), BF16),
                   jax.ShapeDtypeStruct((m, d_qk), BF16),
                   jax.ShapeDtypeStruct((d_a, m), BF16),
                   jax.ShapeDtypeStruct((m, d_b), BF16),
                   jax.ShapeDtypeStruct((n_seq, 2, d_b), F32)],
        scratch_shapes=[pltpu.VMEM((tm + SUBLANE, d_b), F32)],
        input_output_aliases={n_in + j: j for j in range(len(kv_prev))},
        compiler_params=_params(("arbitrary",)),
        name="even_in_prompt",
    )(x, g.reshape(1, d), w_in, p_avg, qg, kg, cw, *kv_prev)


def _even_in_sample_kernel(x_ref, g_ref, w_ref, p_ref, qg_ref, kg_ref, cw_ref, s0_ref, s1_ref,
                           q_ref, k_ref, v_ref, gc_ref, u_ref, *, d_qk, d_a, d_b, scale):
    h = _rms(x_ref[...], g_ref[...]).astype(BF16)
    z = _dot(h, w_ref[...])
    q_ref[...] = _group_rms(z[:, :d_qk], p_ref, qg_ref[...]) * scale
    k_ref[...] = _group_rms(z[:, d_qk:2 * d_qk], p_ref, kg_ref[...])
    o = 2 * d_qk
    v_ref[...] = z[:, o:o + d_a]
    o += d_a
    gate_b = z[:, o:o + d_b]
    u = z[:, o + d_b:o + 2 * d_b] * z[:, o + 2 * d_b:o + 3 * d_b]
    conv = cw_ref[0:1, :] * s0_ref[...] + cw_ref[1:2, :] * s1_ref[...] + cw_ref[2:3, :] * u
    gc_ref[...] = (gate_b * conv).astype(BF16)
    u_ref[...] = u


def _even_in_sample(x, g, w_in, layer, p_avg, qg, kg, cw, s0, s1):
    m, d = x.shape
    d_qk = d_a = d_b = p_avg.shape[0]
    kern = functools.partial(_even_in_sample_kernel, d_qk=d_qk, d_a=d_a, d_b=d_b, scale=DH_QK ** -0.5)
    f = lambda dt: jax.ShapeDtypeStruct((m, d_qk), dt)
    blk = _resident((m, d_qk))
    return pl.pallas_call(
        kern,
        grid=(1,),
        in_specs=[_resident((m, d)), _resident((1, d)), _resident_layer(w_in, layer), _resident(p_avg.shape),
                  _resident((1, d_qk)), _resident((1, d_qk)), _resident(cw.shape), blk, blk],
        out_specs=[pl.BlockSpec((m, d_qk), lambda i: (0, 0))] * 5,
        out_shape=[f(F32), f(F32), f(F32), f(BF16), f(F32)],
        compiler_params=_params(("arbitrary",)),
        name="even_in_sample",
    )(x, g.reshape(1, d), w_in, p_avg, qg, kg, cw, s0, s1)


def _lambda(lam_ref, lam_init):
    lam_p = lam_ref[...]
    return (jnp.exp(jnp.sum(lam_p[0:1] * lam_p[1:2], axis=1, keepdims=True))
            - jnp.exp(jnp.sum(lam_p[2:3] * lam_p[3:4], axis=1, keepdims=True)) + lam_init)


def _prompt_tiles(tiles, hp, lam, cfar_ref, sg_ref, qt_ref, k_ref, vt_ref, bias_ref, o_ref, p_scr, *,
                  lam_init, tq):
    n_sub = qt_ref.shape[0] // DH_QK
    n_head = n_sub // 2
    cfar = jnp.concatenate([jnp.full((1, 2 * tq), cfar_ref[hp * n_head + h], F32) for h in range(n_head)],
                           axis=1)
    grp = lax.broadcasted_iota(jnp.int32, (qt_ref.shape[0], tq), 0) // DH_QK

    for n in tiles:
        qt = qt_ref[:, n * tq:(n + 1) * tq]
        q_blk = jnp.concatenate([jnp.where(grp == c, qt, jnp.zeros_like(qt)) for c in range(n_sub)], axis=1)

        m = None
        s_all = _dot(k_ref[0:(n + 1) * tq, :], q_blk)
        s_tiles = []
        for t in range(n + 1):
            s = s_all[t * tq:(t + 1) * tq, :]
            if t >= n - 1:
                kind = 1 if t == n else 0
                s = jnp.concatenate([s[:, c * tq:(c + 1) * tq] + bias_ref[c // 2, kind] for c in range(n_sub)],
                                    axis=1)
                smax = jnp.max(s, axis=0, keepdims=True)
            else:
                smax = jnp.max(s, axis=0, keepdims=True) + cfar
            m = smax if m is None else jnp.maximum(m, smax)
            s_tiles.append(s)

        m_far = m - cfar
        for t in range(n + 1):
            shift = m if t >= n - 1 else m_far
            p_scr[t * tq:(t + 1) * tq, :] = jnp.exp2(s_tiles[t] - shift).astype(BF16)

        keys = (n + 1) * tq
        ones = jnp.ones((SOFTMAX_SUM_ROWS, keys), BF16)
        outs = []
        for h in range(n_head):
            vte = jnp.concatenate([vt_ref[h * DH_V:(h + 1) * DH_V, 0:keys], ones], axis=0)
            pv = _dot(vte, p_scr[0:keys, h * 2 * tq:(h + 1) * 2 * tq])
            sub = [pv[0:DH_V, c * tq:(c + 1) * tq] / pv[DH_V:DH_V + 1, c * tq:(c + 1) * tq] for c in range(2)]
            o = sub[0] - lam * sub[1]
            ms = jnp.mean(o * o, axis=0, keepdims=True)
            outs.append(o * lax.rsqrt(ms + EPS))
        o = jnp.concatenate(outs, axis=0) * sg_ref[...] * (1.0 - lam_init)
        o_ref[n * tq:(n + 1) * tq, :] = o.T.astype(BF16)


def _decode_pages(first, last, lam, q_ref, kn_ref, vn_ref, bt_ref, b0_ref, sg_ref, k_refs, v_refs, o_ref,
                  m_scr, l_scr, acc_scr, *, lam_init):
    pp = len(k_refs)
    n_sub, d = acc_scr.shape
    page = k_refs[0].shape[1]

    row = lax.broadcasted_iota(jnp.int32, (n_sub, d), 0)
    col = lax.broadcasted_iota(jnp.int32, (n_sub, d), 1)
    qexp = jnp.where(col // DH_QK == row, jnp.broadcast_to(q_ref[...], (n_sub, d)), 0.0)

    if first:
        m_scr[...] = jnp.sum(qexp * kn_ref[...], axis=1, keepdims=True) + b0_ref[...]
        l_scr[...] = jnp.ones(l_scr.shape, F32)
        acc_scr[...] = jnp.broadcast_to(vn_ref[...], (n_sub, d))

    qb = qexp.astype(BF16)
    s = jnp.concatenate([_dot(qb, k_refs[i][...].astype(BF16)) for i in range(pp)], axis=1)
    s = s + bt_ref[...]
    m_old = m_scr[...]
    m_new = jnp.maximum(m_old, jnp.max(s, axis=1, keepdims=True))
    p = jnp.exp(s - m_new)
    alpha = jnp.exp(m_old - m_new)
    l_scr[...] = alpha * l_scr[...] + jnp.sum(p, axis=1, keepdims=True)
    pb = p.astype(BF16)
    pv = jnp.zeros((n_sub, d), F32)
    for i in range(pp):
        pv = pv + lax.dot_general(pb[:, i * page:(i + 1) * page], v_refs[i][...].astype(BF16),
                                  (((1,), (1,)), ((), ())), preferred_element_type=F32)
    acc_scr[...] = alpha * acc_scr[...] + pv
    m_scr[...] = m_new

    if last:
        o = acc_scr[...] / l_scr[...]
        head = col // DH_V
        a0 = jnp.sum(jnp.where(row == 2 * head, o, 0.0), axis=0, keepdims=True)
        a1 = jnp.sum(jnp.where(row == 2 * head + 1, o, 0.0), axis=0, keepdims=True)
        att = a0 - lam * a1
        sq = att * att
        head1 = head[0:1]
        r = jnp.zeros_like(att)
        for hh in range(d // DH_V):
            ms = jnp.sum(jnp.where(head1 == hh, sq, 0.0), axis=1, keepdims=True) * (1.0 / DH_V)
            r = jnp.where(head1 == hh, lax.rsqrt(ms + EPS), r)
        o_ref[...] = att * r * sg_ref[...] * (1.0 - lam_init)


def _attention_kernel(pt_ref, cfar_ref, lam_ref, sgc_ref, qt_ref, k_ref, vt_ref, bias_ref,
                      q_ref, kn_ref, vn_ref, bt_ref, b0_ref, sgr_ref, *rest, pp, tiles, lam_init, tq):
    del pt_ref
    k_refs, v_refs = rest[:pp], rest[pp:2 * pp]
    o_ref, od_ref, p_scr, m_scr, l_scr, acc_scr = rest[2 * pp:]
    hp = pl.program_id(1)
    half = pl.program_id(2)
    lam = _lambda(lam_ref, lam_init)
    for hv in range(2):
        @pl.when(half == hv)
        def _(hv=hv):
            _prompt_tiles(tiles[hv], hp, lam, cfar_ref, sgc_ref, qt_ref, k_ref, vt_ref, bias_ref, o_ref, p_scr,
                          lam_init=lam_init, tq=tq)
            _decode_pages(hv == 0, hv == 1, lam, q_ref, kn_ref, vn_ref, bt_ref, b0_ref, sgr_ref, k_refs, v_refs,
                          od_ref, m_scr, l_scr, acc_scr, lam_init=lam_init)


def _attention(page_table, cfar, lam_p, sg, qt, k, vt, bias_tiles, q, kn, vn, bias_t, bias0,
               cache_kt, cache_vt, layer, n_seq, seq, lam_init):
    d_qk, m = qt.shape
    n_dec, n_pages = page_table.shape
    d = q.shape[-1]
    page = cache_kt.shape[-1]
    pair = 2 * 2 * DH_QK
    n_pair = d_qk // pair
    nq = seq // TQ
    order = sorted(range(nq), reverse=True)
    tiles = ([n for i, n in enumerate(order) if i % 4 in (0, 3)], [n for i, n in enumerate(order) if i % 4 in (1, 2)])
    halves = 2
    pp = n_pages // halves
    assert n_pages % halves == 0 and n_seq * n_pair == n_dec, "one sample sequence per (prompt sequence, head pair)"
    kern = functools.partial(_attention_kernel, pp=pp, tiles=tiles, lam_init=lam_init, tq=TQ)

    def page_spec(i):
        return pl.BlockSpec((None, None, d, page),
                            lambda b, h, s, pt: (layer, pt[(b * n_pair + h) * n_pages + s * pp + i], 0, 0))

    const = lambda shape: pl.BlockSpec(shape, lambda b, h, s, pt: (0,) * len(shape))
    tok = pl.BlockSpec((None, 1, d), lambda b, h, s, pt: (b * n_pair + h, 0, 0))
    grid_spec = pltpu.PrefetchScalarGridSpec(
        num_scalar_prefetch=1,
        grid=(n_seq, n_pair, halves),
        in_specs=[pl.BlockSpec(memory_space=pltpu.SMEM),
                  const(lam_p.shape),
                  const((pair, 1)),
                  pl.BlockSpec((pair, seq), lambda b, h, s, pt: (h, b)),
                  pl.BlockSpec((seq, pair), lambda b, h, s, pt: (b, h)),
                  pl.BlockSpec((pair, seq), lambda b, h, s, pt: (h, b)),
                  pl.BlockSpec((2, 2, TQ, TQ), lambda b, h, s, pt: (h, 0, 0, 0)),
                  tok, tok, tok,
                  pl.BlockSpec((d // DH_QK, pp * page), lambda b, h, s, pt: (0, s)),
                  const((d // DH_QK, 1)),
                  const((1, d))]
                 + [page_spec(i) for i in range(pp)] + [page_spec(i) for i in range(pp)],
        out_specs=[pl.BlockSpec((seq, pair), lambda b, h, s, pt: (b, h)), tok],
        scratch_shapes=[pltpu.VMEM((seq, (pair // DH_QK) * TQ), BF16),
                        pltpu.VMEM((d // DH_QK, 1), F32), pltpu.VMEM((d // DH_QK, 1), F32),
                        pltpu.VMEM((d // DH_QK, d), F32)],
    )
    att_p, att_s = pl.pallas_call(
        kern,
        grid_spec=grid_spec,
        out_shape=[jax.ShapeDtypeStruct((m, d_qk), BF16), jax.ShapeDtypeStruct((n_dec, 1, d), F32)],
        compiler_params=_params(("parallel", "parallel", "arbitrary")),
        name="attention",
    )(page_table.reshape(-1), cfar, lam_p, sg[:pair].reshape(-1, 1), qt, k, vt, bias_tiles,
      q.reshape(n_dec, 1, d), kn.reshape(n_dec, 1, d), vn.reshape(n_dec, 1, d), bias_t, bias0, sg.reshape(1, -1),
      *([cache_kt] * pp), *([cache_vt] * pp))
    return att_p, att_s.reshape(n_dec, d)


def _layernorm_swish(c, g, b):
    mu = jnp.mean(c, axis=-1, keepdims=True)
    xc = c - mu
    y = xc * lax.rsqrt(jnp.mean(xc * xc, axis=-1, keepdims=True) + EPS) * g + b
    return jax.nn.silu(y)


def _odd_prompt_kernel(x_ref, g_ref, w1_ref, b1_ref, cw_ref, cb_ref, lg_ref, lb_ref, w2_ref, b2_ref,
                       o_ref, st_ref, u_scr, c_scr, *, tiles_per_seq, width):
    i = pl.program_id(0)
    tm, d = x_ref.shape
    nl = d // LANE
    x = x_ref[...]
    h = _rms(x, g_ref[...]).astype(BF16)
    z = _dot(h, w1_ref[...]) + b1_ref[...]
    u = z[:, :d] * jax.nn.sigmoid(z[:, d:])

    @pl.when(i % tiles_per_seq == 0)
    def _():
        u_scr[0:HALO * nl, :] = jnp.zeros((HALO * nl, LANE), F32)

    for l in range(nl):
        u_scr[pl.ds(HALO * nl + l, tm, stride=nl), :] = u[:, l * LANE:(l + 1) * LANE]

    @pl.when(i % tiles_per_seq == tiles_per_seq - 1)
    def _():
        st_ref[...] = u[tm - (width - 1):, :]

    first = HALO - (width - 1)
    taps = [cw_ref[j] for j in range(width)]

    def chunk(r, carry):
        t0 = r * CONV_UNROLL
        slab = lambda t: pl.ds(pl.multiple_of(t * nl, nl), nl)
        acc = [None] * CONV_UNROLL
        for k in range(CONV_UNROLL + width - 1):
            row = u_scr[slab(t0 + first + k), :]
            for o in range(max(0, k - width + 1), min(CONV_UNROLL, k + 1)):
                term = taps[k - o] * row
                acc[o] = term if acc[o] is None else acc[o] + term
        for o in range(CONV_UNROLL):
            c_scr[slab(t0 + o), :] = acc[o]
        return carry

    lax.fori_loop(0, tm // CONV_UNROLL, chunk, 0)

    u_scr[0:HALO * nl, :] = u_scr[tm * nl:(tm + HALO) * nl, :]
    c = jnp.concatenate([c_scr[pl.ds(l, tm, stride=nl), :] for l in range(nl)], axis=1)
    a = _layernorm_swish(c + cb_ref[...], lg_ref[...], lb_ref[...]).astype(BF16)
    o_ref[...] = x + _dot(a, w2_ref[...]) + b2_ref[...]


def _odd_prompt(x, g, w1, w2, layer, b1, cw, cb, lg, lb, b2, n_seq, seq):
    m, d = x.shape
    width = cw.shape[0]
    tm = TM_ODD
    tps = seq // tm
    row = pl.BlockSpec((tm, d), lambda i: (i, 0))
    vec = lambda a: a.reshape(1, -1)
    nl = d // LANE
    assert tm >= width - 1 and tm % CONV_UNROLL == 0 and HALO >= width - 1
    kern = functools.partial(_odd_prompt_kernel, tiles_per_seq=tps, width=width)
    return pl.pallas_call(
        kern,
        grid=(m // tm,),
        in_specs=[row, _resident((1, d)), _resident_layer(w1, layer), _resident((1, 2 * d)),
                  _resident((width, nl, LANE)),
                  _resident((1, d)), _resident((1, d)), _resident((1, d)), _resident_layer(w2, layer),
                  _resident((1, d))],
        out_specs=[row, pl.BlockSpec((None, width - 1, d), lambda i: (i // tps, 0, 0))],
        out_shape=[jax.ShapeDtypeStruct((m, d), F32),
                   jax.ShapeDtypeStruct((n_seq, width - 1, d), F32)],
        scratch_shapes=[pltpu.VMEM(((tm + HALO) * nl, LANE), F32), pltpu.VMEM((tm * nl, LANE), F32)],
        compiler_params=_params(("arbitrary",)),
        name="odd_prompt",
    )(x, vec(g), w1, vec(b1), cw.reshape(width, nl, LANE), vec(cb), vec(lg), vec(lb), w2, vec(b2))


def _odd_sample_kernel(x_ref, g_ref, w1_ref, b1_ref, cw_ref, cb_ref, lg_ref, lb_ref, w2_ref, b2_ref,
                       st_ref, o_ref, nst_ref):
    n_hist = st_ref.shape[0]
    d = x_ref.shape[1]
    x = x_ref[...]
    h = _rms(x, g_ref[...]).astype(BF16)
    z = _dot(h, w1_ref[...]) + b1_ref[...]
    u = z[:, :d] * jax.nn.sigmoid(z[:, d:])
    c = cw_ref[n_hist:n_hist + 1, :] * u
    for j in range(n_hist):
        c = c + cw_ref[j:j + 1, :] * st_ref[j]
    for j in range(n_hist - 1):
        nst_ref[j] = st_ref[j + 1]
    nst_ref[n_hist - 1] = u
    a = _layernorm_swish(c + cb_ref[...], lg_ref[...], lb_ref[...]).astype(BF16)
    o_ref[...] = x + _dot(a, w2_ref[...]) + b2_ref[...]


def _odd_sample(x, g, w1, w2, layer, b1, cw, cb, lg, lb, b2, state_t):
    m, d = x.shape
    vec = lambda a: a.reshape(1, -1)
    whole = lambda shape: pl.BlockSpec(shape, lambda i: (0,) * len(shape))
    return pl.pallas_call(
        _odd_sample_kernel,
        grid=(1,),
        in_specs=[_resident((m, d)), _resident((1, d)), _resident_layer(w1, layer), _resident((1, 2 * d)),
                  _resident(cw.shape), _resident((1, d)), _resident((1, d)), _resident((1, d)),
                  _resident_layer(w2, layer), _resident((1, d)), _resident(state_t.shape)],
        out_specs=[whole((m, d)), whole(state_t.shape)],
        out_shape=[jax.ShapeDtypeStruct((m, d), F32), jax.ShapeDtypeStruct(state_t.shape, F32)],
        compiler_params=_params(("arbitrary",)),
        name="odd_sample",
    )(x, vec(g), w1, vec(b1), cw, vec(cb), vec(lg), vec(lb), w2, vec(b2), state_t)


def _bias_by_distance(rel_bias, dist):
    n = jnp.maximum(dist, 0)
    nf = jnp.maximum(n, 1).astype(F32)
    large = MAX_EXACT + (jnp.log(nf / MAX_EXACT) / math.log(MAX_DISTANCE / MAX_EXACT)
                         * (N_BUCKETS - MAX_EXACT)).astype(jnp.int32)
    large = jnp.minimum(large, N_BUCKETS - 1)
    return rel_bias[jnp.where(n < MAX_EXACT, n, large)]


def _prompt_bias_tiles(rel_bias):
    assert TQ >= FAR - 1
    n_head = rel_bias.shape[1]
    by_dist = _bias_by_distance(rel_bias, jnp.arange(2 * TQ)).T * LOG2E
    diag = jnp.concatenate([jnp.full((n_head, TQ - 1), NEG, F32), by_dist[:, :TQ]], axis=1)
    near = by_dist[:, 1:]

    def toeplitz(a):
        period = a.shape[1]
        rows = jnp.tile(a, (1, TQ + 1))[:, :TQ * (period + 1)].reshape(n_head, TQ, period + 1)
        return rows[:, ::-1, :TQ]

    tiles = jnp.stack([toeplitz(near), toeplitz(diag)], axis=1)
    return tiles, by_dist[:, TQ + 1]


def kernel(x_prompt, x_sample, cache_k, cache_v, state_conv_b, state_conv_c, page_table, rel_bias, norm_ffn1, ffn1_w_gate, ffn1_w_up, ffn1_w_down, norm_mix, norm_ffn2, ffn2_w_gate, ffn2_w_up, ffn2_w_down, w_in_even, w_out_even, q_norm_gain, k_norm_gain, lambda_q1, lambda_k1, lambda_q2, lambda_k2, subln_gain, conv_b_w, w_pw1, b_pw1, conv_c_w, conv_c_b, ln_c_gain, ln_c_bias, w_pw2, b_pw2):
    bp, tp, d = x_prompt.shape
    bs = x_sample.shape[0]
    depth = norm_ffn1.shape[0]
    n_even, n_phys, page, n_sub, dh_qk = cache_k.shape
    assert dh_qk == DH_QK and cache_v.shape[-1] == DH_V and x_sample.shape[1] == 1
    d_qk = n_sub * dh_qk
    d_a = cache_v.shape[3] * DH_V
    past_len = page_table.shape[1] * page
    assert tp % TM_ROWS == 0 and tp % TM_ODD == 0 and tp % TQ == 0 and tp >= 2 * TQ

    bf = lambda w: w.astype(BF16)
    xp = x_prompt.reshape(bp * tp, d)
    xs = x_sample.reshape(bs, d)

    cache_kt = jnp.transpose(cache_k, (0, 1, 3, 4, 2)).reshape(n_even, n_phys, d_qk, page)
    cache_vt = jnp.transpose(cache_v, (0, 1, 3, 4, 2)).reshape(n_even, n_phys, d_a, page)
    state_c_t = jnp.transpose(state_conv_c, (0, 2, 1, 3))

    p_avg = jnp.kron(jnp.eye(d_qk // DH_QK, dtype=F32), jnp.full((DH_QK, DH_QK), 1.0 / DH_QK, F32)).astype(BF16)
    bias_tiles, bias_far = _prompt_bias_tiles(rel_bias)
    k_pos = jnp.arange(past_len + 1)
    bias_dec = jnp.repeat(_bias_by_distance(rel_bias, past_len - k_pos), 2, axis=1).T
    bias_dec_past, bias_dec_new = bias_dec[:, :past_len], bias_dec[:, past_len:]

    ffn1_w, ffn2_w = [tuple(w.astype(BF16) for w in ws) for ws in
                      ((ffn1_w_gate, ffn1_w_up, ffn1_w_down), (ffn2_w_gate, ffn2_w_up, ffn2_w_down))]
    w_in, w_out, w1, w2 = (w.astype(BF16) for w in (w_in_even, w_out_even, w_pw1, w_pw2))

    kv_prompt = ()
    ks_l, vs_l, cbp_l, cbs_l, ccp_l, ccs_l = [], [], [], [], [], []
    for li in range(depth):
        xp = _ffn(xp, norm_ffn1[li], li, *ffn1_w)
        xs = _ffn(xs, norm_ffn1[li], li, *ffn1_w)
        if li % 2 == 0:
            e = li // 2
            lam_init = 0.8 - 0.6 * math.exp(-0.3 * li)
            qg = jnp.tile(q_norm_gain[e], d_qk // DH_QK).reshape(1, d_qk)
            kg = jnp.tile(k_norm_gain[e], d_qk // DH_QK).reshape(1, d_qk)
            lam_p = jnp.stack([lambda_q1[e], lambda_k1[e], lambda_q2[e], lambda_k2[e]])
            sg = jnp.tile(subln_gain[e], d_a // DH_V)

            *kv_prompt, qtb, kb, vtb, gcp, cbp = _even_in_prompt(
                xp, norm_mix[li], w_in, e, n_even, tuple(kv_prompt), p_avg, qg, kg, conv_b_w[e], bp, tp)
            qs, kn, vn, gcs, us = _even_in_sample(
                xs, norm_mix[li], w_in, e, p_avg, qg, kg, conv_b_w[e],
                state_conv_b[e, :, 0], state_conv_b[e, :, 1])
            attp, atts = _attention(page_table, bias_far, lam_p, sg, qtb, kb, vtb, bias_tiles,
                                    qs, kn, vn, bias_dec_past, bias_dec_new, cache_kt, cache_vt, e,
                                    bp, tp, lam_init)
            mix_p, mix_s = (attp, gcp, w_out, e), (atts, gcs, w_out, e)

            ks_l.append(kn)
            vs_l.append(vn)
            cbp_l.append(cbp)
            cbs_l.append(jnp.stack([state_conv_b[e, :, 1], us], axis=1))
        else:
            o = li // 2
            prm = (norm_mix[li], w1, w2, o, b_pw1[o], conv_c_w[o], conv_c_b[o], ln_c_gain[o], ln_c_bias[o], b_pw2[o])
            xp, ccp = _odd_prompt(xp, *prm, bp, tp)
            xs, ccs = _odd_sample(xs, *prm, state_c_t[o])
            ccp_l.append(ccp)
            ccs_l.append(ccs)
            mix_p = mix_s = ()
        xp = _ffn(xp, norm_ffn2[li], li, *ffn2_w, mixer=mix_p)
        xs = _ffn(xs, norm_ffn2[li], li, *ffn2_w, mixer=mix_s)

    n_h = d_a // DH_V
    kt_all, vt_all = kv_prompt
    new_k_prompt = jnp.transpose(kt_all.reshape(n_even, bp, n_sub, DH_QK, tp), (0, 1, 4, 2, 3))
    new_v_prompt = jnp.transpose(vt_all.reshape(n_even, bp, n_h, DH_V, tp), (0, 1, 4, 2, 3))
    return (xp.reshape(bp, tp, d), xs.reshape(bs, 1, d),
            new_k_prompt, new_v_prompt,
            jnp.stack(ks_l).reshape(n_even, bs, 1, n_sub, DH_QK),
            jnp.stack(vs_l).reshape(n_even, bs, 1, n_h, DH_V),
            jnp.stack(cbp_l), jnp.stack(cbs_l), jnp.stack(ccp_l),
            jnp.transpose(jnp.stack(ccs_l), (0, 2, 1, 3)))
```

```python
import functools
import math

import jax
import jax.numpy as jnp
from jax import lax
from jax.experimental import pallas as pl
from jax.experimental.pallas import tpu as pltpu

F32 = jnp.float32
BF16 = jnp.bfloat16
EPS = 1e-6
NEG = -1e30

LANE = 128
SUBLANE = 8
VMEM_LIMIT_BYTES = 56 * 1024 * 1024

DH_QK = 32
DH_V = 64
N_BUCKETS = 32
MAX_EXACT = N_BUCKETS // 2
MAX_DISTANCE = 128
FAR = MAX_DISTANCE

TM_ROWS = 512
TM_FFN = 512
FFN_CHUNKS = 2
CAST_STEPS = 16
TM_ODD = 256
TQ = 256
SOFTMAX_SUM_ROWS = 16
HALO = 32
CONV_UNROLL = 8
LOG2E = math.log2(math.e)


def _params(sem):
    return pltpu.CompilerParams(dimension_semantics=sem, vmem_limit_bytes=VMEM_LIMIT_BYTES)


def _resident(shape):
    n = len(shape)
    return pl.BlockSpec(shape, lambda *_: (0,) * n, pipeline_mode=pl.Buffered(1))


def _resident_layer(stacked, layer):
    n = stacked.ndim - 1
    return pl.BlockSpec((None,) + stacked.shape[1:], lambda *_: (layer,) + (0,) * n,
                        pipeline_mode=pl.Buffered(1))


def _rms(x, g):
    return x * lax.rsqrt(jnp.mean(x * x, axis=-1, keepdims=True) + EPS) * g


def _dot(a, b):
    return jnp.dot(a, b, preferred_element_type=F32)


def _group_rms(z, p_ref, gain):
    sq = z * z
    hi = sq.astype(BF16)
    lo = (sq - hi.astype(F32)).astype(BF16)
    ms = _dot(hi, p_ref[...]) + _dot(lo, p_ref[...])
    return z * lax.rsqrt(ms + EPS) * gain


def _ffn_kernel(x_ref, g_ref, wg_ref, wu_ref, wd_ref, *rest, n_chunks, n_mixer, n_cast):
    mixer, cast_src = rest[:n_mixer], rest[n_mixer:n_mixer + n_cast]
    o_ref, cast_dst = rest[n_mixer + n_cast], rest[n_mixer + n_cast + 1:]
    if n_cast:
        @pl.when(pl.program_id(0) < CAST_STEPS)
        def _():
            for src, dst in zip(cast_src, cast_dst):
                dst[...] = src[...].astype(BF16)
    x = x_ref[...]
    if n_mixer:
        a_ref, gc_ref, wo_ref = mixer
        d_a = a_ref.shape[1]
        x = x + _dot(a_ref[...].astype(BF16), wo_ref[0:d_a, :]) + _dot(gc_ref[...], wo_ref[d_a:, :])
    h = _rms(x, g_ref[...]).astype(BF16)
    fc = wg_ref.shape[1] // n_chunks
    y = jnp.zeros(x.shape, F32)
    for c in range(n_chunks):
        gate = _dot(h, wg_ref[:, c * fc:(c + 1) * fc])
        up = _dot(h, wu_ref[:, c * fc:(c + 1) * fc])
        a = (jax.nn.silu(gate) * up).astype(BF16)
        y = y + _dot(a, wd_ref[c * fc:(c + 1) * fc, :])
    o_ref[...] = x + 0.5 * y


def _ffn(x, g, weights, mixer=(), cast=()):
    m, d = x.shape
    tm = min(TM_FFN, m)
    steps = m // tm
    row = lambda width: pl.BlockSpec((tm, width), lambda i: (i, 0))
    mixer_specs = ([row(mixer[0].shape[1]), row(mixer[1].shape[1]), _resident_layer(mixer[2], mixer[3])]
                   if mixer else [])
    *stacks, cast_layer = cast if cast else (None,)
    assert not stacks or steps >= CAST_STEPS
    slab = lambda i: jnp.minimum(i, CAST_STEPS - 1)
    cast_in = [pl.BlockSpec((None, w.shape[1] // CAST_STEPS, w.shape[2]), lambda i: (cast_layer, slab(i), 0))
               for w in stacks]
    cast_out = [pl.BlockSpec((w.shape[1] // CAST_STEPS, w.shape[2]), lambda i: (slab(i), 0)) for w in stacks]
    out = pl.pallas_call(
        functools.partial(_ffn_kernel, n_chunks=FFN_CHUNKS, n_mixer=len(mixer_specs), n_cast=len(stacks)),
        grid=(steps,),
        in_specs=[row(d), _resident((1, d))] + [_resident(w.shape) for w in weights] + mixer_specs + cast_in,
        out_specs=[row(d)] + cast_out,
        out_shape=[jax.ShapeDtypeStruct((m, d), F32)] + [jax.ShapeDtypeStruct(w.shape[1:], BF16) for w in stacks],
        compiler_params=_params(("arbitrary",)),
        name="ffn_mixer" if mixer else "ffn",
    )(x, g.reshape(1, d), *weights, *mixer[:3], *stacks)
    return out[0], tuple(out[1:])


def _even_in_prompt_kernel(x_ref, g_ref, w_ref, p_ref, qg_ref, kg_ref, cw_ref, *rest,
                           tiles_per_seq, d_qk, d_a, d_b, scale):
    kt_ref, vt_ref, qtb_ref, kb_ref, vtb_ref, gc_ref, st_ref, u_scr = rest[-8:]
    i = pl.program_id(0)
    tm = x_ref.shape[0]
    h = _rms(x_ref[...], g_ref[...]).astype(BF16)
    z = _dot(h, w_ref[...])
    q = _group_rms(z[:, :d_qk], p_ref, qg_ref[...]) * scale
    k = _group_rms(z[:, d_qk:2 * d_qk], p_ref, kg_ref[...])
    o = 2 * d_qk
    v = z[:, o:o + d_a]
    o += d_a
    gate_b = z[:, o:o + d_b]
    u = z[:, o + d_b:o + 2 * d_b] * z[:, o + 2 * d_b:o + 3 * d_b]

    kt = k.T
    vt = v.T
    kt_ref[...] = kt
    vt_ref[...] = vt
    qtb_ref[...] = q.T.astype(BF16)
    kb_ref[...] = k.astype(BF16)
    vtb_ref[...] = vt.astype(BF16)

    @pl.when(i % tiles_per_seq == 0)
    def _():
        u_scr[0:SUBLANE, :] = jnp.zeros((SUBLANE, d_b), F32)

    u_scr[SUBLANE:SUBLANE + tm, :] = u
    conv = (cw_ref[0:1, :] * u_scr[SUBLANE - 2:SUBLANE - 2 + tm, :]
            + cw_ref[1:2, :] * u_scr[SUBLANE - 1:SUBLANE - 1 + tm, :]
            + cw_ref[2:3, :] * u)
    gc_ref[...] = (gate_b * conv).astype(BF16)
    st_ref[...] = u_scr[tm + SUBLANE - 2:tm + SUBLANE, :]
    u_scr[0:SUBLANE, :] = u_scr[tm:tm + SUBLANE, :]


def _even_in_prompt(x, g, w_in, layer, n_layers, kv_prev, p_avg, qg, kg, cw, n_seq, seq):
    m, d = x.shape
    d_qk = d_a = d_b = p_avg.shape[0]
    tm = TM_ROWS
    tps = seq // tm
    row = lambda width, dt=None: pl.BlockSpec((tm, width), lambda i: (i, 0))
    col = pl.BlockSpec((d_qk, tm), lambda i: (0, i))
    seq_t = pl.BlockSpec((None, None, d_qk, tm), lambda i: (layer, i // tps, 0, i % tps))
    n_in = 7
    kern = functools.partial(_even_in_prompt_kernel, tiles_per_seq=tps, d_qk=d_qk, d_a=d_a, d_b=d_b,
                             scale=DH_QK ** -0.5 * LOG2E)
    return pl.pallas_call(
        kern,
        grid=(m // tm,),
        in_specs=[row(d), _resident((1, d)), _resident_layer(w_in, layer), _resident(p_avg.shape),
                  _resident((1, d_qk)), _resident((1, d_qk)), _resident(cw.shape)]
                 + [pl.BlockSpec(memory_space=pl.ANY)] * len(kv_prev),
        out_specs=[seq_t, seq_t, col, row(d_qk), col, row(d_b),
                   pl.BlockSpec((None, 2, d_b), lambda i: (i // tps, 0, 0))],
        out_shape=[jax.ShapeDtypeStruct((n_layers, n_seq, d_qk, seq), F32),
                   jax.ShapeDtypeStruct((n_layers, n_seq, d_a, seq), F32),
                   jax.ShapeDtypeStruct((d_qk, m), BF16),
                   jax.ShapeDtypeStruct((m, d_qk), BF16),
                   jax.ShapeDtypeStruct((d_a, m), BF16),
                   jax.ShapeDtypeStruct((m, d_b), BF16),
                   jax.ShapeDtypeStruct((n_seq, 2, d_b), F32)],
        scratch_shapes=[pltpu.VMEM((tm + SUBLANE, d_b), F32)],
        input_output_aliases={n_in + j: j for j in range(len(kv_prev))},
        compiler_params=_params(("arbitrary",)),
        name="even_in_prompt",
    )(x, g.reshape(1, d), w_in, p_avg, qg, kg, cw, *kv_prev)


def _even_in_sample_kernel(x_ref, g_ref, w_ref, p_ref, qg_ref, kg_ref, cw_ref, s0_ref, s1_ref,
                           q_ref, k_ref, v_ref, gc_ref, u_ref, *, d_qk, d_a, d_b, scale):
    h = _rms(x_ref[...], g_ref[...]).astype(BF16)
    z = _dot(h, w_ref[...])
    q_ref[...] = _group_rms(z[:, :d_qk], p_ref, qg_ref[...]) * scale
    k_ref[...] = _group_rms(z[:, d_qk:2 * d_qk], p_ref, kg_ref[...])
    o = 2 * d_qk
    v_ref[...] = z[:, o:o + d_a]
    o += d_a
    gate_b = z[:, o:o + d_b]
    u = z[:, o + d_b:o + 2 * d_b] * z[:, o + 2 * d_b:o + 3 * d_b]
    conv = cw_ref[0:1, :] * s0_ref[...] + cw_ref[1:2, :] * s1_ref[...] + cw_ref[2:3, :] * u
    gc_ref[...] = (gate_b * conv).astype(BF16)
    u_ref[...] = u


def _even_in_sample(x, g, w_in, layer, p_avg, qg, kg, cw, s0, s1):
    m, d = x.shape
    d_qk = d_a = d_b = p_avg.shape[0]
    kern = functools.partial(_even_in_sample_kernel, d_qk=d_qk, d_a=d_a, d_b=d_b, scale=DH_QK ** -0.5)
    f = lambda dt: jax.ShapeDtypeStruct((m, d_qk), dt)
    blk = _resident((m, d_qk))
    return pl.pallas_call(
        kern,
        grid=(1,),
        in_specs=[_resident((m, d)), _resident((1, d)), _resident_layer(w_in, layer), _resident(p_avg.shape),
                  _resident((1, d_qk)), _resident((1, d_qk)), _resident(cw.shape), blk, blk],
        out_specs=[pl.BlockSpec((m, d_qk), lambda i: (0, 0))] * 5,
        out_shape=[f(F32), f(F32), f(F32), f(BF16), f(F32)],
        compiler_params=_params(("arbitrary",)),
        name="even_in_sample",
    )(x, g.reshape(1, d), w_in, p_avg, qg, kg, cw, s0, s1)


def _lambda(lam_ref, lam_init):
    lam_p = lam_ref[...]
    return (jnp.exp(jnp.sum(lam_p[0:1] * lam_p[1:2], axis=1, keepdims=True))
            - jnp.exp(jnp.sum(lam_p[2:3] * lam_p[3:4], axis=1, keepdims=True)) + lam_init)


def _prompt_tiles(tiles, hp, lam, cfar_ref, sg_ref, qt_ref, k_ref, vt_ref, bias_ref, o_ref, p_scr, *,
                  lam_init, tq):
    n_sub = qt_ref.shape[0] // DH_QK
    n_head = n_sub // 2
    cfar = jnp.concatenate([jnp.full((1, 2 * tq), cfar_ref[hp * n_head + h], F32) for h in range(n_head)],
                           axis=1)
    grp = lax.broadcasted_iota(jnp.int32, (qt_ref.shape[0], tq), 0) // DH_QK

    def scores(n):
        qt = qt_ref[:, n * tq:(n + 1) * tq]
        q_blk = jnp.concatenate([jnp.where(grp == c, qt, jnp.zeros_like(qt)) for c in range(n_sub)], axis=1)
        return _dot(k_ref[0:(n + 1) * tq, :], q_blk)

    s_next = scores(tiles[0])
    for idx, n in enumerate(tiles):
        s_all = s_next
        if idx + 1 < len(tiles):
            s_next = scores(tiles[idx + 1])
        p_buf = p_scr.at[idx % 2]

        m = None
        s_tiles = []
        for t in range(n + 1):
            s = s_all[t * tq:(t + 1) * tq, :]
            if t >= n - 1:
                kind = 1 if t == n else 0
                s = jnp.concatenate([s[:, c * tq:(c + 1) * tq] + bias_ref[c // 2, kind] for c in range(n_sub)],
                                    axis=1)
                smax = jnp.max(s, axis=0, keepdims=True)
            else:
                smax = jnp.max(s, axis=0, keepdims=True) + cfar
            m = smax if m is None else jnp.maximum(m, smax)
            s_tiles.append(s)

        m_far = m - cfar
        for t in range(n + 1):
            shift = m if t >= n - 1 else m_far
            p_buf[t * tq:(t + 1) * tq, :] = jnp.exp2(s_tiles[t] - shift).astype(BF16)

        keys = (n + 1) * tq
        ones = jnp.ones((SOFTMAX_SUM_ROWS, keys), BF16)
        outs = []
        for h in range(n_head):
            vte = jnp.concatenate([vt_ref[h * DH_V:(h + 1) * DH_V, 0:keys], ones], axis=0)
            pv = _dot(vte, p_buf[0:keys, h * 2 * tq:(h + 1) * 2 * tq])
            sub = [pv[0:DH_V, c * tq:(c + 1) * tq] / pv[DH_V:DH_V + 1, c * tq:(c + 1) * tq] for c in range(2)]
            o = sub[0] - lam * sub[1]
            ms = jnp.mean(o * o, axis=0, keepdims=True)
            outs.append(o * lax.rsqrt(ms + EPS))
        o = jnp.concatenate(outs, axis=0) * sg_ref[...] * (1.0 - lam_init)
        o_ref[n * tq:(n + 1) * tq, :] = o.T.astype(BF16)


def _decode_pages(first, last, lam, q_ref, kn_ref, vn_ref, bt_ref, b0_ref, sg_ref, k_refs, v_refs, o_ref,
                  m_scr, l_scr, acc_scr, *, lam_init):
    pp = len(k_refs)
    n_sub, d = acc_scr.shape
    page = k_refs[0].shape[1]

    row = lax.broadcasted_iota(jnp.int32, (n_sub, d), 0)
    col = lax.broadcasted_iota(jnp.int32, (n_sub, d), 1)
    qexp = jnp.where(col // DH_QK == row, jnp.broadcast_to(q_ref[...], (n_sub, d)), 0.0)

    if first:
        m_scr[...] = jnp.sum(qexp * kn_ref[...], axis=1, keepdims=True) + b0_ref[...]
        l_scr[...] = jnp.ones(l_scr.shape, F32)
        acc_scr[...] = jnp.broadcast_to(vn_ref[...], (n_sub, d))

    qb = qexp.astype(BF16)
    s = jnp.concatenate([_dot(qb, k_refs[i][...].astype(BF16)) for i in range(pp)], axis=1)
    s = s + bt_ref[...]
    m_old = m_scr[...]
    m_new = jnp.maximum(m_old, jnp.max(s, axis=1, keepdims=True))
    p = jnp.exp(s - m_new)
    alpha = jnp.exp(m_old - m_new)
    l_scr[...] = alpha * l_scr[...] + jnp.sum(p, axis=1, keepdims=True)
    pb = p.astype(BF16)
    pv = jnp.zeros((n_sub, d), F32)
    for i in range(pp):
        pv = pv + lax.dot_general(pb[:, i * page:(i + 1) * page], v_refs[i][...].astype(BF16),
                                  (((1,), (1,)), ((), ())), preferred_element_type=F32)
    acc_scr[...] = alpha * acc_scr[...] + pv
    m_scr[...] = m_new

    if last:
        o = acc_scr[...] / l_scr[...]
        head = col // DH_V
        a0 = jnp.sum(jnp.where(row == 2 * head, o, 0.0), axis=0, keepdims=True)
        a1 = jnp.sum(jnp.where(row == 2 * head + 1, o, 0.0), axis=0, keepdims=True)
        att = a0 - lam * a1
        sq = att * att
        head1 = head[0:1]
        r = jnp.zeros_like(att)
        for hh in range(d // DH_V):
            ms = jnp.sum(jnp.where(head1 == hh, sq, 0.0), axis=1, keepdims=True) * (1.0 / DH_V)
            r = jnp.where(head1 == hh, lax.rsqrt(ms + EPS), r)
        o_ref[...] = att * r * sg_ref[...] * (1.0 - lam_init)


def _attention_kernel(pt_ref, cfar_ref, lam_ref, sgc_ref, qt_ref, k_ref, vt_ref, bias_ref,
                      q_ref, kn_ref, vn_ref, bt_ref, b0_ref, sgr_ref, *rest, pp, tiles, lam_init, tq):
    del pt_ref
    k_refs, v_refs = rest[:pp], rest[pp:2 * pp]
    o_ref, od_ref, p_scr, m_scr, l_scr, acc_scr = rest[2 * pp:]
    hp = pl.program_id(1)
    half = pl.program_id(2)
    lam = _lambda(lam_ref, lam_init)
    for hv in range(2):
        @pl.when(half == hv)
        def _(hv=hv):
            _prompt_tiles(tiles[hv], hp, lam, cfar_ref, sgc_ref, qt_ref, k_ref, vt_ref, bias_ref, o_ref, p_scr,
                          lam_init=lam_init, tq=tq)
            _decode_pages(hv == 0, hv == 1, lam, q_ref, kn_ref, vn_ref, bt_ref, b0_ref, sgr_ref, k_refs, v_refs,
                          od_ref, m_scr, l_scr, acc_scr, lam_init=lam_init)


def _attention(page_table, cfar, lam_p, sg, qt, k, vt, bias_tiles, q, kn, vn, bias_t, bias0,
               cache_kt, cache_vt, layer, n_seq, seq, lam_init):
    d_qk, m = qt.shape
    n_dec, n_pages = page_table.shape
    d = q.shape[-1]
    page = cache_kt.shape[-1]
    pair = 2 * 2 * DH_QK
    n_pair = d_qk // pair
    nq = seq // TQ
    order = sorted(range(nq), reverse=True)
    tiles = ([n for i, n in enumerate(order) if i % 4 in (0, 3)], [n for i, n in enumerate(order) if i % 4 in (1, 2)])
    halves = 2
    pp = n_pages // halves
    assert n_pages % halves == 0 and n_seq * n_pair == n_dec, "one sample sequence per (prompt sequence, head pair)"
    kern = functools.partial(_attention_kernel, pp=pp, tiles=tiles, lam_init=lam_init, tq=TQ)

    def page_spec(i):
        return pl.BlockSpec((None, None, d, page),
                            lambda b, h, s, pt: (layer, pt[(b * n_pair + h) * n_pages + s * pp + i], 0, 0))

    const = lambda shape: pl.BlockSpec(shape, lambda b, h, s, pt: (0,) * len(shape))
    tok = pl.BlockSpec((None, 1, d), lambda b, h, s, pt: (b * n_pair + h, 0, 0))
    grid_spec = pltpu.PrefetchScalarGridSpec(
        num_scalar_prefetch=1,
        grid=(n_seq, n_pair, halves),
        in_specs=[pl.BlockSpec(memory_space=pltpu.SMEM),
                  const(lam_p.shape),
                  const((pair, 1)),
                  pl.BlockSpec((pair, seq), lambda b, h, s, pt: (h, b)),
                  pl.BlockSpec((seq, pair), lambda b, h, s, pt: (b, h)),
                  pl.BlockSpec((pair, seq), lambda b, h, s, pt: (h, b)),
                  pl.BlockSpec((2, 2, TQ, TQ), lambda b, h, s, pt: (h, 0, 0, 0)),
                  tok, tok, tok,
                  pl.BlockSpec((d // DH_QK, pp * page), lambda b, h, s, pt: (0, s)),
                  const((d // DH_QK, 1)),
                  const((1, d))]
                 + [page_spec(i) for i in range(pp)] + [page_spec(i) for i in range(pp)],
        out_specs=[pl.BlockSpec((seq, pair), lambda b, h, s, pt: (b, h)), tok],
        scratch_shapes=[pltpu.VMEM((2, seq, (pair // DH_QK) * TQ), BF16),
                        pltpu.VMEM((d // DH_QK, 1), F32), pltpu.VMEM((d // DH_QK, 1), F32),
                        pltpu.VMEM((d // DH_QK, d), F32)],
    )
    att_p, att_s = pl.pallas_call(
        kern,
        grid_spec=grid_spec,
        out_shape=[jax.ShapeDtypeStruct((m, d_qk), BF16), jax.ShapeDtypeStruct((n_dec, 1, d), F32)],
        compiler_params=_params(("parallel", "parallel", "arbitrary")),
        name="attention",
    )(page_table.reshape(-1), cfar, lam_p, sg[:pair].reshape(-1, 1), qt, k, vt, bias_tiles,
      q.reshape(n_dec, 1, d), kn.reshape(n_dec, 1, d), vn.reshape(n_dec, 1, d), bias_t, bias0, sg.reshape(1, -1),
      *([cache_kt] * pp), *([cache_vt] * pp))
    return att_p, att_s.reshape(n_dec, d)


def _layernorm_swish(c, g, b):
    mu = jnp.mean(c, axis=-1, keepdims=True)
    xc = c - mu
    y = xc * lax.rsqrt(jnp.mean(xc * xc, axis=-1, keepdims=True) + EPS) * g + b
    return jax.nn.silu(y)


def _odd_prompt_kernel(x_ref, g_ref, w1_ref, b1_ref, cw_ref, cb_ref, lg_ref, lb_ref, w2_ref, b2_ref,
                       o_ref, st_ref, u_scr, c_scr, *, tiles_per_seq, width):
    i = pl.program_id(0)
    tm, d = x_ref.shape
    nl = d // LANE
    x = x_ref[...]
    h = _rms(x, g_ref[...]).astype(BF16)
    z = _dot(h, w1_ref[...]) + b1_ref[...]
    u = z[:, :d] * jax.nn.sigmoid(z[:, d:])

    @pl.when(i % tiles_per_seq == 0)
    def _():
        u_scr[0:HALO * nl, :] = jnp.zeros((HALO * nl, LANE), F32)

    for l in range(nl):
        u_scr[pl.ds(HALO * nl + l, tm, stride=nl), :] = u[:, l * LANE:(l + 1) * LANE]

    @pl.when(i % tiles_per_seq == tiles_per_seq - 1)
    def _():
        st_ref[...] = u[tm - (width - 1):, :]

    first = HALO - (width - 1)
    taps = [cw_ref[j] for j in range(width)]

    def chunk(r, carry):
        t0 = r * CONV_UNROLL
        slab = lambda t: pl.ds(pl.multiple_of(t * nl, nl), nl)
        acc = [None] * CONV_UNROLL
        for k in range(CONV_UNROLL + width - 1):
            row = u_scr[slab(t0 + first + k), :]
            for o in range(max(0, k - width + 1), min(CONV_UNROLL, k + 1)):
                term = taps[k - o] * row
                acc[o] = term if acc[o] is None else acc[o] + term
        for o in range(CONV_UNROLL):
            c_scr[slab(t0 + o), :] = acc[o]
        return carry

    lax.fori_loop(0, tm // CONV_UNROLL, chunk, 0)

    u_scr[0:HALO * nl, :] = u_scr[tm * nl:(tm + HALO) * nl, :]
    c = jnp.concatenate([c_scr[pl.ds(l, tm, stride=nl), :] for l in range(nl)], axis=1)
    a = _layernorm_swish(c + cb_ref[...], lg_ref[...], lb_ref[...]).astype(BF16)
    o_ref[...] = x + _dot(a, w2_ref[...]) + b2_ref[...]


def _odd_prompt(x, g, w1, w2, layer, b1, cw, cb, lg, lb, b2, n_seq, seq):
    m, d = x.shape
    width = cw.shape[0]
    tm = TM_ODD
    tps = seq // tm
    row = pl.BlockSpec((tm, d), lambda i: (i, 0))
    vec = lambda a: a.reshape(1, -1)
    nl = d // LANE
    assert tm >= width - 1 and tm % CONV_UNROLL == 0 and HALO >= width - 1
    kern = functools.partial(_odd_prompt_kernel, tiles_per_seq=tps, width=width)
    return pl.pallas_call(
        kern,
        grid=(m // tm,),
        in_specs=[row, _resident((1, d)), _resident_layer(w1, layer), _resident((1, 2 * d)),
                  _resident((width, nl, LANE)),
                  _resident((1, d)), _resident((1, d)), _resident((1, d)), _resident_layer(w2, layer),
                  _resident((1, d))],
        out_specs=[row, pl.BlockSpec((None, width - 1, d), lambda i: (i // tps, 0, 0))],
        out_shape=[jax.ShapeDtypeStruct((m, d), F32),
                   jax.ShapeDtypeStruct((n_seq, width - 1, d), F32)],
        scratch_shapes=[pltpu.VMEM(((tm + HALO) * nl, LANE), F32), pltpu.VMEM((tm * nl, LANE), F32)],
        compiler_params=_params(("arbitrary",)),
        name="odd_prompt",
    )(x, vec(g), w1, vec(b1), cw.reshape(width, nl, LANE), vec(cb), vec(lg), vec(lb), w2, vec(b2))


def _odd_sample_kernel(x_ref, g_ref, w1_ref, b1_ref, cw_ref, cb_ref, lg_ref, lb_ref, w2_ref, b2_ref,
                       st_ref, o_ref, nst_ref):
    n_hist = st_ref.shape[0]
    d = x_ref.shape[1]
    x = x_ref[...]
    h = _rms(x, g_ref[...]).astype(BF16)
    z = _dot(h, w1_ref[...]) + b1_ref[...]
    u = z[:, :d] * jax.nn.sigmoid(z[:, d:])
    c = cw_ref[n_hist:n_hist + 1, :] * u
    for j in range(n_hist):
        c = c + cw_ref[j:j + 1, :] * st_ref[j]
    for j in range(n_hist - 1):
        nst_ref[j] = st_ref[j + 1]
    nst_ref[n_hist - 1] = u
    a = _layernorm_swish(c + cb_ref[...], lg_ref[...], lb_ref[...]).astype(BF16)
    o_ref[...] = x + _dot(a, w2_ref[...]) + b2_ref[...]


def _odd_sample(x, g, w1, w2, layer, b1, cw, cb, lg, lb, b2, state_t):
    m, d = x.shape
    vec = lambda a: a.reshape(1, -1)
    whole = lambda shape: pl.BlockSpec(shape, lambda i: (0,) * len(shape))
    return pl.pallas_call(
        _odd_sample_kernel,
        grid=(1,),
        in_specs=[_resident((m, d)), _resident((1, d)), _resident_layer(w1, layer), _resident((1, 2 * d)),
                  _resident(cw.shape), _resident((1, d)), _resident((1, d)), _resident((1, d)),
                  _resident_layer(w2, layer), _resident((1, d)), _resident(state_t.shape)],
        out_specs=[whole((m, d)), whole(state_t.shape)],
        out_shape=[jax.ShapeDtypeStruct((m, d), F32), jax.ShapeDtypeStruct(state_t.shape, F32)],
        compiler_params=_params(("arbitrary",)),
        name="odd_sample",
    )(x, vec(g), w1, vec(b1), cw, vec(cb), vec(lg), vec(lb), w2, vec(b2), state_t)


def _bias_by_distance(rel_bias, dist):
    n = jnp.maximum(dist, 0)
    nf = jnp.maximum(n, 1).astype(F32)
    large = MAX_EXACT + (jnp.log(nf / MAX_EXACT) / math.log(MAX_DISTANCE / MAX_EXACT)
                         * (N_BUCKETS - MAX_EXACT)).astype(jnp.int32)
    large = jnp.minimum(large, N_BUCKETS - 1)
    return rel_bias[jnp.where(n < MAX_EXACT, n, large)]


def _prompt_bias_tiles(rel_bias):
    assert TQ >= FAR - 1
    n_head = rel_bias.shape[1]
    by_dist = _bias_by_distance(rel_bias, jnp.arange(2 * TQ)).T * LOG2E
    diag = jnp.concatenate([jnp.full((n_head, TQ - 1), NEG, F32), by_dist[:, :TQ]], axis=1)
    near = by_dist[:, 1:]

    def toeplitz(a):
        period = a.shape[1]
        rows = jnp.tile(a, (1, TQ + 1))[:, :TQ * (period + 1)].reshape(n_head, TQ, period + 1)
        return rows[:, ::-1, :TQ]

    tiles = jnp.stack([toeplitz(near), toeplitz(diag)], axis=1)
    return tiles, by_dist[:, TQ + 1]


def kernel(x_prompt, x_sample, cache_k, cache_v, state_conv_b, state_conv_c, page_table, rel_bias, norm_ffn1, ffn1_w_gate, ffn1_w_up, ffn1_w_down, norm_mix, norm_ffn2, ffn2_w_gate, ffn2_w_up, ffn2_w_down, w_in_even, w_out_even, q_norm_gain, k_norm_gain, lambda_q1, lambda_k1, lambda_q2, lambda_k2, subln_gain, conv_b_w, w_pw1, b_pw1, conv_c_w, conv_c_b, ln_c_gain, ln_c_bias, w_pw2, b_pw2):
    bp, tp, d = x_prompt.shape
    bs = x_sample.shape[0]
    depth = norm_ffn1.shape[0]
    n_even, n_phys, page, n_sub, dh_qk = cache_k.shape
    assert dh_qk == DH_QK and cache_v.shape[-1] == DH_V and x_sample.shape[1] == 1
    d_qk = n_sub * dh_qk
    d_a = cache_v.shape[3] * DH_V
    past_len = page_table.shape[1] * page
    assert tp % TM_ROWS == 0 and tp % TM_ODD == 0 and tp % TQ == 0 and tp >= 2 * TQ

    bf = lambda w: w.astype(BF16)
    xp = x_prompt.reshape(bp * tp, d)
    xs = x_sample.reshape(bs, d)

    cache_kt = jnp.transpose(cache_k, (0, 1, 3, 4, 2)).reshape(n_even, n_phys, d_qk, page)
    cache_vt = jnp.transpose(cache_v, (0, 1, 3, 4, 2)).reshape(n_even, n_phys, d_a, page)
    state_c_t = jnp.transpose(state_conv_c, (0, 2, 1, 3))

    p_avg = jnp.kron(jnp.eye(d_qk // DH_QK, dtype=F32), jnp.full((DH_QK, DH_QK), 1.0 / DH_QK, F32)).astype(BF16)
    bias_tiles, bias_far = _prompt_bias_tiles(rel_bias)
    k_pos = jnp.arange(past_len + 1)
    bias_dec = jnp.repeat(_bias_by_distance(rel_bias, past_len - k_pos), 2, axis=1).T
    bias_dec_past, bias_dec_new = bias_dec[:, :past_len], bias_dec[:, past_len:]

    w_in, w_out, w1, w2 = (w.astype(BF16) for w in (w_in_even, w_out_even, w_pw1, w_pw2))
    ffn1_f32 = (ffn1_w_gate, ffn1_w_up, ffn1_w_down)
    ffn2_f32 = (ffn2_w_gate, ffn2_w_up, ffn2_w_down)
    ffn_w = tuple(w[0].astype(BF16) for w in ffn1_f32)

    kv_prompt = ()
    ks_l, vs_l, cbp_l, cbs_l, ccp_l, ccs_l = [], [], [], [], [], []
    for li in range(depth):
        xs, _ = _ffn(xs, norm_ffn1[li], ffn_w)
        xp, ffn_w = _ffn(xp, norm_ffn1[li], ffn_w, cast=(*ffn2_f32, li))
        if li % 2 == 0:
            e = li // 2
            lam_init = 0.8 - 0.6 * math.exp(-0.3 * li)
            qg = jnp.tile(q_norm_gain[e], d_qk // DH_QK).reshape(1, d_qk)
            kg = jnp.tile(k_norm_gain[e], d_qk // DH_QK).reshape(1, d_qk)
            lam_p = jnp.stack([lambda_q1[e], lambda_k1[e], lambda_q2[e], lambda_k2[e]])
            sg = jnp.tile(subln_gain[e], d_a // DH_V)

            *kv_prompt, qtb, kb, vtb, gcp, cbp = _even_in_prompt(
                xp, norm_mix[li], w_in, e, n_even, tuple(kv_prompt), p_avg, qg, kg, conv_b_w[e], bp, tp)
            qs, kn, vn, gcs, us = _even_in_sample(
                xs, norm_mix[li], w_in, e, p_avg, qg, kg, conv_b_w[e],
                state_conv_b[e, :, 0], state_conv_b[e, :, 1])
            attp, atts = _attention(page_table, bias_far, lam_p, sg, qtb, kb, vtb, bias_tiles,
                                    qs, kn, vn, bias_dec_past, bias_dec_new, cache_kt, cache_vt, e,
                                    bp, tp, lam_init)
            mix_p, mix_s = (attp, gcp, w_out, e), (atts, gcs, w_out, e)

            ks_l.append(kn)
            vs_l.append(vn)
            cbp_l.append(cbp)
            cbs_l.append(jnp.stack([state_conv_b[e, :, 1], us], axis=1))
        else:
            o = li // 2
            prm = (norm_mix[li], w1, w2, o, b_pw1[o], conv_c_w[o], conv_c_b[o], ln_c_gain[o], ln_c_bias[o], b_pw2[o])
            xp, ccp = _odd_prompt(xp, *prm, bp, tp)
            xs, ccs = _odd_sample(xs, *prm, state_c_t[o])
            ccp_l.append(ccp)
            ccs_l.append(ccs)
            mix_p = mix_s = ()
        xs, _ = _ffn(xs, norm_ffn2[li], ffn_w, mixer=mix_s)
        xp, ffn_w = _ffn(xp, norm_ffn2[li], ffn_w, mixer=mix_p,
                         cast=(*ffn1_f32, li + 1) if li + 1 < depth else ())

    n_h = d_a // DH_V
    kt_all, vt_all = kv_prompt
    new_k_prompt = jnp.transpose(kt_all.reshape(n_even, bp, n_sub, DH_QK, tp), (0, 1, 4, 2, 3))
    new_v_prompt = jnp.transpose(vt_all.reshape(n_even, bp, n_h, DH_V, tp), (0, 1, 4, 2, 3))
    return (xp.reshape(bp, tp, d), xs.reshape(bs, 1, d),
            new_k_prompt, new_v_prompt,
            jnp.stack(ks_l).reshape(n_even, bs, 1, n_sub, DH_QK),
            jnp.stack(vs_l).reshape(n_even, bs, 1, n_h, DH_V),
            jnp.stack(cbp_l), jnp.stack(cbs_l), jnp.stack(ccp_l),
            jnp.transpose(jnp.stack(ccs_l), (0, 2, 1, 3)))
```

```python
import functools
import math

import jax
import jax.numpy as jnp
from jax import lax
from jax.experimental import pallas as pl
from jax.experimental.pallas import tpu as pltpu

F32 = jnp.float32
BF16 = jnp.bfloat16
EPS = 1e-6
NEG = -1e30

LANE = 128
SUBLANE = 8
VMEM_LIMIT_BYTES = 56 * 1024 * 1024

DH_QK = 32
DH_V = 64
N_BUCKETS = 32
MAX_EXACT = N_BUCKETS // 2
MAX_DISTANCE = 128
FAR = MAX_DISTANCE

TM_ROWS = 512
TM_FFN = 512
FFN_CHUNKS = 2
CAST_STEPS = 16
TM_ODD = 256
TQ = 256
SOFTMAX_SUM_ROWS = 16
HALO = 32
CONV_UNROLL = 8
LOG2E = math.log2(math.e)


def _params(sem):
    return pltpu.CompilerParams(dimension_semantics=sem, vmem_limit_bytes=VMEM_LIMIT_BYTES)


def _resident(shape):
    n = len(shape)
    return pl.BlockSpec(shape, lambda *_: (0,) * n, pipeline_mode=pl.Buffered(1))


def _resident_layer(stacked, layer):
    n = stacked.ndim - 1
    return pl.BlockSpec((None,) + stacked.shape[1:], lambda *_: (layer,) + (0,) * n,
                        pipeline_mode=pl.Buffered(1))


def _rms(x, g):
    return x * lax.rsqrt(jnp.mean(x * x, axis=-1, keepdims=True) + EPS) * g


def _dot(a, b):
    return jnp.dot(a, b, preferred_element_type=F32)


def _group_rms(z, p_ref, gain):
    sq = z * z
    hi = sq.astype(BF16)
    lo = (sq - hi.astype(F32)).astype(BF16)
    ms = _dot(hi, p_ref[...]) + _dot(lo, p_ref[...])
    return z * lax.rsqrt(ms + EPS) * gain


def _ffn_kernel(x_ref, g_ref, wg_ref, wu_ref, wd_ref, *rest, n_chunks, n_mixer, n_cast):
    mixer, cast_src = rest[:n_mixer], rest[n_mixer:n_mixer + n_cast]
    o_ref, cast_dst = rest[n_mixer + n_cast], rest[n_mixer + n_cast + 1:]
    if n_cast:
        @pl.when(pl.program_id(0) < CAST_STEPS)
        def _():
            for src, dst in zip(cast_src, cast_dst):
                dst[...] = src[...].astype(BF16)
    x = x_ref[...]
    if n_mixer:
        a_ref, gc_ref, wo_ref = mixer
        d_a = a_ref.shape[1]
        x = x + _dot(a_ref[...].astype(BF16), wo_ref[0:d_a, :]) + _dot(gc_ref[...], wo_ref[d_a:, :])
    h = _rms(x, g_ref[...]).astype(BF16)
    fc = wg_ref.shape[1] // n_chunks
    y = jnp.zeros(x.shape, F32)
    for c in range(n_chunks):
        gate = _dot(h, wg_ref[:, c * fc:(c + 1) * fc])
        up = _dot(h, wu_ref[:, c * fc:(c + 1) * fc])
        a = (jax.nn.silu(gate) * up).astype(BF16)
        y = y + _dot(a, wd_ref[c * fc:(c + 1) * fc, :])
    o_ref[...] = x + 0.5 * y


def _ffn(x, g, weights, mixer=(), cast=()):
    m, d = x.shape
    tm = min(TM_FFN, m)
    steps = m // tm
    row = lambda width: pl.BlockSpec((tm, width), lambda i: (i, 0))
    mixer_specs = ([row(mixer[0].shape[1]), row(mixer[1].shape[1]), _resident_layer(mixer[2], mixer[3])]
                   if mixer else [])
    *stacks, cast_layer = cast if cast else (None,)
    assert not stacks or steps >= CAST_STEPS
    slab = lambda i: jnp.minimum(i, CAST_STEPS - 1)
    cast_in = [pl.BlockSpec((None, w.shape[1] // CAST_STEPS, w.shape[2]), lambda i: (cast_layer, slab(i), 0))
               for w in stacks]
    cast_out = [pl.BlockSpec((w.shape[1] // CAST_STEPS, w.shape[2]), lambda i: (slab(i), 0)) for w in stacks]
    out = pl.pallas_call(
        functools.partial(_ffn_kernel, n_chunks=FFN_CHUNKS, n_mixer=len(mixer_specs), n_cast=len(stacks)),
        grid=(steps,),
        in_specs=[row(d), _resident((1, d))] + [_resident(w.shape) for w in weights] + mixer_specs + cast_in,
        out_specs=[row(d)] + cast_out,
        out_shape=[jax.ShapeDtypeStruct((m, d), F32)] + [jax.ShapeDtypeStruct(w.shape[1:], BF16) for w in stacks],
        compiler_params=_params(("arbitrary",)),
        name="ffn_mixer" if mixer else "ffn",
    )(x, g.reshape(1, d), *weights, *mixer[:3], *stacks)
    return out[0], tuple(out[1:])


def _even_in_prompt_kernel(x_ref, g_ref, w_ref, p_ref, qg_ref, kg_ref, cw_ref, *rest,
                           tiles_per_seq, d_qk, d_a, d_b, scale, layer):
    kt_ref, vt_ref, qtb_ref, kb_ref, vtb_ref, gc_ref, st_ref, u_scr = rest[-8:]
    i = pl.program_id(0)
    tm = x_ref.shape[0]
    h = _rms(x_ref[...], g_ref[...]).astype(BF16)
    z = _dot(h, w_ref[...])
    q = _group_rms(z[:, :d_qk], p_ref, qg_ref[...]) * scale
    k = _group_rms(z[:, d_qk:2 * d_qk], p_ref, kg_ref[...])
    o = 2 * d_qk
    v = z[:, o:o + d_a]
    o += d_a
    gate_b = z[:, o:o + d_b]
    u = z[:, o + d_b:o + 2 * d_b] * z[:, o + 2 * d_b:o + 3 * d_b]

    kt = k.T
    vt = v.T
    for ref, val in ((kt_ref, kt), (vt_ref, vt)):
        if len(ref.shape) == 2:
            ref[...] = val
        else:
            for slot in range(ref.shape[0]):
                ref[slot] = val if slot == layer else jnp.zeros_like(val)
    qtb_ref[...] = q.T.astype(BF16)
    kb_ref[...] = k.astype(BF16)
    vtb_ref[...] = vt.astype(BF16)

    @pl.when(i % tiles_per_seq == 0)
    def _():
        u_scr[0:SUBLANE, :] = jnp.zeros((SUBLANE, d_b), F32)

    u_scr[SUBLANE:SUBLANE + tm, :] = u
    conv = (cw_ref[0:1, :] * u_scr[SUBLANE - 2:SUBLANE - 2 + tm, :]
            + cw_ref[1:2, :] * u_scr[SUBLANE - 1:SUBLANE - 1 + tm, :]
            + cw_ref[2:3, :] * u)
    gc_ref[...] = (gate_b * conv).astype(BF16)
    st_ref[...] = u_scr[tm + SUBLANE - 2:tm + SUBLANE, :]
    u_scr[0:SUBLANE, :] = u_scr[tm:tm + SUBLANE, :]


def _even_in_prompt(x, g, w_in, layer, n_layers, kv_prev, p_avg, qg, kg, cw, n_seq, seq):
    m, d = x.shape
    d_qk = d_a = d_b = p_avg.shape[0]
    tm = TM_ROWS
    tps = seq // tm
    row = lambda width, dt=None: pl.BlockSpec((tm, width), lambda i: (i, 0))
    col = pl.BlockSpec((d_qk, tm), lambda i: (0, i))
    if kv_prev:
        seq_t = pl.BlockSpec((None, None, d_qk, tm), lambda i: (layer, i // tps, 0, i % tps))
    else:
        seq_t = pl.BlockSpec((n_layers, None, d_qk, tm), lambda i: (0, i // tps, 0, i % tps))
    n_in = 7
    kern = functools.partial(_even_in_prompt_kernel, tiles_per_seq=tps, d_qk=d_qk, d_a=d_a, d_b=d_b, layer=layer,
                             scale=DH_QK ** -0.5 * LOG2E)
    return pl.pallas_call(
        kern,
        grid=(m // tm,),
        in_specs=[row(d), _resident((1, d)), _resident_layer(w_in, layer), _resident(p_avg.shape),
                  _resident((1, d_qk)), _resident((1, d_qk)), _resident(cw.shape)]
                 + [pl.BlockSpec(memory_space=pl.ANY)] * len(kv_prev),
        out_specs=[seq_t, seq_t, col, row(d_qk), col, row(d_b),
                   pl.BlockSpec((None, 2, d_b), lambda i: (i // tps, 0, 0))],
        out_shape=[jax.ShapeDtypeStruct((n_layers, n_seq, d_qk, seq), F32),
                   jax.ShapeDtypeStruct((n_layers, n_seq, d_a, seq), F32),
                   jax.ShapeDtypeStruct((d_qk, m), BF16),
                   jax.ShapeDtypeStruct((m, d_qk), BF16),
                   jax.ShapeDtypeStruct((d_a, m), BF16),
                   jax.ShapeDtypeStruct((m, d_b), BF16),
                   jax.ShapeDtypeStruct((n_seq, 2, d_b), F32)],
        scratch_shapes=[pltpu.VMEM((tm + SUBLANE, d_b), F32)],
        input_output_aliases={n_in + j: j for j in range(len(kv_prev))},
        compiler_params=_params(("arbitrary",)),
        name="even_in_prompt",
    )(x, g.reshape(1, d), w_in, p_avg, qg, kg, cw, *kv_prev)


def _even_in_sample_kernel(x_ref, g_ref, w_ref, p_ref, qg_ref, kg_ref, cw_ref, s0_ref, s1_ref,
                           q_ref, k_ref, v_ref, gc_ref, u_ref, *, d_qk, d_a, d_b, scale):
    h = _rms(x_ref[...], g_ref[...]).astype(BF16)
    z = _dot(h, w_ref[...])
    q_ref[...] = _group_rms(z[:, :d_qk], p_ref, qg_ref[...]) * scale
    k_ref[...] = _group_rms(z[:, d_qk:2 * d_qk], p_ref, kg_ref[...])
    o = 2 * d_qk
    v_ref[...] = z[:, o:o + d_a]
    o += d_a
    gate_b = z[:, o:o + d_b]
    u = z[:, o + d_b:o + 2 * d_b] * z[:, o + 2 * d_b:o + 3 * d_b]
    conv = cw_ref[0:1, :] * s0_ref[...] + cw_ref[1:2, :] * s1_ref[...] + cw_ref[2:3, :] * u
    gc_ref[...] = (gate_b * conv).astype(BF16)
    u_ref[...] = u


def _even_in_sample(x, g, w_in, layer, p_avg, qg, kg, cw, s0, s1):
    m, d = x.shape
    d_qk = d_a = d_b = p_avg.shape[0]
    kern = functools.partial(_even_in_sample_kernel, d_qk=d_qk, d_a=d_a, d_b=d_b, scale=DH_QK ** -0.5)
    f = lambda dt: jax.ShapeDtypeStruct((m, d_qk), dt)
    blk = _resident((m, d_qk))
    return pl.pallas_call(
        kern,
        grid=(1,),
        in_specs=[_resident((m, d)), _resident((1, d)), _resident_layer(w_in, layer), _resident(p_avg.shape),
                  _resident((1, d_qk)), _resident((1, d_qk)), _resident(cw.shape), blk, blk],
        out_specs=[pl.BlockSpec((m, d_qk), lambda i: (0, 0))] * 5,
        out_shape=[f(F32), f(F32), f(F32), f(BF16), f(F32)],
        compiler_params=_params(("arbitrary",)),
        name="even_in_sample",
    )(x, g.reshape(1, d), w_in, p_avg, qg, kg, cw, s0, s1)


def _lambda(lam_ref, lam_init):
    lam_p = lam_ref[...]
    return (jnp.exp(jnp.sum(lam_p[0:1] * lam_p[1:2], axis=1, keepdims=True))
            - jnp.exp(jnp.sum(lam_p[2:3] * lam_p[3:4], axis=1, keepdims=True)) + lam_init)


def _prompt_tiles(tiles, hp, lam, cfar_ref, sg_ref, qt_ref, k_ref, vt_ref, bias_ref, o_ref, p_scr, *,
                  lam_init, tq):
    n_sub = qt_ref.shape[0] // DH_QK
    n_head = n_sub // 2
    cfar = jnp.concatenate([jnp.full((1, 2 * tq), cfar_ref[hp * n_head + h], F32) for h in range(n_head)],
                           axis=1)
    grp = lax.broadcasted_iota(jnp.int32, (qt_ref.shape[0], tq), 0) // DH_QK

    def scores(n):
        qt = qt_ref[:, n * tq:(n + 1) * tq]
        q_blk = jnp.concatenate([jnp.where(grp == c, qt, jnp.zeros_like(qt)) for c in range(n_sub)], axis=1)
        return _dot(k_ref[0:(n + 1) * tq, :], q_blk)

    s_next = scores(tiles[0])
    for idx, n in enumerate(tiles):
        s_all = s_next
        if idx + 1 < len(tiles):
            s_next = scores(tiles[idx + 1])
        p_buf = p_scr.at[idx % 2]

        m = None
        s_tiles = []
        for t in range(n + 1):
            s = s_all[t * tq:(t + 1) * tq, :]
            if t >= n - 1:
                kind = 1 if t == n else 0
                s = jnp.concatenate([s[:, c * tq:(c + 1) * tq] + bias_ref[c // 2, kind] for c in range(n_sub)],
                                    axis=1)
                smax = jnp.max(s, axis=0, keepdims=True)
            else:
                smax = jnp.max(s, axis=0, keepdims=True) + cfar
            m = smax if m is None else jnp.maximum(m, smax)
            s_tiles.append(s)

        m_far = m - cfar
        for t in range(n + 1):
            shift = m if t >= n - 1 else m_far
            p_buf[t * tq:(t + 1) * tq, :] = jnp.exp2(s_tiles[t] - shift).astype(BF16)

        keys = (n + 1) * tq
        ones = jnp.ones((SOFTMAX_SUM_ROWS, keys), BF16)
        outs = []
        for h in range(n_head):
            vte = jnp.concatenate([vt_ref[h * DH_V:(h + 1) * DH_V, 0:keys], ones], axis=0)
            pv = _dot(vte, p_buf[0:keys, h * 2 * tq:(h + 1) * 2 * tq])
            sub = [pv[0:DH_V, c * tq:(c + 1) * tq] / pv[DH_V:DH_V + 1, c * tq:(c + 1) * tq] for c in range(2)]
            o = sub[0] - lam * sub[1]
            ms = jnp.mean(o * o, axis=0, keepdims=True)
            outs.append(o * lax.rsqrt(ms + EPS))
        o = jnp.concatenate(outs, axis=0) * sg_ref[...] * (1.0 - lam_init)
        o_ref[n * tq:(n + 1) * tq, :] = o.T.astype(BF16)


def _decode_pages(first, last, lam, q_ref, kn_ref, vn_ref, bias, b0_ref, sg_ref, k_refs, v_refs, o_ref,
                  m_scr, l_scr, acc_scr, *, lam_init):
    pp = len(k_refs)
    n_sub, d = acc_scr.shape
    page = k_refs[0].shape[1]

    row = lax.broadcasted_iota(jnp.int32, (n_sub, d), 0)
    col = lax.broadcasted_iota(jnp.int32, (n_sub, d), 1)
    qexp = jnp.where(col // DH_QK == row, jnp.broadcast_to(q_ref[...], (n_sub, d)), 0.0)

    if first:
        m_scr[...] = jnp.sum(qexp * kn_ref[...], axis=1, keepdims=True) + b0_ref[...]
        l_scr[...] = jnp.ones(l_scr.shape, F32)
        acc_scr[...] = jnp.broadcast_to(vn_ref[...], (n_sub, d))

    qb = qexp.astype(BF16)
    s = jnp.concatenate([_dot(qb, k_refs[i][...].astype(BF16)) for i in range(pp)], axis=1)
    s = s + bias
    m_old = m_scr[...]
    m_new = jnp.maximum(m_old, jnp.max(s, axis=1, keepdims=True))
    p = jnp.exp(s - m_new)
    alpha = jnp.exp(m_old - m_new)
    l_scr[...] = alpha * l_scr[...] + jnp.sum(p, axis=1, keepdims=True)
    pb = p.astype(BF16)
    pv = jnp.zeros((n_sub, d), F32)
    for i in range(pp):
        pv = pv + lax.dot_general(pb[:, i * page:(i + 1) * page], v_refs[i][...].astype(BF16),
                                  (((1,), (1,)), ((), ())), preferred_element_type=F32)
    acc_scr[...] = alpha * acc_scr[...] + pv
    m_scr[...] = m_new

    if last:
        o = acc_scr[...] / l_scr[...]
        head = col // DH_V
        a0 = jnp.sum(jnp.where(row == 2 * head, o, 0.0), axis=0, keepdims=True)
        a1 = jnp.sum(jnp.where(row == 2 * head + 1, o, 0.0), axis=0, keepdims=True)
        att = a0 - lam * a1
        sq = att * att
        head1 = head[0:1]
        r = jnp.zeros_like(att)
        for hh in range(d // DH_V):
            ms = jnp.sum(jnp.where(head1 == hh, sq, 0.0), axis=1, keepdims=True) * (1.0 / DH_V)
            r = jnp.where(head1 == hh, lax.rsqrt(ms + EPS), r)
        o_ref[...] = att * r * sg_ref[...] * (1.0 - lam_init)


def _attention_kernel(pt_ref, cfar_ref, lam_ref, sgc_ref, qt_ref, k_ref, vt_ref, bias_ref,
                      q_ref, kn_ref, vn_ref, bt_ref, b0_ref, sgr_ref, *rest, pp, tiles, lam_init, tq, layer):
    ck_hbm, cv_hbm, o_ref, od_ref, p_scr, m_scr, l_scr, acc_scr, kbuf, vbuf, sem = rest
    hp = pl.program_id(1)
    half = pl.program_id(2)
    step = (pl.program_id(0) * pl.num_programs(1) + hp) * pl.num_programs(2) + half
    last_step = pl.num_programs(0) * pl.num_programs(1) * pl.num_programs(2) - 1
    slot = step % 2

    def page_copies(of_step, into):
        copies = []
        for i in range(pp):
            pid = pt_ref[of_step * pp + i]
            copies.append(pltpu.make_async_copy(ck_hbm.at[layer, pid], kbuf.at[into, i], sem.at[into, 0]))
            copies.append(pltpu.make_async_copy(cv_hbm.at[layer, pid], vbuf.at[into, i], sem.at[into, 1]))
        return copies

    @pl.when(step == 0)
    def _():
        for c in page_copies(0, 0):
            c.start()

    lam = _lambda(lam_ref, lam_init)
    nxt = jnp.minimum(step + 1, last_step)
    for hv in range(2):
        @pl.when(half == hv)
        def _(hv=hv):
            for c in page_copies(nxt, 1 - slot):
                c.start()
            _prompt_tiles(tiles[hv], hp, lam, cfar_ref, sgc_ref, qt_ref, k_ref, vt_ref, bias_ref, o_ref, p_scr,
                          lam_init=lam_init, tq=tq)
            for c in page_copies(step, slot):
                c.wait()
            k_refs = [kbuf.at[slot, i] for i in range(pp)]
            v_refs = [vbuf.at[slot, i] for i in range(pp)]
            _decode_pages(hv == 0, hv == 1, lam, q_ref, kn_ref, vn_ref, bt_ref[...], b0_ref, sgr_ref,
                          k_refs, v_refs, od_ref, m_scr, l_scr, acc_scr, lam_init=lam_init)

    @pl.when(step == last_step)
    def _():
        for c in page_copies(last_step, 1 - slot):
            c.wait()


def _attention(page_table, cfar, lam_p, sg, qt, k, vt, bias_tiles, q, kn, vn, bias_t, bias0,
               cache_kt, cache_vt, layer, n_seq, seq, lam_init):
    d_qk, m = qt.shape
    n_dec, n_pages = page_table.shape
    d = q.shape[-1]
    page = cache_kt.shape[-1]
    pair = 2 * 2 * DH_QK
    n_pair = d_qk // pair
    nq = seq // TQ
    order = sorted(range(nq), reverse=True)
    tiles = ([n for i, n in enumerate(order) if i % 4 in (0, 3)], [n for i, n in enumerate(order) if i % 4 in (1, 2)])
    halves = 2
    pp = n_pages // halves
    assert n_pages % halves == 0 and n_seq * n_pair == n_dec, "one sample sequence per (prompt sequence, head pair)"
    kern = functools.partial(_attention_kernel, pp=pp, tiles=tiles, lam_init=lam_init, tq=TQ, layer=layer)
    const = lambda shape: pl.BlockSpec(shape, lambda b, h, s, pt: (0,) * len(shape))
    tok = pl.BlockSpec((None, 1, d), lambda b, h, s, pt: (b * n_pair + h, 0, 0))
    grid_spec = pltpu.PrefetchScalarGridSpec(
        num_scalar_prefetch=1,
        grid=(n_seq, n_pair, halves),
        in_specs=[pl.BlockSpec(memory_space=pltpu.SMEM),
                  const(lam_p.shape),
                  const((pair, 1)),
                  pl.BlockSpec((pair, seq), lambda b, h, s, pt: (h, b)),
                  pl.BlockSpec((seq, pair), lambda b, h, s, pt: (b, h)),
                  pl.BlockSpec((pair, seq), lambda b, h, s, pt: (h, b)),
                  pl.BlockSpec((2, 2, TQ, TQ), lambda b, h, s, pt: (h, 0, 0, 0)),
                  tok, tok, tok,
                  pl.BlockSpec((d // DH_QK, pp * page), lambda b, h, s, pt: (0, s)),
                  const((d // DH_QK, 1)),
                  const((1, d)),
                  pl.BlockSpec(memory_space=pl.ANY), pl.BlockSpec(memory_space=pl.ANY)],
        out_specs=[pl.BlockSpec((seq, pair), lambda b, h, s, pt: (b, h)), tok],
        scratch_shapes=[pltpu.VMEM((2, seq, (pair // DH_QK) * TQ), BF16),
                        pltpu.VMEM((d // DH_QK, 1), F32), pltpu.VMEM((d // DH_QK, 1), F32),
                        pltpu.VMEM((d // DH_QK, d), F32),
                        pltpu.VMEM((2, pp, d, page), F32), pltpu.VMEM((2, pp, d, page), F32),
                        pltpu.SemaphoreType.DMA((2, 2))],
    )
    att_p, att_s = pl.pallas_call(
        kern,
        grid_spec=grid_spec,
        out_shape=[jax.ShapeDtypeStruct((m, d_qk), BF16), jax.ShapeDtypeStruct((n_dec, 1, d), F32)],
        compiler_params=_params(("arbitrary", "arbitrary", "arbitrary")),
        name="attention",
    )(page_table.reshape(-1), cfar, lam_p, sg[:pair].reshape(-1, 1), qt, k, vt, bias_tiles,
      q.reshape(n_dec, 1, d), kn.reshape(n_dec, 1, d), vn.reshape(n_dec, 1, d), bias_t, bias0, sg.reshape(1, -1),
      cache_kt, cache_vt)
    return att_p, att_s.reshape(n_dec, d)


def _layernorm_swish(c, g, b):
    mu = jnp.mean(c, axis=-1, keepdims=True)
    xc = c - mu
    y = xc * lax.rsqrt(jnp.mean(xc * xc, axis=-1, keepdims=True) + EPS) * g + b
    return jax.nn.silu(y)


def _odd_prompt_kernel(x_ref, g_ref, w1_ref, b1_ref, cw_ref, cb_ref, lg_ref, lb_ref, w2_ref, b2_ref,
                       o_ref, st_ref, u_scr, c_scr, *, tiles_per_seq, width):
    i = pl.program_id(0)
    tm, d = x_ref.shape
    nl = d // LANE
    x = x_ref[...]
    h = _rms(x, g_ref[...]).astype(BF16)
    z = _dot(h, w1_ref[...]) + b1_ref[...]
    u = z[:, :d] * jax.nn.sigmoid(z[:, d:])

    @pl.when(i % tiles_per_seq == 0)
    def _():
        u_scr[0:HALO * nl, :] = jnp.zeros((HALO * nl, LANE), F32)

    for l in range(nl):
        u_scr[pl.ds(HALO * nl + l, tm, stride=nl), :] = u[:, l * LANE:(l + 1) * LANE]

    @pl.when(i % tiles_per_seq == tiles_per_seq - 1)
    def _():
        st_ref[...] = u[tm - (width - 1):, :]

    first = HALO - (width - 1)
    taps = [cw_ref[j] for j in range(width)]

    def chunk(r, carry):
        t0 = r * CONV_UNROLL
        slab = lambda t: pl.ds(pl.multiple_of(t * nl, nl), nl)
        acc = [None] * CONV_UNROLL
        for k in range(CONV_UNROLL + width - 1):
            row = u_scr[slab(t0 + first + k), :]
            for o in range(max(0, k - width + 1), min(CONV_UNROLL, k + 1)):
                term = taps[k - o] * row
                acc[o] = term if acc[o] is None else acc[o] + term
        for o in range(CONV_UNROLL):
            c_scr[slab(t0 + o), :] = acc[o]
        return carry

    lax.fori_loop(0, tm // CONV_UNROLL, chunk, 0)

    u_scr[0:HALO * nl, :] = u_scr[tm * nl:(tm + HALO) * nl, :]
    c = jnp.concatenate([c_scr[pl.ds(l, tm, stride=nl), :] for l in range(nl)], axis=1)
    a = _layernorm_swish(c + cb_ref[...], lg_ref[...], lb_ref[...]).astype(BF16)
    o_ref[...] = x + _dot(a, w2_ref[...]) + b2_ref[...]


def _odd_prompt(x, g, w1, w2, layer, b1, cw, cb, lg, lb, b2, n_seq, seq):
    m, d = x.shape
    width = cw.shape[0]
    tm = TM_ODD
    tps = seq // tm
    row = pl.BlockSpec((tm, d), lambda i: (i, 0))
    vec = lambda a: a.reshape(1, -1)
    nl = d // LANE
    assert tm >= width - 1 and tm % CONV_UNROLL == 0 and HALO >= width - 1
    kern = functools.partial(_odd_prompt_kernel, tiles_per_seq=tps, width=width)
    return pl.pallas_call(
        kern,
        grid=(m // tm,),
        in_specs=[row, _resident((1, d)), _resident_layer(w1, layer), _resident((1, 2 * d)),
                  _resident((width, nl, LANE)),
                  _resident((1, d)), _resident((1, d)), _resident((1, d)), _resident_layer(w2, layer),
                  _resident((1, d))],
        out_specs=[row, pl.BlockSpec((None, width - 1, d), lambda i: (i // tps, 0, 0))],
        out_shape=[jax.ShapeDtypeStruct((m, d), F32),
                   jax.ShapeDtypeStruct((n_seq, width - 1, d), F32)],
        scratch_shapes=[pltpu.VMEM(((tm + HALO) * nl, LANE), F32), pltpu.VMEM((tm * nl, LANE), F32)],
        compiler_params=_params(("arbitrary",)),
        name="odd_prompt",
    )(x, vec(g), w1, vec(b1), cw.reshape(width, nl, LANE), vec(cb), vec(lg), vec(lb), w2, vec(b2))


def _odd_sample_kernel(x_ref, g_ref, w1_ref, b1_ref, cw_ref, cb_ref, lg_ref, lb_ref, w2_ref, b2_ref,
                       st_ref, o_ref, nst_ref):
    n_hist = st_ref.shape[0]
    d = x_ref.shape[1]
    x = x_ref[...]
    h = _rms(x, g_ref[...]).astype(BF16)
    z = _dot(h, w1_ref[...]) + b1_ref[...]
    u = z[:, :d] * jax.nn.sigmoid(z[:, d:])
    c = cw_ref[n_hist:n_hist + 1, :] * u
    for j in range(n_hist):
        c = c + cw_ref[j:j + 1, :] * st_ref[j]
    for j in range(n_hist - 1):
        nst_ref[j] = st_ref[j + 1]
    nst_ref[n_hist - 1] = u
    a = _layernorm_swish(c + cb_ref[...], lg_ref[...], lb_ref[...]).astype(BF16)
    o_ref[...] = x + _dot(a, w2_ref[...]) + b2_ref[...]


def _odd_sample(x, g, w1, w2, layer, b1, cw, cb, lg, lb, b2, state_t):
    m, d = x.shape
    vec = lambda a: a.reshape(1, -1)
    whole = lambda shape: pl.BlockSpec(shape, lambda i: (0,) * len(shape))
    return pl.pallas_call(
        _odd_sample_kernel,
        grid=(1,),
        in_specs=[_resident((m, d)), _resident((1, d)), _resident_layer(w1, layer), _resident((1, 2 * d)),
                  _resident(cw.shape), _resident((1, d)), _resident((1, d)), _resident((1, d)),
                  _resident_layer(w2, layer), _resident((1, d)), _resident(state_t.shape)],
        out_specs=[whole((m, d)), whole(state_t.shape)],
        out_shape=[jax.ShapeDtypeStruct((m, d), F32), jax.ShapeDtypeStruct(state_t.shape, F32)],
        compiler_params=_params(("arbitrary",)),
        name="odd_sample",
    )(x, vec(g), w1, vec(b1), cw, vec(cb), vec(lg), vec(lb), w2, vec(b2), state_t)


def _bias_by_distance(rel_bias, dist):
    n = jnp.maximum(dist, 0)
    nf = jnp.maximum(n, 1).astype(F32)
    large = MAX_EXACT + (jnp.log(nf / MAX_EXACT) / math.log(MAX_DISTANCE / MAX_EXACT)
                         * (N_BUCKETS - MAX_EXACT)).astype(jnp.int32)
    large = jnp.minimum(large, N_BUCKETS - 1)
    return rel_bias[jnp.where(n < MAX_EXACT, n, large)]


def _prompt_bias_tiles(rel_bias):
    assert TQ >= FAR - 1
    n_head = rel_bias.shape[1]
    by_dist = _bias_by_distance(rel_bias, jnp.arange(2 * TQ)).T * LOG2E
    diag = jnp.concatenate([jnp.full((n_head, TQ - 1), NEG, F32), by_dist[:, :TQ]], axis=1)
    near = by_dist[:, 1:]

    def toeplitz(a):
        period = a.shape[1]
        rows = jnp.tile(a, (1, TQ + 1))[:, :TQ * (period + 1)].reshape(n_head, TQ, period + 1)
        return rows[:, ::-1, :TQ]

    tiles = jnp.stack([toeplitz(near), toeplitz(diag)], axis=1)
    return tiles, by_dist[:, TQ + 1]


def kernel(x_prompt, x_sample, cache_k, cache_v, state_conv_b, state_conv_c, page_table, rel_bias, norm_ffn1, ffn1_w_gate, ffn1_w_up, ffn1_w_down, norm_mix, norm_ffn2, ffn2_w_gate, ffn2_w_up, ffn2_w_down, w_in_even, w_out_even, q_norm_gain, k_norm_gain, lambda_q1, lambda_k1, lambda_q2, lambda_k2, subln_gain, conv_b_w, w_pw1, b_pw1, conv_c_w, conv_c_b, ln_c_gain, ln_c_bias, w_pw2, b_pw2):
    bp, tp, d = x_prompt.shape
    bs = x_sample.shape[0]
    depth = norm_ffn1.shape[0]
    n_even, n_phys, page, n_sub, dh_qk = cache_k.shape
    assert dh_qk == DH_QK and cache_v.shape[-1] == DH_V and x_sample.shape[1] == 1
    d_qk = n_sub * dh_qk
    d_a = cache_v.shape[3] * DH_V
    past_len = page_table.shape[1] * page
    assert tp % TM_ROWS == 0 and tp % TM_ODD == 0 and tp % TQ == 0 and tp >= 2 * TQ

    bf = lambda w: w.astype(BF16)
    xp = x_prompt.reshape(bp * tp, d)
    xs = x_sample.reshape(bs, d)

    cache_kt = jnp.transpose(cache_k, (0, 1, 3, 4, 2)).reshape(n_even, n_phys, d_qk, page)
    cache_vt = jnp.transpose(cache_v, (0, 1, 3, 4, 2)).reshape(n_even, n_phys, d_a, page)
    state_c_t = jnp.transpose(state_conv_c, (0, 2, 1, 3))

    p_avg = jnp.kron(jnp.eye(d_qk // DH_QK, dtype=F32), jnp.full((DH_QK, DH_QK), 1.0 / DH_QK, F32)).astype(BF16)
    bias_tiles, bias_far = _prompt_bias_tiles(rel_bias)
    k_pos = jnp.arange(past_len + 1)
    bias_dec = jnp.repeat(_bias_by_distance(rel_bias, past_len - k_pos), 2, axis=1).T
    bias_dec_past, bias_dec_new = bias_dec[:, :past_len], bias_dec[:, past_len:]

    w_in, w_out, w1, w2 = (w.astype(BF16) for w in (w_in_even, w_out_even, w_pw1, w_pw2))
    ffn1_f32 = (ffn1_w_gate, ffn1_w_up, ffn1_w_down)
    ffn2_f32 = (ffn2_w_gate, ffn2_w_up, ffn2_w_down)
    ffn_w = tuple(w[0].astype(BF16) for w in ffn1_f32)

    kv_prompt = ()
    ks_l, vs_l, cbp_l, cbs_l, ccp_l, ccs_l = [], [], [], [], [], []
    for li in range(depth):
        xs, _ = _ffn(xs, norm_ffn1[li], ffn_w)
        xp, ffn_w = _ffn(xp, norm_ffn1[li], ffn_w, cast=(*ffn2_f32, li))
        if li % 2 == 0:
            e = li // 2
            lam_init = 0.8 - 0.6 * math.exp(-0.3 * li)
            qg = jnp.tile(q_norm_gain[e], d_qk // DH_QK).reshape(1, d_qk)
            kg = jnp.tile(k_norm_gain[e], d_qk // DH_QK).reshape(1, d_qk)
            lam_p = jnp.stack([lambda_q1[e], lambda_k1[e], lambda_q2[e], lambda_k2[e]])
            sg = jnp.tile(subln_gain[e], d_a // DH_V)

            *kv_prompt, qtb, kb, vtb, gcp, cbp = _even_in_prompt(
                xp, norm_mix[li], w_in, e, n_even, tuple(kv_prompt), p_avg, qg, kg, conv_b_w[e], bp, tp)
            qs, kn, vn, gcs, us = _even_in_sample(
                xs, norm_mix[li], w_in, e, p_avg, qg, kg, conv_b_w[e],
                state_conv_b[e, :, 0], state_conv_b[e, :, 1])
            attp, atts = _attention(page_table, bias_far, lam_p, sg, qtb, kb, vtb, bias_tiles,
                                    qs, kn, vn, bias_dec_past, bias_dec_new, cache_kt, cache_vt, e,
                                    bp, tp, lam_init)
            mix_p, mix_s = (attp, gcp, w_out, e), (atts, gcs, w_out, e)

            ks_l.append(kn)
            vs_l.append(vn)
            cbp_l.append(cbp)
            cbs_l.append(jnp.stack([state_conv_b[e, :, 1], us], axis=1))
        else:
            o = li // 2
            prm = (norm_mix[li], w1, w2, o, b_pw1[o], conv_c_w[o], conv_c_b[o], ln_c_gain[o], ln_c_bias[o], b_pw2[o])
            xp, ccp = _odd_prompt(xp, *prm, bp, tp)
            xs, ccs = _odd_sample(xs, *prm, state_c_t[o])
            ccp_l.append(ccp)
            ccs_l.append(ccs)
            mix_p = mix_s = ()
        xs, _ = _ffn(xs, norm_ffn2[li], ffn_w, mixer=mix_s)
        xp, ffn_w = _ffn(xp, norm_ffn2[li], ffn_w, mixer=mix_p,
                         cast=(*ffn1_f32, li + 1) if li + 1 < depth else ())

    n_h = d_a // DH_V
    kt_all, vt_all = kv_prompt
    new_k_prompt = jnp.transpose(kt_all.reshape(n_even, bp, n_sub, DH_QK, tp), (0, 1, 4, 2, 3))
    new_v_prompt = jnp.transpose(vt_all.reshape(n_even, bp, n_h, DH_V, tp), (0, 1, 4, 2, 3))
    return (xp.reshape(bp, tp, d), xs.reshape(bs, 1, d),
            new_k_prompt, new_v_prompt,
            jnp.stack(ks_l).reshape(n_even, bs, 1, n_sub, DH_QK),
            jnp.stack(vs_l).reshape(n_even, bs, 1, n_h, DH_V),
            jnp.stack(cbp_l), jnp.stack(cbs_l), jnp.stack(ccp_l),
            jnp.transpose(jnp.stack(ccs_l), (0, 2, 1, 3)))
```

```python
import functools
import math

import jax
import jax.numpy as jnp
from jax import lax
from jax.experimental import pallas as pl
from jax.experimental.pallas import tpu as pltpu

F32 = jnp.float32
BF16 = jnp.bfloat16
EPS = 1e-6
NEG = -1e30

LANE = 128
SUBLANE = 8
VMEM_LIMIT_BYTES = 56 * 1024 * 1024

DH_QK = 32
DH_V = 64
N_BUCKETS = 32
MAX_EXACT = N_BUCKETS // 2
MAX_DISTANCE = 128
FAR = MAX_DISTANCE

TM_ROWS = 512
TM_FFN = 512
FFN_CHUNKS = 2
CAST_STEPS = 16
TM_ODD = 256
TQ = 256
SOFTMAX_SUM_ROWS = 16
HALO = 32
CONV_UNROLL = 8
LOG2E = math.log2(math.e)


def _params(sem):
    return pltpu.CompilerParams(dimension_semantics=sem, vmem_limit_bytes=VMEM_LIMIT_BYTES)


def _resident(shape):
    n = len(shape)
    return pl.BlockSpec(shape, lambda *_: (0,) * n, pipeline_mode=pl.Buffered(1))


def _resident_layer(stacked, layer):
    n = stacked.ndim - 1
    return pl.BlockSpec((None,) + stacked.shape[1:], lambda *_: (layer,) + (0,) * n,
                        pipeline_mode=pl.Buffered(1))


def _rms(x, g):
    return x * lax.rsqrt(jnp.mean(x * x, axis=-1, keepdims=True) + EPS) * g


def _dot(a, b):
    return jnp.dot(a, b, preferred_element_type=F32)


def _group_rms(z, p_ref, gain):
    sq = z * z
    hi = sq.astype(BF16)
    lo = (sq - hi.astype(F32)).astype(BF16)
    ms = _dot(hi, p_ref[...]) + _dot(lo, p_ref[...])
    return z * lax.rsqrt(ms + EPS) * gain


def _swiglu_rows(x, mix, g_ref, wg_ref, wu_ref, wd_ref, wo_ref, n_chunks):
    if mix:
        att, gc = mix
        d_a = att.shape[1]
        x = x + _dot(att.astype(BF16), wo_ref[0:d_a, :]) + _dot(gc, wo_ref[d_a:, :])
    h = _rms(x, g_ref[...]).astype(BF16)
    fc = wg_ref.shape[1] // n_chunks
    y = jnp.zeros(x.shape, F32)
    for c in range(n_chunks):
        gate = _dot(h, wg_ref[:, c * fc:(c + 1) * fc])
        up = _dot(h, wu_ref[:, c * fc:(c + 1) * fc])
        a = (jax.nn.silu(gate) * up).astype(BF16)
        y = y + _dot(a, wd_ref[c * fc:(c + 1) * fc, :])
    return x + 0.5 * y


def _ffn_kernel(x_ref, g_ref, wg_ref, wu_ref, wd_ref, *rest, n_chunks, mixer, n_cast):
    rest = list(rest)
    take = lambda n: [rest.pop(0) for _ in range(n)]
    a_ref, gc_ref, wo_ref = take(3) if mixer else (None, None, None)
    xs_ref, = take(1)
    as_ref, gcs_ref = take(2) if mixer else (None, None)
    cast_src = take(n_cast)
    o_ref, os_ref = take(2)
    cast_dst = take(n_cast)
    step = pl.program_id(0)
    if n_cast:
        @pl.when(step < CAST_STEPS)
        def _():
            for src, dst in zip(cast_src, cast_dst):
                dst[...] = src[...].astype(BF16)

    @pl.when(step == 0)
    def _():
        mix_s = (as_ref[...], gcs_ref[...]) if mixer else ()
        os_ref[...] = _swiglu_rows(xs_ref[...], mix_s, g_ref, wg_ref, wu_ref, wd_ref, wo_ref, n_chunks)

    mix = (a_ref[...], gc_ref[...]) if mixer else ()
    o_ref[...] = _swiglu_rows(x_ref[...], mix, g_ref, wg_ref, wu_ref, wd_ref, wo_ref, n_chunks)


def _ffn(x, xs, g, weights, mixer=(), cast=()):
    m, d = x.shape
    tm = TM_FFN
    steps = m // tm
    row = lambda width: pl.BlockSpec((tm, width), lambda i: (i, 0))
    if mixer:
        att, gc, att_s, gc_s, w_out, out_layer = mixer
        mixer_ops = [att, gc, w_out]
        mixer_specs = [row(att.shape[1]), row(gc.shape[1]), _resident_layer(w_out, out_layer)]
        side_ops = [xs, att_s, gc_s]
    else:
        mixer_ops, mixer_specs, side_ops = [], [], [xs]
    side_specs = [_resident(a.shape) for a in side_ops]
    *stacks, cast_layer = cast if cast else (None,)
    assert not stacks or steps >= CAST_STEPS
    slab = lambda i: jnp.minimum(i, CAST_STEPS - 1)
    cast_in = [pl.BlockSpec((None, w.shape[1] // CAST_STEPS, w.shape[2]), lambda i: (cast_layer, slab(i), 0))
               for w in stacks]
    cast_out = [pl.BlockSpec((w.shape[1] // CAST_STEPS, w.shape[2]), lambda i: (slab(i), 0)) for w in stacks]
    out = pl.pallas_call(
        functools.partial(_ffn_kernel, n_chunks=FFN_CHUNKS, mixer=bool(mixer), n_cast=len(stacks)),
        grid=(steps,),
        in_specs=[row(d), _resident((1, d))] + [_resident(w.shape) for w in weights]
                 + mixer_specs + side_specs + cast_in,
        out_specs=[row(d), pl.BlockSpec(xs.shape, lambda i: (0, 0))] + cast_out,
        out_shape=[jax.ShapeDtypeStruct((m, d), F32), jax.ShapeDtypeStruct(xs.shape, F32)]
                  + [jax.ShapeDtypeStruct(w.shape[1:], BF16) for w in stacks],
        compiler_params=_params(("arbitrary",)),
        name="ffn_mixer" if mixer else "ffn",
    )(x, g.reshape(1, d), *weights, *mixer_ops, *side_ops, *stacks)
    return out[0], out[1], tuple(out[2:])


def _even_in_prompt_kernel(x_ref, g_ref, w_ref, p_ref, qg_ref, kg_ref, cw_ref, *rest,
                           tiles_per_seq, d_qk, d_a, d_b, scale, layer):
    kt_ref, vt_ref, qtb_ref, kb_ref, vtb_ref, gc_ref, st_ref, u_scr = rest[-8:]
    i = pl.program_id(0)
    tm = x_ref.shape[0]
    h = _rms(x_ref[...], g_ref[...]).astype(BF16)
    z = _dot(h, w_ref[...])
    q = _group_rms(z[:, :d_qk], p_ref, qg_ref[...]) * scale
    k = _group_rms(z[:, d_qk:2 * d_qk], p_ref, kg_ref[...])
    o = 2 * d_qk
    v = z[:, o:o + d_a]
    o += d_a
    gate_b = z[:, o:o + d_b]
    u = z[:, o + d_b:o + 2 * d_b] * z[:, o + 2 * d_b:o + 3 * d_b]

    kt = k.T
    vt = v.T
    for ref, val in ((kt_ref, kt), (vt_ref, vt)):
        if len(ref.shape) == 2:
            ref[...] = val
        else:
            for slot in range(ref.shape[0]):
                ref[slot] = val if slot == layer else jnp.zeros_like(val)
    qtb_ref[...] = q.T.astype(BF16)
    kb_ref[...] = k.astype(BF16)
    vtb_ref[...] = vt.astype(BF16)

    @pl.when(i % tiles_per_seq == 0)
    def _():
        u_scr[0:SUBLANE, :] = jnp.zeros((SUBLANE, d_b), F32)

    u_scr[SUBLANE:SUBLANE + tm, :] = u
    conv = (cw_ref[0:1, :] * u_scr[SUBLANE - 2:SUBLANE - 2 + tm, :]
            + cw_ref[1:2, :] * u_scr[SUBLANE - 1:SUBLANE - 1 + tm, :]
            + cw_ref[2:3, :] * u)
    gc_ref[...] = (gate_b * conv).astype(BF16)
    st_ref[...] = u_scr[tm + SUBLANE - 2:tm + SUBLANE, :]
    u_scr[0:SUBLANE, :] = u_scr[tm:tm + SUBLANE, :]


def _even_in_prompt(x, g, w_in, layer, n_layers, kv_prev, p_avg, qg, kg, cw, n_seq, seq):
    m, d = x.shape
    d_qk = d_a = d_b = p_avg.shape[0]
    tm = TM_ROWS
    tps = seq // tm
    row = lambda width, dt=None: pl.BlockSpec((tm, width), lambda i: (i, 0))
    col = pl.BlockSpec((d_qk, tm), lambda i: (0, i))
    if kv_prev:
        seq_t = pl.BlockSpec((None, None, d_qk, tm), lambda i: (layer, i // tps, 0, i % tps))
    else:
        seq_t = pl.BlockSpec((n_layers, None, d_qk, tm), lambda i: (0, i // tps, 0, i % tps))
    n_in = 7
    kern = functools.partial(_even_in_prompt_kernel, tiles_per_seq=tps, d_qk=d_qk, d_a=d_a, d_b=d_b, layer=layer,
                             scale=DH_QK ** -0.5 * LOG2E)
    return pl.pallas_call(
        kern,
        grid=(m // tm,),
        in_specs=[row(d), _resident((1, d)), _resident_layer(w_in, layer), _resident(p_avg.shape),
                  _resident((1, d_qk)), _resident((1, d_qk)), _resident(cw.shape)]
                 + [pl.BlockSpec(memory_space=pl.ANY)] * len(kv_prev),
        out_specs=[seq_t, seq_t, col, row(d_qk), col, row(d_b),
                   pl.BlockSpec((None, 2, d_b), lambda i: (i // tps, 0, 0))],
        out_shape=[jax.ShapeDtypeStruct((n_layers, n_seq, d_qk, seq), F32),
                   jax.ShapeDtypeStruct((n_layers, n_seq, d_a, seq), F32),
                   jax.ShapeDtypeStruct((d_qk, m), BF16),
                   jax.ShapeDtypeStruct((m, d_qk), BF16),
                   jax.ShapeDtypeStruct((d_a, m), BF16),
                   jax.ShapeDtypeStruct((m, d_b), BF16),
                   jax.ShapeDtypeStruct((n_seq, 2, d_b), F32)],
        scratch_shapes=[pltpu.VMEM((tm + SUBLANE, d_b), F32)],
        input_output_aliases={n_in + j: j for j in range(len(kv_prev))},
        compiler_params=_params(("arbitrary",)),
        name="even_in_prompt",
    )(x, g.reshape(1, d), w_in, p_avg, qg, kg, cw, *kv_prev)


def _even_in_sample_kernel(x_ref, g_ref, w_ref, p_ref, qg_ref, kg_ref, cw_ref, s0_ref, s1_ref,
                           q_ref, k_ref, v_ref, gc_ref, u_ref, *, d_qk, d_a, d_b, scale):
    h = _rms(x_ref[...], g_ref[...]).astype(BF16)
    z = _dot(h, w_ref[...])
    q_ref[...] = _group_rms(z[:, :d_qk], p_ref, qg_ref[...]) * scale
    k_ref[...] = _group_rms(z[:, d_qk:2 * d_qk], p_ref, kg_ref[...])
    o = 2 * d_qk
    v_ref[...] = z[:, o:o + d_a]
    o += d_a
    gate_b = z[:, o:o + d_b]
    u = z[:, o + d_b:o + 2 * d_b] * z[:, o + 2 * d_b:o + 3 * d_b]
    conv = cw_ref[0:1, :] * s0_ref[...] + cw_ref[1:2, :] * s1_ref[...] + cw_ref[2:3, :] * u
    gc_ref[...] = (gate_b * conv).astype(BF16)
    u_ref[...] = u


def _even_in_sample(x, g, w_in, layer, p_avg, qg, kg, cw, s0, s1):
    m, d = x.shape
    d_qk = d_a = d_b = p_avg.shape[0]
    kern = functools.partial(_even_in_sample_kernel, d_qk=d_qk, d_a=d_a, d_b=d_b, scale=DH_QK ** -0.5)
    f = lambda dt: jax.ShapeDtypeStruct((m, d_qk), dt)
    blk = _resident((m, d_qk))
    return pl.pallas_call(
        kern,
        grid=(1,),
        in_specs=[_resident((m, d)), _resident((1, d)), _resident_layer(w_in, layer), _resident(p_avg.shape),
                  _resident((1, d_qk)), _resident((1, d_qk)), _resident(cw.shape), blk, blk],
        out_specs=[pl.BlockSpec((m, d_qk), lambda i: (0, 0))] * 5,
        out_shape=[f(F32), f(F32), f(F32), f(BF16), f(F32)],
        compiler_params=_params(("arbitrary",)),
        name="even_in_sample",
    )(x, g.reshape(1, d), w_in, p_avg, qg, kg, cw, s0, s1)


def _lambda(lam_ref, lam_init):
    lam_p = lam_ref[...]
    return (jnp.exp(jnp.sum(lam_p[0:1] * lam_p[1:2], axis=1, keepdims=True))
            - jnp.exp(jnp.sum(lam_p[2:3] * lam_p[3:4], axis=1, keepdims=True)) + lam_init)


def _prompt_tiles(tiles, hp, lam, cfar_ref, sg_ref, qt_ref, k_ref, vt_ref, bias_ref, o_ref, p_scr, *,
                  lam_init, tq):
    n_sub = qt_ref.shape[0] // DH_QK
    n_head = n_sub // 2
    cfar = jnp.concatenate([jnp.full((1, 2 * tq), cfar_ref[hp * n_head + h], F32) for h in range(n_head)],
                           axis=1)
    grp = lax.broadcasted_iota(jnp.int32, (qt_ref.shape[0], tq), 0) // DH_QK

    def scores(n):
        qt = qt_ref[:, n * tq:(n + 1) * tq]
        q_blk = jnp.concatenate([jnp.where(grp == c, qt, jnp.zeros_like(qt)) for c in range(n_sub)], axis=1)
        return _dot(k_ref[0:(n + 1) * tq, :], q_blk)

    s_next = scores(tiles[0])
    for idx, n in enumerate(tiles):
        s_all = s_next
        if idx + 1 < len(tiles):
            s_next = scores(tiles[idx + 1])
        p_buf = p_scr.at[idx % 2]

        m = None
        s_tiles = []
        for t in range(n + 1):
            s = s_all[t * tq:(t + 1) * tq, :]
            if t >= n - 1:
                kind = 1 if t == n else 0
                s = jnp.concatenate([s[:, c * tq:(c + 1) * tq] + bias_ref[c // 2, kind] for c in range(n_sub)],
                                    axis=1)
                smax = jnp.max(s, axis=0, keepdims=True)
            else:
                smax = jnp.max(s, axis=0, keepdims=True) + cfar
            m = smax if m is None else jnp.maximum(m, smax)
            s_tiles.append(s)

        m_far = m - cfar
        for t in range(n + 1):
            shift = m if t >= n - 1 else m_far
            p_buf[t * tq:(t + 1) * tq, :] = jnp.exp2(s_tiles[t] - shift).astype(BF16)

        keys = (n + 1) * tq
        ones = jnp.ones((SOFTMAX_SUM_ROWS, keys), BF16)
        outs = []
        for h in range(n_head):
            vte = jnp.concatenate([vt_ref[h * DH_V:(h + 1) * DH_V, 0:keys], ones], axis=0)
            pv = _dot(vte, p_buf[0:keys, h * 2 * tq:(h + 1) * 2 * tq])
            sub = [pv[0:DH_V, c * tq:(c + 1) * tq] / pv[DH_V:DH_V + 1, c * tq:(c + 1) * tq] for c in range(2)]
            o = sub[0] - lam * sub[1]
            ms = jnp.mean(o * o, axis=0, keepdims=True)
            outs.append(o * lax.rsqrt(ms + EPS))
        o = jnp.concatenate(outs, axis=0) * sg_ref[...] * (1.0 - lam_init)
        o_ref[n * tq:(n + 1) * tq, :] = o.T.astype(BF16)


def _decode_pages(first, last, lam, q_ref, kn_ref, vn_ref, bias, b0_ref, sg_ref, k_refs, v_refs, o_ref,
                  m_scr, l_scr, acc_scr, *, lam_init):
    pp = len(k_refs)
    n_sub, d = acc_scr.shape
    page = k_refs[0].shape[1]

    row = lax.broadcasted_iota(jnp.int32, (n_sub, d), 0)
    col = lax.broadcasted_iota(jnp.int32, (n_sub, d), 1)
    qexp = jnp.where(col // DH_QK == row, jnp.broadcast_to(q_ref[...], (n_sub, d)), 0.0)

    if first:
        m_scr[...] = jnp.sum(qexp * kn_ref[...], axis=1, keepdims=True) + b0_ref[...]
        l_scr[...] = jnp.ones(l_scr.shape, F32)
        acc_scr[...] = jnp.broadcast_to(vn_ref[...], (n_sub, d))

    qb = qexp.astype(BF16)
    s = jnp.concatenate([_dot(qb, k_refs[i][...].astype(BF16)) for i in range(pp)], axis=1)
    s = s + bias
    m_old = m_scr[...]
    m_new = jnp.maximum(m_old, jnp.max(s, axis=1, keepdims=True))
    p = jnp.exp(s - m_new)
    alpha = jnp.exp(m_old - m_new)
    l_scr[...] = alpha * l_scr[...] + jnp.sum(p, axis=1, keepdims=True)
    pb = p.astype(BF16)
    pv = jnp.zeros((n_sub, d), F32)
    for i in range(pp):
        pv = pv + lax.dot_general(pb[:, i * page:(i + 1) * page], v_refs[i][...].astype(BF16),
                                  (((1,), (1,)), ((), ())), preferred_element_type=F32)
    acc_scr[...] = alpha * acc_scr[...] + pv
    m_scr[...] = m_new

    if last:
        o = acc_scr[...] / l_scr[...]
        head = col // DH_V
        a0 = jnp.sum(jnp.where(row == 2 * head, o, 0.0), axis=0, keepdims=True)
        a1 = jnp.sum(jnp.where(row == 2 * head + 1, o, 0.0), axis=0, keepdims=True)
        att = a0 - lam * a1
        sq = att * att
        head1 = head[0:1]
        r = jnp.zeros_like(att)
        for hh in range(d // DH_V):
            ms = jnp.sum(jnp.where(head1 == hh, sq, 0.0), axis=1, keepdims=True) * (1.0 / DH_V)
            r = jnp.where(head1 == hh, lax.rsqrt(ms + EPS), r)
        o_ref[...] = att * r * sg_ref[...] * (1.0 - lam_init)


def _attention_kernel(pt_ref, cfar_ref, lam_ref, sgc_ref, qt_ref, k_ref, vt_ref, bias_ref,
                      q_ref, kn_ref, vn_ref, bt_ref, b0_ref, sgr_ref, *rest, pp, tiles, lam_init, tq, layer):
    ck_hbm, cv_hbm, o_ref, od_ref, p_scr, m_scr, l_scr, acc_scr, kbuf, vbuf, sem = rest
    hp = pl.program_id(1)
    half = pl.program_id(2)
    step = (pl.program_id(0) * pl.num_programs(1) + hp) * pl.num_programs(2) + half
    last_step = pl.num_programs(0) * pl.num_programs(1) * pl.num_programs(2) - 1
    slot = step % 2

    def page_copies(of_step, into):
        copies = []
        for i in range(pp):
            pid = pt_ref[of_step * pp + i]
            copies.append(pltpu.make_async_copy(ck_hbm.at[layer, pid], kbuf.at[into, i], sem.at[into, 0]))
            copies.append(pltpu.make_async_copy(cv_hbm.at[layer, pid], vbuf.at[into, i], sem.at[into, 1]))
        return copies

    @pl.when(step == 0)
    def _():
        for c in page_copies(0, 0):
            c.start()

    lam = _lambda(lam_ref, lam_init)
    nxt = jnp.minimum(step + 1, last_step)
    for hv in range(2):
        @pl.when(half == hv)
        def _(hv=hv):
            for c in page_copies(nxt, 1 - slot):
                c.start()
            _prompt_tiles(tiles[hv], hp, lam, cfar_ref, sgc_ref, qt_ref, k_ref, vt_ref, bias_ref, o_ref, p_scr,
                          lam_init=lam_init, tq=tq)
            for c in page_copies(step, slot):
                c.wait()
            k_refs = [kbuf.at[slot, i] for i in range(pp)]
            v_refs = [vbuf.at[slot, i] for i in range(pp)]
            _decode_pages(hv == 0, hv == 1, lam, q_ref, kn_ref, vn_ref, bt_ref[...], b0_ref, sgr_ref,
                          k_refs, v_refs, od_ref, m_scr, l_scr, acc_scr, lam_init=lam_init)

    @pl.when(step == last_step)
    def _():
        for c in page_copies(last_step, 1 - slot):
            c.wait()


def _attention(page_table, cfar, lam_p, sg, qt, k, vt, bias_tiles, q, kn, vn, bias_t, bias0,
               cache_kt, cache_vt, layer, n_seq, seq, lam_init):
    d_qk, m = qt.shape
    n_dec, n_pages = page_table.shape
    d = q.shape[-1]
    page = cache_kt.shape[-1]
    pair = 2 * 2 * DH_QK
    n_pair = d_qk // pair
    nq = seq // TQ
    order = sorted(range(nq), reverse=True)
    tiles = ([n for i, n in enumerate(order) if i % 4 in (0, 3)], [n for i, n in enumerate(order) if i % 4 in (1, 2)])
    halves = 2
    pp = n_pages // halves
    assert n_pages % halves == 0 and n_seq * n_pair == n_dec, "one sample sequence per (prompt sequence, head pair)"
    kern = functools.partial(_attention_kernel, pp=pp, tiles=tiles, lam_init=lam_init, tq=TQ, layer=layer)
    const = lambda shape: pl.BlockSpec(shape, lambda b, h, s, pt: (0,) * len(shape))
    tok = pl.BlockSpec((None, 1, d), lambda b, h, s, pt: (b * n_pair + h, 0, 0))
    grid_spec = pltpu.PrefetchScalarGridSpec(
        num_scalar_prefetch=1,
        grid=(n_seq, n_pair, halves),
        in_specs=[pl.BlockSpec(memory_space=pltpu.SMEM),
                  const(lam_p.shape),
                  const((pair, 1)),
                  pl.BlockSpec((pair, seq), lambda b, h, s, pt: (h, b)),
                  pl.BlockSpec((seq, pair), lambda b, h, s, pt: (b, h)),
                  pl.BlockSpec((pair, seq), lambda b, h, s, pt: (h, b)),
                  pl.BlockSpec((2, 2, TQ, TQ), lambda b, h, s, pt: (h, 0, 0, 0)),
                  tok, tok, tok,
                  pl.BlockSpec((d // DH_QK, pp * page), lambda b, h, s, pt: (0, s)),
                  const((d // DH_QK, 1)),
                  const((1, d)),
                  pl.BlockSpec(memory_space=pl.ANY), pl.BlockSpec(memory_space=pl.ANY)],
        out_specs=[pl.BlockSpec((seq, pair), lambda b, h, s, pt: (b, h)), tok],
        scratch_shapes=[pltpu.VMEM((2, seq, (pair // DH_QK) * TQ), BF16),
                        pltpu.VMEM((d // DH_QK, 1), F32), pltpu.VMEM((d // DH_QK, 1), F32),
                        pltpu.VMEM((d // DH_QK, d), F32),
                        pltpu.VMEM((2, pp, d, page), F32), pltpu.VMEM((2, pp, d, page), F32),
                        pltpu.SemaphoreType.DMA((2, 2))],
    )
    att_p, att_s = pl.pallas_call(
        kern,
        grid_spec=grid_spec,
        out_shape=[jax.ShapeDtypeStruct((m, d_qk), BF16), jax.ShapeDtypeStruct((n_dec, 1, d), F32)],
        compiler_params=_params(("arbitrary", "arbitrary", "arbitrary")),
        name="attention",
    )(page_table.reshape(-1), cfar, lam_p, sg[:pair].reshape(-1, 1), qt, k, vt, bias_tiles,
      q.reshape(n_dec, 1, d), kn.reshape(n_dec, 1, d), vn.reshape(n_dec, 1, d), bias_t, bias0, sg.reshape(1, -1),
      cache_kt, cache_vt)
    return att_p, att_s.reshape(n_dec, d)


def _layernorm_swish(c, g, b):
    mu = jnp.mean(c, axis=-1, keepdims=True)
    xc = c - mu
    y = xc * lax.rsqrt(jnp.mean(xc * xc, axis=-1, keepdims=True) + EPS) * g + b
    return jax.nn.silu(y)


def _odd_prompt_kernel(x_ref, g_ref, w1_ref, b1_ref, cw_ref, cb_ref, lg_ref, lb_ref, w2_ref, b2_ref,
                       o_ref, st_ref, u_scr, c_scr, *, tiles_per_seq, width):
    i = pl.program_id(0)
    tm, d = x_ref.shape
    nl = d // LANE
    x = x_ref[...]
    h = _rms(x, g_ref[...]).astype(BF16)
    z = _dot(h, w1_ref[...]) + b1_ref[...]
    u = z[:, :d] * jax.nn.sigmoid(z[:, d:])

    @pl.when(i % tiles_per_seq == 0)
    def _():
        u_scr[0:HALO * nl, :] = jnp.zeros((HALO * nl, LANE), F32)

    for l in range(nl):
        u_scr[pl.ds(HALO * nl + l, tm, stride=nl), :] = u[:, l * LANE:(l + 1) * LANE]

    @pl.when(i % tiles_per_seq == tiles_per_seq - 1)
    def _():
        st_ref[...] = u[tm - (width - 1):, :]

    first = HALO - (width - 1)
    taps = [cw_ref[j] for j in range(width)]

    def chunk(r, carry):
        t0 = r * CONV_UNROLL
        slab = lambda t: pl.ds(pl.multiple_of(t * nl, nl), nl)
        acc = [None] * CONV_UNROLL
        for k in range(CONV_UNROLL + width - 1):
            row = u_scr[slab(t0 + first + k), :]
            for o in range(max(0, k - width + 1), min(CONV_UNROLL, k + 1)):
                term = taps[k - o] * row
                acc[o] = term if acc[o] is None else acc[o] + term
        for o in range(CONV_UNROLL):
            c_scr[slab(t0 + o), :] = acc[o]
        return carry

    lax.fori_loop(0, tm // CONV_UNROLL, chunk, 0)

    u_scr[0:HALO * nl, :] = u_scr[tm * nl:(tm + HALO) * nl, :]
    c = jnp.concatenate([c_scr[pl.ds(l, tm, stride=nl), :] for l in range(nl)], axis=1)
    a = _layernorm_swish(c + cb_ref[...], lg_ref[...], lb_ref[...]).astype(BF16)
    o_ref[...] = x + _dot(a, w2_ref[...]) + b2_ref[...]


def _odd_prompt(x, g, w1, w2, layer, b1, cw, cb, lg, lb, b2, n_seq, seq):
    m, d = x.shape
    width = cw.shape[0]
    tm = TM_ODD
    tps = seq // tm
    row = pl.BlockSpec((tm, d), lambda i: (i, 0))
    vec = lambda a: a.reshape(1, -1)
    nl = d // LANE
    assert tm >= width - 1 and tm % CONV_UNROLL == 0 and HALO >= width - 1
    kern = functools.partial(_odd_prompt_kernel, tiles_per_seq=tps, width=width)
    return pl.pallas_call(
        kern,
        grid=(m // tm,),
        in_specs=[row, _resident((1, d)), _resident_layer(w1, layer), _resident((1, 2 * d)),
                  _resident((width, nl, LANE)),
                  _resident((1, d)), _resident((1, d)), _resident((1, d)), _resident_layer(w2, layer),
                  _resident((1, d))],
        out_specs=[row, pl.BlockSpec((None, width - 1, d), lambda i: (i // tps, 0, 0))],
        out_shape=[jax.ShapeDtypeStruct((m, d), F32),
                   jax.ShapeDtypeStruct((n_seq, width - 1, d), F32)],
        scratch_shapes=[pltpu.VMEM(((tm + HALO) * nl, LANE), F32), pltpu.VMEM((tm * nl, LANE), F32)],
        compiler_params=_params(("arbitrary",)),
        name="odd_prompt",
    )(x, vec(g), w1, vec(b1), cw.reshape(width, nl, LANE), vec(cb), vec(lg), vec(lb), w2, vec(b2))


def _odd_sample_kernel(x_ref, g_ref, w1_ref, b1_ref, cw_ref, cb_ref, lg_ref, lb_ref, w2_ref, b2_ref,
                       st_ref, o_ref, nst_ref):
    n_hist = st_ref.shape[0]
    d = x_ref.shape[1]
    x = x_ref[...]
    h = _rms(x, g_ref[...]).astype(BF16)
    z = _dot(h, w1_ref[...]) + b1_ref[...]
    u = z[:, :d] * jax.nn.sigmoid(z[:, d:])
    c = cw_ref[n_hist:n_hist + 1, :] * u
    for j in range(n_hist):
        c = c + cw_ref[j:j + 1, :] * st_ref[j]
    for j in range(n_hist - 1):
        nst_ref[j] = st_ref[j + 1]
    nst_ref[n_hist - 1] = u
    a = _layernorm_swish(c + cb_ref[...], lg_ref[...], lb_ref[...]).astype(BF16)
    o_ref[...] = x + _dot(a, w2_ref[...]) + b2_ref[...]


def _odd_sample(x, g, w1, w2, layer, b1, cw, cb, lg, lb, b2, state_t):
    m, d = x.shape
    vec = lambda a: a.reshape(1, -1)
    whole = lambda shape: pl.BlockSpec(shape, lambda i: (0,) * len(shape))
    return pl.pallas_call(
        _odd_sample_kernel,
        grid=(1,),
        in_specs=[_resident((m, d)), _resident((1, d)), _resident_layer(w1, layer), _resident((1, 2 * d)),
                  _resident(cw.shape), _resident((1, d)), _resident((1, d)), _resident((1, d)),
                  _resident_layer(w2, layer), _resident((1, d)), _resident(state_t.shape)],
        out_specs=[whole((m, d)), whole(state_t.shape)],
        out_shape=[jax.ShapeDtypeStruct((m, d), F32), jax.ShapeDtypeStruct(state_t.shape, F32)],
        compiler_params=_params(("arbitrary",)),
        name="odd_sample",
    )(x, vec(g), w1, vec(b1), cw, vec(cb), vec(lg), vec(lb), w2, vec(b2), state_t)


def _bias_by_distance(rel_bias, dist):
    n = jnp.maximum(dist, 0)
    nf = jnp.maximum(n, 1).astype(F32)
    large = MAX_EXACT + (jnp.log(nf / MAX_EXACT) / math.log(MAX_DISTANCE / MAX_EXACT)
                         * (N_BUCKETS - MAX_EXACT)).astype(jnp.int32)
    large = jnp.minimum(large, N_BUCKETS - 1)
    return rel_bias[jnp.where(n < MAX_EXACT, n, large)]


def _prompt_bias_tiles(by_dist_nat):
    assert TQ >= FAR - 1
    n_head = by_dist_nat.shape[0]
    by_dist = by_dist_nat * LOG2E
    pad = jnp.zeros((n_head, 1), F32)
    diag = jnp.concatenate([jnp.full((n_head, TQ - 1), NEG, F32), by_dist[:, :TQ], pad], axis=1)
    near = jnp.concatenate([by_dist[:, 1:], pad], axis=1)

    def toeplitz(a):
        period = a.shape[1]
        rows = jnp.tile(a, (1, TQ))[:, :TQ * (period - 1)].reshape(n_head, TQ, period - 1)
        return rows[:, :, TQ - 1:]

    tiles = jnp.stack([toeplitz(near), toeplitz(diag)], axis=1)
    return tiles, by_dist[:, TQ + 1]


def _decode_bias(by_dist_nat, past_len):
    n_head, n_tab = by_dist_nat.shape
    assert n_tab > FAR and past_len + 1 >= n_tab
    far = jnp.broadcast_to(by_dist_nat[:, FAR:FAR + 1], (n_head, past_len + 1 - n_tab))
    return jnp.repeat(jnp.concatenate([far, by_dist_nat[:, ::-1]], axis=1), 2, axis=0)


def kernel(x_prompt, x_sample, cache_k, cache_v, state_conv_b, state_conv_c, page_table, rel_bias, norm_ffn1, ffn1_w_gate, ffn1_w_up, ffn1_w_down, norm_mix, norm_ffn2, ffn2_w_gate, ffn2_w_up, ffn2_w_down, w_in_even, w_out_even, q_norm_gain, k_norm_gain, lambda_q1, lambda_k1, lambda_q2, lambda_k2, subln_gain, conv_b_w, w_pw1, b_pw1, conv_c_w, conv_c_b, ln_c_gain, ln_c_bias, w_pw2, b_pw2):
    bp, tp, d = x_prompt.shape
    bs = x_sample.shape[0]
    depth = norm_ffn1.shape[0]
    n_even, n_phys, page, n_sub, dh_qk = cache_k.shape
    assert dh_qk == DH_QK and cache_v.shape[-1] == DH_V and x_sample.shape[1] == 1
    d_qk = n_sub * dh_qk
    d_a = cache_v.shape[3] * DH_V
    past_len = page_table.shape[1] * page
    assert tp % TM_ROWS == 0 and tp % TM_ODD == 0 and tp % TQ == 0 and tp >= 2 * TQ

    bf = lambda w: w.astype(BF16)
    xp = x_prompt.reshape(bp * tp, d)
    xs = x_sample.reshape(bs, d)

    cache_kt = jnp.transpose(cache_k, (0, 1, 3, 4, 2)).reshape(n_even, n_phys, d_qk, page)
    cache_vt = jnp.transpose(cache_v, (0, 1, 3, 4, 2)).reshape(n_even, n_phys, d_a, page)
    state_c_t = jnp.transpose(state_conv_c, (0, 2, 1, 3))

    p_avg = jnp.kron(jnp.eye(d_qk // DH_QK, dtype=F32), jnp.full((DH_QK, DH_QK), 1.0 / DH_QK, F32)).astype(BF16)
    by_dist_nat = _bias_by_distance(rel_bias, jnp.arange(2 * TQ)).T
    bias_tiles, bias_far = _prompt_bias_tiles(by_dist_nat)
    bias_dec = _decode_bias(by_dist_nat, past_len)
    bias_dec_past, bias_dec_new = bias_dec[:, :past_len], bias_dec[:, past_len:]

    w_in, w_out, w1, w2 = (w.astype(BF16) for w in (w_in_even, w_out_even, w_pw1, w_pw2))
    ffn1_f32 = (ffn1_w_gate, ffn1_w_up, ffn1_w_down)
    ffn2_f32 = (ffn2_w_gate, ffn2_w_up, ffn2_w_down)
    ffn_w = tuple(w[0].astype(BF16) for w in ffn1_f32)

    kv_prompt = ()
    ks_l, vs_l, cbp_l, cbs_l, ccp_l, ccs_l = [], [], [], [], [], []
    for li in range(depth):
        xp, xs, ffn_w = _ffn(xp, xs, norm_ffn1[li], ffn_w, cast=(*ffn2_f32, li))
        if li % 2 == 0:
            e = li // 2
            lam_init = 0.8 - 0.6 * math.exp(-0.3 * li)
            qg = jnp.tile(q_norm_gain[e], d_qk // DH_QK).reshape(1, d_qk)
            kg = jnp.tile(k_norm_gain[e], d_qk // DH_QK).reshape(1, d_qk)
            lam_p = jnp.stack([lambda_q1[e], lambda_k1[e], lambda_q2[e], lambda_k2[e]])
            sg = jnp.tile(subln_gain[e], d_a // DH_V)

            *kv_prompt, qtb, kb, vtb, gcp, cbp = _even_in_prompt(
                xp, norm_mix[li], w_in, e, n_even, tuple(kv_prompt), p_avg, qg, kg, conv_b_w[e], bp, tp)
            qs, kn, vn, gcs, us = _even_in_sample(
                xs, norm_mix[li], w_in, e, p_avg, qg, kg, conv_b_w[e],
                state_conv_b[e, :, 0], state_conv_b[e, :, 1])
            attp, atts = _attention(page_table, bias_far, lam_p, sg, qtb, kb, vtb, bias_tiles,
                                    qs, kn, vn, bias_dec_past, bias_dec_new, cache_kt, cache_vt, e,
                                    bp, tp, lam_init)
            mixer = (attp, gcp, atts, gcs, w_out, e)

            ks_l.append(kn)
            vs_l.append(vn)
            cbp_l.append(cbp)
            cbs_l.append(jnp.stack([state_conv_b[e, :, 1], us], axis=1))
        else:
            o = li // 2
            prm = (norm_mix[li], w1, w2, o, b_pw1[o], conv_c_w[o], conv_c_b[o], ln_c_gain[o], ln_c_bias[o], b_pw2[o])
            xp, ccp = _odd_prompt(xp, *prm, bp, tp)
            xs, ccs = _odd_sample(xs, *prm, state_c_t[o])
            ccp_l.append(ccp)
            ccs_l.append(ccs)
            mixer = ()
        xp, xs, ffn_w = _ffn(xp, xs, norm_ffn2[li], ffn_w, mixer=mixer,
                             cast=(*ffn1_f32, li + 1) if li + 1 < depth else ())

    n_h = d_a // DH_V
    kt_all, vt_all = kv_prompt
    new_k_prompt = jnp.transpose(kt_all.reshape(n_even, bp, n_sub, DH_QK, tp), (0, 1, 4, 2, 3))
    new_v_prompt = jnp.transpose(vt_all.reshape(n_even, bp, n_h, DH_V, tp), (0, 1, 4, 2, 3))
    return (xp.reshape(bp, tp, d), xs.reshape(bs, 1, d),
            new_k_prompt, new_v_prompt,
            jnp.stack(ks_l).reshape(n_even, bs, 1, n_sub, DH_QK),
            jnp.stack(vs_l).reshape(n_even, bs, 1, n_h, DH_V),
            jnp.stack(cbp_l), jnp.stack(cbs_l), jnp.stack(ccp_l),
            jnp.transpose(jnp.stack(ccs_l), (0, 2, 1, 3)))
```

```python
import functools
import math

import jax
import jax.numpy as jnp
from jax import lax
from jax.experimental import pallas as pl
from jax.experimental.pallas import tpu as pltpu

F32 = jnp.float32
BF16 = jnp.bfloat16
EPS = 1e-6
NEG = -1e30

LANE = 128
SUBLANE = 8
MXU_DEPTH = 256
VMEM_LIMIT_BYTES = 56 * 1024 * 1024

DH_QK = 32
DH_V = 64
N_BUCKETS = 32
MAX_EXACT = N_BUCKETS // 2
MAX_DISTANCE = 128
FAR = MAX_DISTANCE

TM_ROWS = 512
TM_FFN = 512
FFN_CHUNKS = 2
CAST_STEPS = 16
TM_ODD = 256
TQ = 256
SOFTMAX_SUM_ROWS = 16
HALO = 32
CONV_UNROLL = 16
LOG2E = math.log2(math.e)


def _params(sem):
    return pltpu.CompilerParams(dimension_semantics=sem, vmem_limit_bytes=VMEM_LIMIT_BYTES)


def _resident(shape):
    n = len(shape)
    return pl.BlockSpec(shape, lambda *_: (0,) * n, pipeline_mode=pl.Buffered(1))


def _resident_layer(stacked, layer):
    n = stacked.ndim - 1
    return pl.BlockSpec((None,) + stacked.shape[1:], lambda *_: (layer,) + (0,) * n,
                        pipeline_mode=pl.Buffered(1))


def _rms(x, g):
    return x * lax.rsqrt(jnp.mean(x * x, axis=-1, keepdims=True) + EPS) * g


def _dot(a, b):
    return jnp.dot(a, b, preferred_element_type=F32)


def _group_rms(z, p_ref, gain):
    sq = z * z
    hi = sq.astype(BF16)
    lo = (sq - hi.astype(F32)).astype(BF16)
    ms = _dot(hi, p_ref[...]) + _dot(lo, p_ref[...])
    return z * lax.rsqrt(ms + EPS) * gain


def _swiglu_rows(x, mix, g_ref, wg_ref, wu_ref, wd_ref, wo_ref, n_chunks):
    if mix:
        att, gc = mix
        d_a = att.shape[1]
        x = x + _dot(att.astype(BF16), wo_ref[0:d_a, :]) + _dot(gc, wo_ref[d_a:, :])
    h = _rms(x, g_ref[...]).astype(BF16)
    n_pass = wg_ref.shape[1] // MXU_DEPTH
    cuts = [MXU_DEPTH * ((n_pass * c + n_chunks - 1) // n_chunks) for c in range(n_chunks)] + [wg_ref.shape[1]]
    y = jnp.zeros(x.shape, F32)
    for lo, hi in zip(cuts[:-1], cuts[1:]):
        gate = _dot(h, wg_ref[:, lo:hi])
        up = _dot(h, wu_ref[:, lo:hi])
        a = (jax.nn.silu(gate) * up).astype(BF16)
        y = y + _dot(a, wd_ref[lo:hi, :])
    return x + 0.5 * y


def _ffn_kernel(x_ref, g_ref, wg_ref, wu_ref, wd_ref, *rest, n_chunks, mixer, n_cast):
    rest = list(rest)
    take = lambda n: [rest.pop(0) for _ in range(n)]
    a_ref, gc_ref, wo_ref = take(3) if mixer else (None, None, None)
    xs_ref, = take(1)
    as_ref, gcs_ref = take(2) if mixer else (None, None)
    cast_src = take(n_cast)
    o_ref, os_ref = take(2)
    cast_dst = take(n_cast)
    step = pl.program_id(0)
    if n_cast:
        @pl.when(step < CAST_STEPS)
        def _():
            for src, dst in zip(cast_src, cast_dst):
                dst[...] = src[...].astype(BF16)

    @pl.when(step == 0)
    def _():
        mix_s = (as_ref[...], gcs_ref[...]) if mixer else ()
        os_ref[...] = _swiglu_rows(xs_ref[...], mix_s, g_ref, wg_ref, wu_ref, wd_ref, wo_ref, n_chunks)

    mix = (a_ref[...], gc_ref[...]) if mixer else ()
    o_ref[...] = _swiglu_rows(x_ref[...], mix, g_ref, wg_ref, wu_ref, wd_ref, wo_ref, n_chunks)


def _ffn(x, xs, g, weights, mixer=(), cast=()):
    m, d = x.shape
    tm = TM_FFN
    steps = m // tm
    row = lambda width: pl.BlockSpec((tm, width), lambda i: (i, 0))
    if mixer:
        att, gc, att_s, gc_s, w_out, out_layer = mixer
        mixer_ops = [att, gc, w_out]
        mixer_specs = [row(att.shape[1]), row(gc.shape[1]), _resident_layer(w_out, out_layer)]
        side_ops = [xs, att_s, gc_s]
    else:
        mixer_ops, mixer_specs, side_ops = [], [], [xs]
    side_specs = [_resident(a.shape) for a in side_ops]
    *stacks, cast_layer = cast if cast else (None,)
    assert not stacks or steps >= CAST_STEPS
    slab = lambda i: jnp.minimum(i, CAST_STEPS - 1)
    cast_in = [pl.BlockSpec((None, w.shape[1] // CAST_STEPS, w.shape[2]), lambda i: (cast_layer, slab(i), 0))
               for w in stacks]
    cast_out = [pl.BlockSpec((w.shape[1] // CAST_STEPS, w.shape[2]), lambda i: (slab(i), 0)) for w in stacks]
    out = pl.pallas_call(
        functools.partial(_ffn_kernel, n_chunks=FFN_CHUNKS, mixer=bool(mixer), n_cast=len(stacks)),
        grid=(steps,),
        in_specs=[row(d), _resident((1, d))] + [_resident(w.shape) for w in weights]
                 + mixer_specs + side_specs + cast_in,
        out_specs=[row(d), pl.BlockSpec(xs.shape, lambda i: (0, 0))] + cast_out,
        out_shape=[jax.ShapeDtypeStruct((m, d), F32), jax.ShapeDtypeStruct(xs.shape, F32)]
                  + [jax.ShapeDtypeStruct(w.shape[1:], BF16) for w in stacks],
        compiler_params=_params(("arbitrary",)),
        name="ffn_mixer" if mixer else "ffn",
    )(x, g.reshape(1, d), *weights, *mixer_ops, *side_ops, *stacks)
    return out[0], out[1], tuple(out[2:])


def _even_in_prompt_kernel(x_ref, g_ref, w_ref, p_ref, qg_ref, kg_ref, cw_ref, *rest,
                           tiles_per_seq, d_qk, d_a, d_b, scale, layer):
    kt_ref, vt_ref, qtb_ref, kb_ref, vtb_ref, gc_ref, st_ref, u_scr = rest[-8:]
    i = pl.program_id(0)
    tm = x_ref.shape[0]
    h = _rms(x_ref[...], g_ref[...]).astype(BF16)
    z = _dot(h, w_ref[...])
    q = _group_rms(z[:, :d_qk], p_ref, qg_ref[...]) * scale
    k = _group_rms(z[:, d_qk:2 * d_qk], p_ref, kg_ref[...])
    o = 2 * d_qk
    v = z[:, o:o + d_a]
    o += d_a
    gate_b = z[:, o:o + d_b]
    u = z[:, o + d_b:o + 2 * d_b] * z[:, o + 2 * d_b:o + 3 * d_b]

    kt = k.T
    vt = v.T
    for ref, val in ((kt_ref, kt), (vt_ref, vt)):
        if len(ref.shape) == 2:
            ref[...] = val
        else:
            for slot in range(ref.shape[0]):
                ref[slot] = val if slot == layer else jnp.zeros_like(val)
    qtb_ref[...] = q.T.astype(BF16)
    kb_ref[...] = k.astype(BF16)
    vtb_ref[...] = vt.astype(BF16)

    @pl.when(i % tiles_per_seq == 0)
    def _():
        u_scr[0:SUBLANE, :] = jnp.zeros((SUBLANE, d_b), F32)

    u_scr[SUBLANE:SUBLANE + tm, :] = u
    conv = (cw_ref[0:1, :] * u_scr[SUBLANE - 2:SUBLANE - 2 + tm, :]
            + cw_ref[1:2, :] * u_scr[SUBLANE - 1:SUBLANE - 1 + tm, :]
            + cw_ref[2:3, :] * u)
    gc_ref[...] = (gate_b * conv).astype(BF16)
    st_ref[...] = u_scr[tm + SUBLANE - 2:tm + SUBLANE, :]
    u_scr[0:SUBLANE, :] = u_scr[tm:tm + SUBLANE, :]


def _even_in_prompt(x, g, w_in, layer, n_layers, kv_prev, p_avg, qg, kg, cw, n_seq, seq):
    m, d = x.shape
    d_qk = d_a = d_b = p_avg.shape[0]
    tm = TM_ROWS
    tps = seq // tm
    row = lambda width, dt=None: pl.BlockSpec((tm, width), lambda i: (i, 0))
    col = pl.BlockSpec((d_qk, tm), lambda i: (0, i))
    if kv_prev:
        seq_t = pl.BlockSpec((None, None, d_qk, tm), lambda i: (layer, i // tps, 0, i % tps))
    else:
        seq_t = pl.BlockSpec((n_layers, None, d_qk, tm), lambda i: (0, i // tps, 0, i % tps))
    n_in = 7
    kern = functools.partial(_even_in_prompt_kernel, tiles_per_seq=tps, d_qk=d_qk, d_a=d_a, d_b=d_b, layer=layer,
                             scale=DH_QK ** -0.5 * LOG2E)
    return pl.pallas_call(
        kern,
        grid=(m // tm,),
        in_specs=[row(d), _resident((1, d)), _resident_layer(w_in, layer), _resident(p_avg.shape),
                  _resident((1, d_qk)), _resident((1, d_qk)), _resident(cw.shape)]
                 + [pl.BlockSpec(memory_space=pl.ANY)] * len(kv_prev),
        out_specs=[seq_t, seq_t, col, row(d_qk), col, row(d_b),
                   pl.BlockSpec((None, 2, d_b), lambda i: (i // tps, 0, 0))],
        out_shape=[jax.ShapeDtypeStruct((n_layers, n_seq, d_qk, seq), F32),
                   jax.ShapeDtypeStruct((n_layers, n_seq, d_a, seq), F32),
                   jax.ShapeDtypeStruct((d_qk, m), BF16),
                   jax.ShapeDtypeStruct((m, d_qk), BF16),
                   jax.ShapeDtypeStruct((d_a, m), BF16),
                   jax.ShapeDtypeStruct((m, d_b), BF16),
                   jax.ShapeDtypeStruct((n_seq, 2, d_b), F32)],
        scratch_shapes=[pltpu.VMEM((tm + SUBLANE, d_b), F32)],
        input_output_aliases={n_in + j: j for j in range(len(kv_prev))},
        compiler_params=_params(("arbitrary",)),
        name="even_in_prompt",
    )(x, g.reshape(1, d), w_in, p_avg, qg, kg, cw, *kv_prev)


def _even_in_sample_kernel(x_ref, g_ref, w_ref, p_ref, qg_ref, kg_ref, cw_ref, s0_ref, s1_ref,
                           q_ref, k_ref, v_ref, gc_ref, u_ref, *, d_qk, d_a, d_b, scale):
    h = _rms(x_ref[...], g_ref[...]).astype(BF16)
    z = _dot(h, w_ref[...])
    q_ref[...] = _group_rms(z[:, :d_qk], p_ref, qg_ref[...]) * scale
    k_ref[...] = _group_rms(z[:, d_qk:2 * d_qk], p_ref, kg_ref[...])
    o = 2 * d_qk
    v_ref[...] = z[:, o:o + d_a]
    o += d_a
    gate_b = z[:, o:o + d_b]
    u = z[:, o + d_b:o + 2 * d_b] * z[:, o + 2 * d_b:o + 3 * d_b]
    conv = cw_ref[0:1, :] * s0_ref[...] + cw_ref[1:2, :] * s1_ref[...] + cw_ref[2:3, :] * u
    gc_ref[...] = (gate_b * conv).astype(BF16)
    u_ref[...] = u


def _even_in_sample(x, g, w_in, layer, p_avg, qg, kg, cw, s0, s1):
    m, d = x.shape
    d_qk = d_a = d_b = p_avg.shape[0]
    kern = functools.partial(_even_in_sample_kernel, d_qk=d_qk, d_a=d_a, d_b=d_b, scale=DH_QK ** -0.5)
    f = lambda dt: jax.ShapeDtypeStruct((m, d_qk), dt)
    blk = _resident((m, d_qk))
    return pl.pallas_call(
        kern,
        grid=(1,),
        in_specs=[_resident((m, d)), _resident((1, d)), _resident_layer(w_in, layer), _resident(p_avg.shape),
                  _resident((1, d_qk)), _resident((1, d_qk)), _resident(cw.shape), blk, blk],
        out_specs=[pl.BlockSpec((m, d_qk), lambda i: (0, 0))] * 5,
        out_shape=[f(F32), f(F32), f(F32), f(BF16), f(F32)],
        compiler_params=_params(("arbitrary",)),
        name="even_in_sample",
    )(x, g.reshape(1, d), w_in, p_avg, qg, kg, cw, s0, s1)


def _lambda(lam_ref, lam_init):
    lam_p = lam_ref[...]
    return (jnp.exp(jnp.sum(lam_p[0:1] * lam_p[1:2], axis=1, keepdims=True))
            - jnp.exp(jnp.sum(lam_p[2:3] * lam_p[3:4], axis=1, keepdims=True)) + lam_init)


def _prompt_tiles(tiles, hp, lam, cfar_ref, sg_ref, qt_ref, k_ref, vt_ref, bias_ref, o_ref, p_scr, *,
                  lam_init, tq):
    n_sub = qt_ref.shape[0] // DH_QK
    n_head = n_sub // 2
    cfar = jnp.concatenate([jnp.full((1, 2 * tq), cfar_ref[hp * n_head + h], F32) for h in range(n_head)],
                           axis=1)
    grp = lax.broadcasted_iota(jnp.int32, (qt_ref.shape[0], tq), 0) // DH_QK

    def scores(n):
        qt = qt_ref[:, n * tq:(n + 1) * tq]
        q_blk = jnp.concatenate([jnp.where(grp == c, qt, jnp.zeros_like(qt)) for c in range(n_sub)], axis=1)
        return _dot(k_ref[0:(n + 1) * tq, :], q_blk)

    s_next = scores(tiles[0])
    for idx, n in enumerate(tiles):
        s_all = s_next
        if idx + 1 < len(tiles):
            s_next = scores(tiles[idx + 1])
        p_buf = p_scr.at[idx % 2]

        m = None
        s_tiles = []
        for t in range(n + 1):
            s = s_all[t * tq:(t + 1) * tq, :]
            if t >= n - 1:
                kind = 1 if t == n else 0
                s = jnp.concatenate([s[:, c * tq:(c + 1) * tq] + bias_ref[c // 2, kind] for c in range(n_sub)],
                                    axis=1)
                smax = jnp.max(s, axis=0, keepdims=True)
            else:
                smax = jnp.max(s, axis=0, keepdims=True) + cfar
            m = smax if m is None else jnp.maximum(m, smax)
            s_tiles.append(s)

        m_far = m - cfar
        for t in range(n + 1):
            shift = m if t >= n - 1 else m_far
            p_buf[t * tq:(t + 1) * tq, :] = jnp.exp2(s_tiles[t] - shift).astype(BF16)

        keys = (n + 1) * tq
        ones = jnp.ones((SOFTMAX_SUM_ROWS, keys), BF16)
        outs = []
        for h in range(n_head):
            vte = jnp.concatenate([vt_ref[h * DH_V:(h + 1) * DH_V, 0:keys], ones], axis=0)
            pv = _dot(vte, p_buf[0:keys, h * 2 * tq:(h + 1) * 2 * tq])
            sub = [pv[0:DH_V, c * tq:(c + 1) * tq] / pv[DH_V:DH_V + 1, c * tq:(c + 1) * tq] for c in range(2)]
            o = sub[0] - lam * sub[1]
            ms = jnp.mean(o * o, axis=0, keepdims=True)
            outs.append(o * lax.rsqrt(ms + EPS))
        o = jnp.concatenate(outs, axis=0) * sg_ref[...] * (1.0 - lam_init)
        o_ref[n * tq:(n + 1) * tq, :] = o.T.astype(BF16)


def _decode_pages(first, last, lam, q_ref, kn_ref, vn_ref, bias, b0_ref, sg_ref, k_refs, v_refs, o_ref,
                  m_scr, l_scr, acc_scr, *, lam_init):
    pp = len(k_refs)
    n_sub, d = acc_scr.shape
    page = k_refs[0].shape[1]

    row = lax.broadcasted_iota(jnp.int32, (n_sub, d), 0)
    col = lax.broadcasted_iota(jnp.int32, (n_sub, d), 1)
    qexp = jnp.where(col // DH_QK == row, jnp.broadcast_to(q_ref[...], (n_sub, d)), 0.0)

    if first:
        m_scr[...] = jnp.sum(qexp * kn_ref[...], axis=1, keepdims=True) + b0_ref[...]
        l_scr[...] = jnp.ones(l_scr.shape, F32)
        acc_scr[...] = jnp.broadcast_to(vn_ref[...], (n_sub, d))

    qb = qexp.astype(BF16)
    s = jnp.concatenate([_dot(qb, k_refs[i][...].astype(BF16)) for i in range(pp)], axis=1)
    s = s + bias
    m_old = m_scr[...]
    m_new = jnp.maximum(m_old, jnp.max(s, axis=1, keepdims=True))
    p = jnp.exp(s - m_new)
    alpha = jnp.exp(m_old - m_new)
    l_scr[...] = alpha * l_scr[...] + jnp.sum(p, axis=1, keepdims=True)
    pb = p.astype(BF16)
    pv = jnp.zeros((n_sub, d), F32)
    for i in range(pp):
        pv = pv + lax.dot_general(pb[:, i * page:(i + 1) * page], v_refs[i][...].astype(BF16),
                                  (((1,), (1,)), ((), ())), preferred_element_type=F32)
    acc_scr[...] = alpha * acc_scr[...] + pv
    m_scr[...] = m_new

    if last:
        o = acc_scr[...] / l_scr[...]
        head = col // DH_V
        a0 = jnp.sum(jnp.where(row == 2 * head, o, 0.0), axis=0, keepdims=True)
        a1 = jnp.sum(jnp.where(row == 2 * head + 1, o, 0.0), axis=0, keepdims=True)
        att = a0 - lam * a1
        sq = att * att
        head1 = head[0:1]
        r = jnp.zeros_like(att)
        for hh in range(d // DH_V):
            ms = jnp.sum(jnp.where(head1 == hh, sq, 0.0), axis=1, keepdims=True) * (1.0 / DH_V)
            r = jnp.where(head1 == hh, lax.rsqrt(ms + EPS), r)
        o_ref[...] = att * r * sg_ref[...] * (1.0 - lam_init)


def _attention_kernel(pt_ref, cfar_ref, lam_ref, sgc_ref, qt_ref, k_ref, vt_ref, bias_ref,
                      q_ref, kn_ref, vn_ref, bt_ref, b0_ref, sgr_ref, *rest, pp, tiles, lam_init, tq, layer):
    ck_hbm, cv_hbm, o_ref, od_ref, p_scr, m_scr, l_scr, acc_scr, kbuf, vbuf, sem = rest
    hp = pl.program_id(1)
    half = pl.program_id(2)
    step = (pl.program_id(0) * pl.num_programs(1) + hp) * pl.num_programs(2) + half
    last_step = pl.num_programs(0) * pl.num_programs(1) * pl.num_programs(2) - 1
    slot = step % 2

    def page_copies(of_step, into):
        copies = []
        for i in range(pp):
            pid = pt_ref[of_step * pp + i]
            copies.append(pltpu.make_async_copy(ck_hbm.at[layer, pid], kbuf.at[into, i], sem.at[into, 0]))
            copies.append(pltpu.make_async_copy(cv_hbm.at[layer, pid], vbuf.at[into, i], sem.at[into, 1]))
        return copies

    @pl.when(step == 0)
    def _():
        for c in page_copies(0, 0):
            c.start()

    lam = _lambda(lam_ref, lam_init)
    nxt = jnp.minimum(step + 1, last_step)
    for hv in range(2):
        @pl.when(half == hv)
        def _(hv=hv):
            for c in page_copies(nxt, 1 - slot):
                c.start()
            _prompt_tiles(tiles[hv], hp, lam, cfar_ref, sgc_ref, qt_ref, k_ref, vt_ref, bias_ref, o_ref, p_scr,
                          lam_init=lam_init, tq=tq)
            for c in page_copies(step, slot):
                c.wait()
            k_refs = [kbuf.at[slot, i] for i in range(pp)]
            v_refs = [vbuf.at[slot, i] for i in range(pp)]
            _decode_pages(hv == 0, hv == 1, lam, q_ref, kn_ref, vn_ref, bt_ref[...], b0_ref, sgr_ref,
                          k_refs, v_refs, od_ref, m_scr, l_scr, acc_scr, lam_init=lam_init)

    @pl.when(step == last_step)
    def _():
        for c in page_copies(last_step, 1 - slot):
            c.wait()


def _attention(page_table, cfar, lam_p, sg, qt, k, vt, bias_tiles, q, kn, vn, bias_t, bias0,
               cache_kt, cache_vt, layer, n_seq, seq, lam_init):
    d_qk, m = qt.shape
    n_dec, n_pages = page_table.shape
    d = q.shape[-1]
    page = cache_kt.shape[-1]
    pair = 2 * 2 * DH_QK
    n_pair = d_qk // pair
    nq = seq // TQ
    order = sorted(range(nq), reverse=True)
    tiles = ([n for i, n in enumerate(order) if i % 4 in (0, 3)], [n for i, n in enumerate(order) if i % 4 in (1, 2)])
    halves = 2
    pp = n_pages // halves
    assert n_pages % halves == 0 and n_seq * n_pair == n_dec, "one sample sequence per (prompt sequence, head pair)"
    kern = functools.partial(_attention_kernel, pp=pp, tiles=tiles, lam_init=lam_init, tq=TQ, layer=layer)
    const = lambda shape: pl.BlockSpec(shape, lambda b, h, s, pt: (0,) * len(shape))
    tok = pl.BlockSpec((None, 1, d), lambda b, h, s, pt: (b * n_pair + h, 0, 0))
    grid_spec = pltpu.PrefetchScalarGridSpec(
        num_scalar_prefetch=1,
        grid=(n_seq, n_pair, halves),
        in_specs=[pl.BlockSpec(memory_space=pltpu.SMEM),
                  const(lam_p.shape),
                  const((pair, 1)),
                  pl.BlockSpec((pair, seq), lambda b, h, s, pt: (h, b)),
                  pl.BlockSpec((seq, pair), lambda b, h, s, pt: (b, h)),
                  pl.BlockSpec((pair, seq), lambda b, h, s, pt: (h, b)),
                  pl.BlockSpec((2, 2, TQ, TQ), lambda b, h, s, pt: (h, 0, 0, 0)),
                  tok, tok, tok,
                  pl.BlockSpec((d // DH_QK, pp * page), lambda b, h, s, pt: (0, s)),
                  const((d // DH_QK, 1)),
                  const((1, d)),
                  pl.BlockSpec(memory_space=pl.ANY), pl.BlockSpec(memory_space=pl.ANY)],
        out_specs=[pl.BlockSpec((seq, pair), lambda b, h, s, pt: (b, h)), tok],
        scratch_shapes=[pltpu.VMEM((2, seq, (pair // DH_QK) * TQ), BF16),
                        pltpu.VMEM((d // DH_QK, 1), F32), pltpu.VMEM((d // DH_QK, 1), F32),
                        pltpu.VMEM((d // DH_QK, d), F32),
                        pltpu.VMEM((2, pp, d, page), F32), pltpu.VMEM((2, pp, d, page), F32),
                        pltpu.SemaphoreType.DMA((2, 2))],
    )
    att_p, att_s = pl.pallas_call(
        kern,
        grid_spec=grid_spec,
        out_shape=[jax.ShapeDtypeStruct((m, d_qk), BF16), jax.ShapeDtypeStruct((n_dec, 1, d), F32)],
        compiler_params=_params(("arbitrary", "arbitrary", "arbitrary")),
        name="attention",
    )(page_table.reshape(-1), cfar, lam_p, sg[:pair].reshape(-1, 1), qt, k, vt, bias_tiles,
      q.reshape(n_dec, 1, d), kn.reshape(n_dec, 1, d), vn.reshape(n_dec, 1, d), bias_t, bias0, sg.reshape(1, -1),
      cache_kt, cache_vt)
    return att_p, att_s.reshape(n_dec, d)


def _layernorm_swish(c, g, b):
    mu = jnp.mean(c, axis=-1, keepdims=True)
    xc = c - mu
    y = xc * lax.rsqrt(jnp.mean(xc * xc, axis=-1, keepdims=True) + EPS) * g + b
    return jax.nn.silu(y)


def _odd_prompt_kernel(x_ref, g_ref, w1_ref, b1_ref, cw_ref, cb_ref, lg_ref, lb_ref, w2_ref, b2_ref,
                       o_ref, st_ref, u_scr, c_scr, *, tiles_per_seq, width):
    i = pl.program_id(0)
    tm, d = x_ref.shape
    nl = d // LANE
    x = x_ref[...]
    h = _rms(x, g_ref[...]).astype(BF16)
    z = _dot(h, w1_ref[...]) + b1_ref[...]
    u = z[:, :d] * jax.nn.sigmoid(z[:, d:])

    @pl.when(i % tiles_per_seq == 0)
    def _():
        u_scr[0:HALO * nl, :] = jnp.zeros((HALO * nl, LANE), F32)

    for l in range(nl):
        u_scr[pl.ds(HALO * nl + l, tm, stride=nl), :] = u[:, l * LANE:(l + 1) * LANE]

    @pl.when(i % tiles_per_seq == tiles_per_seq - 1)
    def _():
        st_ref[...] = u[tm - (width - 1):, :]

    first = HALO - (width - 1)
    taps = [cw_ref[j] for j in range(width)]

    def chunk(r, carry):
        t0 = r * CONV_UNROLL
        slab = lambda t: pl.ds(pl.multiple_of(t * nl, nl), nl)
        acc = [None] * CONV_UNROLL
        for k in range(CONV_UNROLL + width - 1):
            row = u_scr[slab(t0 + first + k), :]
            for o in range(max(0, k - width + 1), min(CONV_UNROLL, k + 1)):
                term = taps[k - o] * row
                acc[o] = term if acc[o] is None else acc[o] + term
        for o in range(CONV_UNROLL):
            c_scr[slab(t0 + o), :] = acc[o]
        return carry

    lax.fori_loop(0, tm // CONV_UNROLL, chunk, 0)

    u_scr[0:HALO * nl, :] = u_scr[tm * nl:(tm + HALO) * nl, :]
    c = jnp.concatenate([c_scr[pl.ds(l, tm, stride=nl), :] for l in range(nl)], axis=1)
    a = _layernorm_swish(c + cb_ref[...], lg_ref[...], lb_ref[...]).astype(BF16)
    o_ref[...] = x + _dot(a, w2_ref[...]) + b2_ref[...]


def _odd_prompt(x, g, w1, w2, layer, b1, cw, cb, lg, lb, b2, n_seq, seq):
    m, d = x.shape
    width = cw.shape[0]
    tm = TM_ODD
    tps = seq // tm
    row = pl.BlockSpec((tm, d), lambda i: (i, 0))
    vec = lambda a: a.reshape(1, -1)
    nl = d // LANE
    assert tm >= width - 1 and tm % CONV_UNROLL == 0 and HALO >= width - 1
    kern = functools.partial(_odd_prompt_kernel, tiles_per_seq=tps, width=width)
    return pl.pallas_call(
        kern,
        grid=(m // tm,),
        in_specs=[row, _resident((1, d)), _resident_layer(w1, layer), _resident((1, 2 * d)),
                  _resident((width, nl, LANE)),
                  _resident((1, d)), _resident((1, d)), _resident((1, d)), _resident_layer(w2, layer),
                  _resident((1, d))],
        out_specs=[row, pl.BlockSpec((None, width - 1, d), lambda i: (i // tps, 0, 0))],
        out_shape=[jax.ShapeDtypeStruct((m, d), F32),
                   jax.ShapeDtypeStruct((n_seq, width - 1, d), F32)],
        scratch_shapes=[pltpu.VMEM(((tm + HALO) * nl, LANE), F32), pltpu.VMEM((tm * nl, LANE), F32)],
        compiler_params=_params(("arbitrary",)),
        name="odd_prompt",
    )(x, vec(g), w1, vec(b1), cw.reshape(width, nl, LANE), vec(cb), vec(lg), vec(lb), w2, vec(b2))


def _odd_sample_kernel(x_ref, g_ref, w1_ref, b1_ref, cw_ref, cb_ref, lg_ref, lb_ref, w2_ref, b2_ref,
                       st_ref, o_ref, nst_ref):
    n_hist = st_ref.shape[0]
    d = x_ref.shape[1]
    x = x_ref[...]
    h = _rms(x, g_ref[...]).astype(BF16)
    z = _dot(h, w1_ref[...]) + b1_ref[...]
    u = z[:, :d] * jax.nn.sigmoid(z[:, d:])
    c = cw_ref[n_hist:n_hist + 1, :] * u
    for j in range(n_hist):
        c = c + cw_ref[j:j + 1, :] * st_ref[j]
    for j in range(n_hist - 1):
        nst_ref[j] = st_ref[j + 1]
    nst_ref[n_hist - 1] = u
    a = _layernorm_swish(c + cb_ref[...], lg_ref[...], lb_ref[...]).astype(BF16)
    o_ref[...] = x + _dot(a, w2_ref[...]) + b2_ref[...]


def _odd_sample(x, g, w1, w2, layer, b1, cw, cb, lg, lb, b2, state_t):
    m, d = x.shape
    vec = lambda a: a.reshape(1, -1)
    whole = lambda shape: pl.BlockSpec(shape, lambda i: (0,) * len(shape))
    return pl.pallas_call(
        _odd_sample_kernel,
        grid=(1,),
        in_specs=[_resident((m, d)), _resident((1, d)), _resident_layer(w1, layer), _resident((1, 2 * d)),
                  _resident(cw.shape), _resident((1, d)), _resident((1, d)), _resident((1, d)),
                  _resident_layer(w2, layer), _resident((1, d)), _resident(state_t.shape)],
        out_specs=[whole((m, d)), whole(state_t.shape)],
        out_shape=[jax.ShapeDtypeStruct((m, d), F32), jax.ShapeDtypeStruct(state_t.shape, F32)],
        compiler_params=_params(("arbitrary",)),
        name="odd_sample",
    )(x, vec(g), w1, vec(b1), cw, vec(cb), vec(lg), vec(lb), w2, vec(b2), state_t)


def _bias_by_distance(rel_bias, dist):
    n = jnp.maximum(dist, 0)
    nf = jnp.maximum(n, 1).astype(F32)
    large = MAX_EXACT + (jnp.log(nf / MAX_EXACT) / math.log(MAX_DISTANCE / MAX_EXACT)
                         * (N_BUCKETS - MAX_EXACT)).astype(jnp.int32)
    large = jnp.minimum(large, N_BUCKETS - 1)
    return rel_bias[jnp.where(n < MAX_EXACT, n, large)]


def _prompt_bias_tiles(by_dist_nat):
    assert TQ >= FAR - 1
    n_head = by_dist_nat.shape[0]
    by_dist = by_dist_nat * LOG2E
    pad = jnp.zeros((n_head, 1), F32)
    diag = jnp.concatenate([jnp.full((n_head, TQ - 1), NEG, F32), by_dist[:, :TQ], pad], axis=1)
    near = jnp.concatenate([by_dist[:, 1:], pad], axis=1)

    def toeplitz(a):
        period = a.shape[1]
        rows = jnp.tile(a, (1, TQ))[:, :TQ * (period - 1)].reshape(n_head, TQ, period - 1)
        return rows[:, :, TQ - 1:]

    tiles = jnp.stack([toeplitz(near), toeplitz(diag)], axis=1)
    return tiles, by_dist[:, TQ + 1]


def _decode_bias(by_dist_nat, past_len):
    n_head, n_tab = by_dist_nat.shape
    assert n_tab > FAR and past_len + 1 >= n_tab
    far = jnp.broadcast_to(by_dist_nat[:, FAR:FAR + 1], (n_head, past_len + 1 - n_tab))
    return jnp.repeat(jnp.concatenate([far, by_dist_nat[:, ::-1]], axis=1), 2, axis=0)


def kernel(x_prompt, x_sample, cache_k, cache_v, state_conv_b, state_conv_c, page_table, rel_bias, norm_ffn1, ffn1_w_gate, ffn1_w_up, ffn1_w_down, norm_mix, norm_ffn2, ffn2_w_gate, ffn2_w_up, ffn2_w_down, w_in_even, w_out_even, q_norm_gain, k_norm_gain, lambda_q1, lambda_k1, lambda_q2, lambda_k2, subln_gain, conv_b_w, w_pw1, b_pw1, conv_c_w, conv_c_b, ln_c_gain, ln_c_bias, w_pw2, b_pw2):
    bp, tp, d = x_prompt.shape
    bs = x_sample.shape[0]
    depth = norm_ffn1.shape[0]
    n_even, n_phys, page, n_sub, dh_qk = cache_k.shape
    assert dh_qk == DH_QK and cache_v.shape[-1] == DH_V and x_sample.shape[1] == 1
    d_qk = n_sub * dh_qk
    d_a = cache_v.shape[3] * DH_V
    past_len = page_table.shape[1] * page
    assert tp % TM_ROWS == 0 and tp % TM_ODD == 0 and tp % TQ == 0 and tp >= 2 * TQ

    bf = lambda w: w.astype(BF16)
    xp = x_prompt.reshape(bp * tp, d)
    xs = x_sample.reshape(bs, d)

    cache_kt = jnp.transpose(cache_k, (0, 1, 3, 4, 2)).reshape(n_even, n_phys, d_qk, page)
    cache_vt = jnp.transpose(cache_v, (0, 1, 3, 4, 2)).reshape(n_even, n_phys, d_a, page)
    state_c_t = jnp.transpose(state_conv_c, (0, 2, 1, 3))

    p_avg = jnp.kron(jnp.eye(d_qk // DH_QK, dtype=F32), jnp.full((DH_QK, DH_QK), 1.0 / DH_QK, F32)).astype(BF16)
    by_dist_nat = _bias_by_distance(rel_bias, jnp.arange(2 * TQ)).T
    bias_tiles, bias_far = _prompt_bias_tiles(by_dist_nat)
    bias_dec = _decode_bias(by_dist_nat, past_len)
    bias_dec_past, bias_dec_new = bias_dec[:, :past_len], bias_dec[:, past_len:]

    w_in, w_out, w1, w2 = (w.astype(BF16) for w in (w_in_even, w_out_even, w_pw1, w_pw2))
    ffn1_f32 = (ffn1_w_gate, ffn1_w_up, ffn1_w_down)
    ffn2_f32 = (ffn2_w_gate, ffn2_w_up, ffn2_w_down)
    ffn_w = tuple(w[0].astype(BF16) for w in ffn1_f32)

    kv_prompt = ()
    ks_l, vs_l, cbp_l, cbs_l, ccp_l, ccs_l = [], [], [], [], [], []
    for li in range(depth):
        xp, xs, ffn_w = _ffn(xp, xs, norm_ffn1[li], ffn_w, cast=(*ffn2_f32, li))
        if li % 2 == 0:
            e = li // 2
            lam_init = 0.8 - 0.6 * math.exp(-0.3 * li)
            qg = jnp.tile(q_norm_gain[e], d_qk // DH_QK).reshape(1, d_qk)
            kg = jnp.tile(k_norm_gain[e], d_qk // DH_QK).reshape(1, d_qk)
            lam_p = jnp.stack([lambda_q1[e], lambda_k1[e], lambda_q2[e], lambda_k2[e]])
            sg = jnp.tile(subln_gain[e], d_a // DH_V)

            *kv_prompt, qtb, kb, vtb, gcp, cbp = _even_in_prompt(
                xp, norm_mix[li], w_in, e, n_even, tuple(kv_prompt), p_avg, qg, kg, conv_b_w[e], bp, tp)
            qs, kn, vn, gcs, us = _even_in_sample(
                xs, norm_mix[li], w_in, e, p_avg, qg, kg, conv_b_w[e],
                state_conv_b[e, :, 0], state_conv_b[e, :, 1])
            attp, atts = _attention(page_table, bias_far, lam_p, sg, qtb, kb, vtb, bias_tiles,
                                    qs, kn, vn, bias_dec_past, bias_dec_new, cache_kt, cache_vt, e,
                                    bp, tp, lam_init)
            mixer = (attp, gcp, atts, gcs, w_out, e)

            ks_l.append(kn)
            vs_l.append(vn)
            cbp_l.append(cbp)
            cbs_l.append(jnp.stack([state_conv_b[e, :, 1], us], axis=1))
        else:
            o = li // 2
            prm = (norm_mix[li], w1, w2, o, b_pw1[o], conv_c_w[o], conv_c_b[o], ln_c_gain[o], ln_c_bias[o], b_pw2[o])
            xp, ccp = _odd_prompt(xp, *prm, bp, tp)
            xs, ccs = _odd_sample(xs, *prm, state_c_t[o])
            ccp_l.append(ccp)
            ccs_l.append(ccs)
            mixer = ()
        xp, xs, ffn_w = _ffn(xp, xs, norm_ffn2[li], ffn_w, mixer=mixer,
                             cast=(*ffn1_f32, li + 1) if li + 1 < depth else ())

    n_h = d_a // DH_V
    kt_all, vt_all = kv_prompt
    new_k_prompt = jnp.transpose(kt_all.reshape(n_even, bp, n_sub, DH_QK, tp), (0, 1, 4, 2, 3))
    new_v_prompt = jnp.transpose(vt_all.reshape(n_even, bp, n_h, DH_V, tp), (0, 1, 4, 2, 3))
    return (xp.reshape(bp, tp, d), xs.reshape(bs, 1, d),
            new_k_prompt, new_v_prompt,
            jnp.stack(ks_l).reshape(n_even, bs, 1, n_sub, DH_QK),
            jnp.stack(vs_l).reshape(n_even, bs, 1, n_h, DH_V),
            jnp.stack(cbp_l), jnp.stack(cbs_l), jnp.stack(ccp_l),
            jnp.transpose(jnp.stack(ccs_l), (0, 2, 1, 3)))
```

```python
import functools
import math

import jax
import jax.numpy as jnp
from jax import lax
from jax.experimental import pallas as pl
from jax.experimental.pallas import tpu as pltpu

F32 = jnp.float32
BF16 = jnp.bfloat16
EPS = 1e-6
NEG = -1e30

LANE = 128
SUBLANE = 8
MXU_DEPTH = 256
VMEM_LIMIT_BYTES = 56 * 1024 * 1024

DH_QK = 32
DH_V = 64
N_BUCKETS = 32
MAX_EXACT = N_BUCKETS // 2
MAX_DISTANCE = 128
FAR = MAX_DISTANCE

TM_ROWS = 512
TM_FFN = 512
FFN_CHUNKS = 2
CAST_STEPS = 16
TM_ODD = 512
TQ = 256
SOFTMAX_SUM_ROWS = 16
HALO = 32
CONV_UNROLL = 16
LOG2E = math.log2(math.e)


def _params(sem):
    return pltpu.CompilerParams(dimension_semantics=sem, vmem_limit_bytes=VMEM_LIMIT_BYTES)


def _resident(shape):
    n = len(shape)
    return pl.BlockSpec(shape, lambda *_: (0,) * n, pipeline_mode=pl.Buffered(1))


def _resident_layer(stacked, layer):
    n = stacked.ndim - 1
    return pl.BlockSpec((None,) + stacked.shape[1:], lambda *_: (layer,) + (0,) * n,
                        pipeline_mode=pl.Buffered(1))


def _rms(x, g):
    return x * lax.rsqrt(jnp.mean(x * x, axis=-1, keepdims=True) + EPS) * g


def _dot(a, b):
    return jnp.dot(a, b, preferred_element_type=F32)


def _group_rms(z, p_ref, gain):
    sq = z * z
    hi = sq.astype(BF16)
    lo = (sq - hi.astype(F32)).astype(BF16)
    ms = _dot(hi, p_ref[...]) + _dot(lo, p_ref[...])
    return z * lax.rsqrt(ms + EPS) * gain


def _swiglu_rows(x, mix, g_ref, wg_ref, wu_ref, wd_ref, wo_ref, n_chunks):
    if mix:
        att, gc = mix
        d_a = att.shape[1]
        x = x + _dot(att.astype(BF16), wo_ref[0:d_a, :]) + _dot(gc, wo_ref[d_a:, :])
    h = _rms(x, g_ref[...]).astype(BF16)
    n_pass = wg_ref.shape[1] // MXU_DEPTH
    cuts = [MXU_DEPTH * ((n_pass * c + n_chunks - 1) // n_chunks) for c in range(n_chunks)] + [wg_ref.shape[1]]
    y = jnp.zeros(x.shape, F32)
    for lo, hi in zip(cuts[:-1], cuts[1:]):
        gate = _dot(h, wg_ref[:, lo:hi])
        up = _dot(h, wu_ref[:, lo:hi])
        a = (jax.nn.silu(gate) * up).astype(BF16)
        y = y + _dot(a, wd_ref[lo:hi, :])
    return x + 0.5 * y


def _ffn_kernel(x_ref, g_ref, wg_ref, wu_ref, wd_ref, *rest, n_chunks, mixer, n_cast):
    rest = list(rest)
    take = lambda n: [rest.pop(0) for _ in range(n)]
    a_ref, gc_ref, wo_ref = take(3) if mixer else (None, None, None)
    xs_ref, = take(1)
    as_ref, gcs_ref = take(2) if mixer else (None, None)
    cast_src = take(n_cast)
    o_ref, os_ref = take(2)
    cast_dst = take(n_cast)
    step = pl.program_id(0)
    if n_cast:
        @pl.when(step < CAST_STEPS)
        def _():
            for src, dst in zip(cast_src, cast_dst):
                dst[...] = src[...].astype(BF16)

    @pl.when(step == 0)
    def _():
        mix_s = (as_ref[...], gcs_ref[...]) if mixer else ()
        os_ref[...] = _swiglu_rows(xs_ref[...], mix_s, g_ref, wg_ref, wu_ref, wd_ref, wo_ref, n_chunks)

    mix = (a_ref[...], gc_ref[...]) if mixer else ()
    o_ref[...] = _swiglu_rows(x_ref[...], mix, g_ref, wg_ref, wu_ref, wd_ref, wo_ref, n_chunks)


def _ffn(x, xs, g, weights, mixer=(), cast=()):
    m, d = x.shape
    tm = TM_FFN
    steps = m // tm
    row = lambda width: pl.BlockSpec((tm, width), lambda i: (i, 0))
    if mixer:
        att, gc, att_s, gc_s, w_out, out_layer = mixer
        mixer_ops = [att, gc, w_out]
        mixer_specs = [row(att.shape[1]), row(gc.shape[1]), _resident_layer(w_out, out_layer)]
        side_ops = [xs, att_s, gc_s]
    else:
        mixer_ops, mixer_specs, side_ops = [], [], [xs]
    side_specs = [_resident(a.shape) for a in side_ops]
    *stacks, cast_layer = cast if cast else (None,)
    assert not stacks or steps >= CAST_STEPS
    slab = lambda i: jnp.minimum(i, CAST_STEPS - 1)
    cast_in = [pl.BlockSpec((None, w.shape[1] // CAST_STEPS, w.shape[2]), lambda i: (cast_layer, slab(i), 0))
               for w in stacks]
    cast_out = [pl.BlockSpec((w.shape[1] // CAST_STEPS, w.shape[2]), lambda i: (slab(i), 0)) for w in stacks]
    out = pl.pallas_call(
        functools.partial(_ffn_kernel, n_chunks=FFN_CHUNKS, mixer=bool(mixer), n_cast=len(stacks)),
        grid=(steps,),
        in_specs=[row(d), _resident((1, d))] + [_resident(w.shape) for w in weights]
                 + mixer_specs + side_specs + cast_in,
        out_specs=[row(d), pl.BlockSpec(xs.shape, lambda i: (0, 0))] + cast_out,
        out_shape=[jax.ShapeDtypeStruct((m, d), F32), jax.ShapeDtypeStruct(xs.shape, F32)]
                  + [jax.ShapeDtypeStruct(w.shape[1:], BF16) for w in stacks],
        compiler_params=_params(("arbitrary",)),
        name="ffn_mixer" if mixer else "ffn",
    )(x, g.reshape(1, d), *weights, *mixer_ops, *side_ops, *stacks)
    return out[0], out[1], tuple(out[2:])


def _even_in_prompt_kernel(x_ref, g_ref, w_ref, p_ref, qg_ref, kg_ref, cw_ref, *rest,
                           tiles_per_seq, d_qk, d_a, d_b, scale, layer):
    kt_ref, vt_ref, qtb_ref, kb_ref, vtb_ref, gc_ref, st_ref, u_scr = rest[-8:]
    i = pl.program_id(0)
    tm = x_ref.shape[0]
    h = _rms(x_ref[...], g_ref[...]).astype(BF16)
    z = _dot(h, w_ref[...])
    q = _group_rms(z[:, :d_qk], p_ref, qg_ref[...]) * scale
    k = _group_rms(z[:, d_qk:2 * d_qk], p_ref, kg_ref[...])
    o = 2 * d_qk
    v = z[:, o:o + d_a]
    o += d_a
    gate_b = z[:, o:o + d_b]
    u = z[:, o + d_b:o + 2 * d_b] * z[:, o + 2 * d_b:o + 3 * d_b]

    kt = k.T
    vt = v.T
    for ref, val in ((kt_ref, kt), (vt_ref, vt)):
        if len(ref.shape) == 2:
            ref[...] = val
        else:
            for slot in range(ref.shape[0]):
                ref[slot] = val if slot == layer else jnp.zeros_like(val)
    qtb_ref[...] = q.T.astype(BF16)
    kb_ref[...] = k.astype(BF16)
    vtb_ref[...] = vt.astype(BF16)

    @pl.when(i % tiles_per_seq == 0)
    def _():
        u_scr[0:SUBLANE, :] = jnp.zeros((SUBLANE, d_b), F32)

    u_scr[SUBLANE:SUBLANE + tm, :] = u
    conv = (cw_ref[0:1, :] * u_scr[SUBLANE - 2:SUBLANE - 2 + tm, :]
            + cw_ref[1:2, :] * u_scr[SUBLANE - 1:SUBLANE - 1 + tm, :]
            + cw_ref[2:3, :] * u)
    gc_ref[...] = (gate_b * conv).astype(BF16)
    st_ref[...] = u_scr[tm + SUBLANE - 2:tm + SUBLANE, :]
    u_scr[0:SUBLANE, :] = u_scr[tm:tm + SUBLANE, :]


def _even_in_prompt(x, g, w_in, layer, n_layers, kv_prev, p_avg, qg, kg, cw, n_seq, seq):
    m, d = x.shape
    d_qk = d_a = d_b = p_avg.shape[0]
    tm = TM_ROWS
    tps = seq // tm
    row = lambda width, dt=None: pl.BlockSpec((tm, width), lambda i: (i, 0))
    col = pl.BlockSpec((d_qk, tm), lambda i: (0, i))
    if kv_prev:
        seq_t = pl.BlockSpec((None, None, d_qk, tm), lambda i: (layer, i // tps, 0, i % tps))
    else:
        seq_t = pl.BlockSpec((n_layers, None, d_qk, tm), lambda i: (0, i // tps, 0, i % tps))
    n_in = 7
    kern = functools.partial(_even_in_prompt_kernel, tiles_per_seq=tps, d_qk=d_qk, d_a=d_a, d_b=d_b, layer=layer,
                             scale=DH_QK ** -0.5 * LOG2E)
    return pl.pallas_call(
        kern,
        grid=(m // tm,),
        in_specs=[row(d), _resident((1, d)), _resident_layer(w_in, layer), _resident(p_avg.shape),
                  _resident((1, d_qk)), _resident((1, d_qk)), _resident(cw.shape)]
                 + [pl.BlockSpec(memory_space=pl.ANY)] * len(kv_prev),
        out_specs=[seq_t, seq_t, col, row(d_qk), col, row(d_b),
                   pl.BlockSpec((None, 2, d_b), lambda i: (i // tps, 0, 0))],
        out_shape=[jax.ShapeDtypeStruct((n_layers, n_seq, d_qk, seq), F32),
                   jax.ShapeDtypeStruct((n_layers, n_seq, d_a, seq), F32),
                   jax.ShapeDtypeStruct((d_qk, m), BF16),
                   jax.ShapeDtypeStruct((m, d_qk), BF16),
                   jax.ShapeDtypeStruct((d_a, m), BF16),
                   jax.ShapeDtypeStruct((m, d_b), BF16),
                   jax.ShapeDtypeStruct((n_seq, 2, d_b), F32)],
        scratch_shapes=[pltpu.VMEM((tm + SUBLANE, d_b), F32)],
        input_output_aliases={n_in + j: j for j in range(len(kv_prev))},
        compiler_params=_params(("arbitrary",)),
        name="even_in_prompt",
    )(x, g.reshape(1, d), w_in, p_avg, qg, kg, cw, *kv_prev)


def _even_in_sample_kernel(x_ref, g_ref, w_ref, p_ref, qg_ref, kg_ref, cw_ref, s0_ref, s1_ref,
                           q_ref, k_ref, v_ref, gc_ref, u_ref, *, d_qk, d_a, d_b, scale):
    h = _rms(x_ref[...], g_ref[...]).astype(BF16)
    z = _dot(h, w_ref[...])
    q_ref[...] = _group_rms(z[:, :d_qk], p_ref, qg_ref[...]) * scale
    k_ref[...] = _group_rms(z[:, d_qk:2 * d_qk], p_ref, kg_ref[...])
    o = 2 * d_qk
    v_ref[...] = z[:, o:o + d_a]
    o += d_a
    gate_b = z[:, o:o + d_b]
    u = z[:, o + d_b:o + 2 * d_b] * z[:, o + 2 * d_b:o + 3 * d_b]
    conv = cw_ref[0:1, :] * s0_ref[...] + cw_ref[1:2, :] * s1_ref[...] + cw_ref[2:3, :] * u
    gc_ref[...] = (gate_b * conv).astype(BF16)
    u_ref[...] = u


def _even_in_sample(x, g, w_in, layer, p_avg, qg, kg, cw, s0, s1):
    m, d = x.shape
    d_qk = d_a = d_b = p_avg.shape[0]
    kern = functools.partial(_even_in_sample_kernel, d_qk=d_qk, d_a=d_a, d_b=d_b, scale=DH_QK ** -0.5)
    f = lambda dt: jax.ShapeDtypeStruct((m, d_qk), dt)
    blk = _resident((m, d_qk))
    return pl.pallas_call(
        kern,
        grid=(1,),
        in_specs=[_resident((m, d)), _resident((1, d)), _resident_layer(w_in, layer), _resident(p_avg.shape),
                  _resident((1, d_qk)), _resident((1, d_qk)), _resident(cw.shape), blk, blk],
        out_specs=[pl.BlockSpec((m, d_qk), lambda i: (0, 0))] * 5,
        out_shape=[f(F32), f(F32), f(F32), f(BF16), f(F32)],
        compiler_params=_params(("arbitrary",)),
        name="even_in_sample",
    )(x, g.reshape(1, d), w_in, p_avg, qg, kg, cw, s0, s1)


def _lambda(lam_ref, lam_init):
    lam_p = lam_ref[...]
    return (jnp.exp(jnp.sum(lam_p[0:1] * lam_p[1:2], axis=1, keepdims=True))
            - jnp.exp(jnp.sum(lam_p[2:3] * lam_p[3:4], axis=1, keepdims=True)) + lam_init)


def _prompt_tiles(tiles, hp, lam, cfar_ref, sg_ref, qt_ref, k_ref, vt_ref, bias_ref, o_ref, p_scr, *,
                  lam_init, tq):
    n_sub = qt_ref.shape[0] // DH_QK
    n_head = n_sub // 2
    cfar = jnp.concatenate([jnp.full((1, 2 * tq), cfar_ref[hp * n_head + h], F32) for h in range(n_head)],
                           axis=1)
    grp = lax.broadcasted_iota(jnp.int32, (qt_ref.shape[0], tq), 0) // DH_QK

    def scores(n):
        qt = qt_ref[:, n * tq:(n + 1) * tq]
        q_blk = jnp.concatenate([jnp.where(grp == c, qt, jnp.zeros_like(qt)) for c in range(n_sub)], axis=1)
        return _dot(k_ref[0:(n + 1) * tq, :], q_blk)

    s_next = scores(tiles[0])
    for idx, n in enumerate(tiles):
        s_all = s_next
        if idx + 1 < len(tiles):
            s_next = scores(tiles[idx + 1])
        p_buf = p_scr.at[idx % 2]

        m = None
        s_tiles = []
        for t in range(n + 1):
            s = s_all[t * tq:(t + 1) * tq, :]
            if t >= n - 1:
                kind = 1 if t == n else 0
                s = jnp.concatenate([s[:, c * tq:(c + 1) * tq] + bias_ref[c // 2, kind] for c in range(n_sub)],
                                    axis=1)
                smax = jnp.max(s, axis=0, keepdims=True)
            else:
                smax = jnp.max(s, axis=0, keepdims=True) + cfar
            m = smax if m is None else jnp.maximum(m, smax)
            s_tiles.append(s)

        m_far = m - cfar
        for t in range(n + 1):
            shift = m if t >= n - 1 else m_far
            p_buf[t * tq:(t + 1) * tq, :] = jnp.exp2(s_tiles[t] - shift).astype(BF16)

        keys = (n + 1) * tq
        ones = jnp.ones((SOFTMAX_SUM_ROWS, keys), BF16)
        outs = []
        for h in range(n_head):
            vte = jnp.concatenate([vt_ref[h * DH_V:(h + 1) * DH_V, 0:keys], ones], axis=0)
            pv = _dot(vte, p_buf[0:keys, h * 2 * tq:(h + 1) * 2 * tq])
            sub = [pv[0:DH_V, c * tq:(c + 1) * tq] / pv[DH_V:DH_V + 1, c * tq:(c + 1) * tq] for c in range(2)]
            o = sub[0] - lam * sub[1]
            ms = jnp.mean(o * o, axis=0, keepdims=True)
            outs.append(o * lax.rsqrt(ms + EPS))
        o = jnp.concatenate(outs, axis=0) * sg_ref[...] * (1.0 - lam_init)
        o_ref[n * tq:(n + 1) * tq, :] = o.T.astype(BF16)


def _decode_pages(first, last, lam, q_ref, kn_ref, vn_ref, bias, b0_ref, sg_ref, k_refs, v_refs, o_ref,
                  m_scr, l_scr, acc_scr, *, lam_init):
    pp = len(k_refs)
    n_sub, d = acc_scr.shape
    page = k_refs[0].shape[1]

    row = lax.broadcasted_iota(jnp.int32, (n_sub, d), 0)
    col = lax.broadcasted_iota(jnp.int32, (n_sub, d), 1)
    qexp = jnp.where(col // DH_QK == row, jnp.broadcast_to(q_ref[...], (n_sub, d)), 0.0)

    if first:
        m_scr[...] = jnp.sum(qexp * kn_ref[...], axis=1, keepdims=True) + b0_ref[...]
        l_scr[...] = jnp.ones(l_scr.shape, F32)
        acc_scr[...] = jnp.broadcast_to(vn_ref[...], (n_sub, d))

    qb = qexp.astype(BF16)
    s = jnp.concatenate([_dot(qb, k_refs[i][...].astype(BF16)) for i in range(pp)], axis=1)
    s = s + bias
    m_old = m_scr[...]
    m_new = jnp.maximum(m_old, jnp.max(s, axis=1, keepdims=True))
    p = jnp.exp(s - m_new)
    alpha = jnp.exp(m_old - m_new)
    l_scr[...] = alpha * l_scr[...] + jnp.sum(p, axis=1, keepdims=True)
    pb = p.astype(BF16)
    pv = jnp.zeros((n_sub, d), F32)
    for i in range(pp):
        pv = pv + lax.dot_general(pb[:, i * page:(i + 1) * page], v_refs[i][...].astype(BF16),
                                  (((1,), (1,)), ((), ())), preferred_element_type=F32)
    acc_scr[...] = alpha * acc_scr[...] + pv
    m_scr[...] = m_new

    if last:
        o = acc_scr[...] / l_scr[...]
        head = col // DH_V
        a0 = jnp.sum(jnp.where(row == 2 * head, o, 0.0), axis=0, keepdims=True)
        a1 = jnp.sum(jnp.where(row == 2 * head + 1, o, 0.0), axis=0, keepdims=True)
        att = a0 - lam * a1
        sq = att * att
        head1 = head[0:1]
        r = jnp.zeros_like(att)
        for hh in range(d // DH_V):
            ms = jnp.sum(jnp.where(head1 == hh, sq, 0.0), axis=1, keepdims=True) * (1.0 / DH_V)
            r = jnp.where(head1 == hh, lax.rsqrt(ms + EPS), r)
        o_ref[...] = att * r * sg_ref[...] * (1.0 - lam_init)


def _attention_kernel(pt_ref, cfar_ref, lam_ref, sgc_ref, qt_ref, k_ref, vt_ref, bias_ref,
                      q_ref, kn_ref, vn_ref, bt_ref, b0_ref, sgr_ref, *rest, pp, tiles, lam_init, tq, layer):
    ck_hbm, cv_hbm, o_ref, od_ref, p_scr, m_scr, l_scr, acc_scr, kbuf, vbuf, sem = rest
    hp = pl.program_id(1)
    half = pl.program_id(2)
    step = (pl.program_id(0) * pl.num_programs(1) + hp) * pl.num_programs(2) + half
    last_step = pl.num_programs(0) * pl.num_programs(1) * pl.num_programs(2) - 1
    slot = step % 2

    def page_copies(of_step, into):
        copies = []
        for i in range(pp):
            pid = pt_ref[of_step * pp + i]
            copies.append(pltpu.make_async_copy(ck_hbm.at[layer, pid], kbuf.at[into, i], sem.at[into, 0]))
            copies.append(pltpu.make_async_copy(cv_hbm.at[layer, pid], vbuf.at[into, i], sem.at[into, 1]))
        return copies

    @pl.when(step == 0)
    def _():
        for c in page_copies(0, 0):
            c.start()

    lam = _lambda(lam_ref, lam_init)
    nxt = jnp.minimum(step + 1, last_step)
    for hv in range(2):
        @pl.when(half == hv)
        def _(hv=hv):
            for c in page_copies(nxt, 1 - slot):
                c.start()
            _prompt_tiles(tiles[hv], hp, lam, cfar_ref, sgc_ref, qt_ref, k_ref, vt_ref, bias_ref, o_ref, p_scr,
                          lam_init=lam_init, tq=tq)
            for c in page_copies(step, slot):
                c.wait()
            k_refs = [kbuf.at[slot, i] for i in range(pp)]
            v_refs = [vbuf.at[slot, i] for i in range(pp)]
            _decode_pages(hv == 0, hv == 1, lam, q_ref, kn_ref, vn_ref, bt_ref[...], b0_ref, sgr_ref,
                          k_refs, v_refs, od_ref, m_scr, l_scr, acc_scr, lam_init=lam_init)

    @pl.when(step == last_step)
    def _():
        for c in page_copies(last_step, 1 - slot):
            c.wait()


def _attention(page_table, cfar, lam_p, sg, qt, k, vt, bias_tiles, q, kn, vn, bias_t, bias0,
               cache_kt, cache_vt, layer, n_seq, seq, lam_init):
    d_qk, m = qt.shape
    n_dec, n_pages = page_table.shape
    d = q.shape[-1]
    page = cache_kt.shape[-1]
    pair = 2 * 2 * DH_QK
    n_pair = d_qk // pair
    nq = seq // TQ
    order = sorted(range(nq), reverse=True)
    tiles = ([n for i, n in enumerate(order) if i % 4 in (0, 3)], [n for i, n in enumerate(order) if i % 4 in (1, 2)])
    halves = 2
    pp = n_pages // halves
    assert n_pages % halves == 0 and n_seq * n_pair == n_dec, "one sample sequence per (prompt sequence, head pair)"
    kern = functools.partial(_attention_kernel, pp=pp, tiles=tiles, lam_init=lam_init, tq=TQ, layer=layer)
    const = lambda shape: pl.BlockSpec(shape, lambda b, h, s, pt: (0,) * len(shape))
    tok = pl.BlockSpec((None, 1, d), lambda b, h, s, pt: (b * n_pair + h, 0, 0))
    grid_spec = pltpu.PrefetchScalarGridSpec(
        num_scalar_prefetch=1,
        grid=(n_seq, n_pair, halves),
        in_specs=[pl.BlockSpec(memory_space=pltpu.SMEM),
                  const(lam_p.shape),
                  const((pair, 1)),
                  pl.BlockSpec((pair, seq), lambda b, h, s, pt: (h, b)),
                  pl.BlockSpec((seq, pair), lambda b, h, s, pt: (b, h)),
                  pl.BlockSpec((pair, seq), lambda b, h, s, pt: (h, b)),
                  pl.BlockSpec((2, 2, TQ, TQ), lambda b, h, s, pt: (h, 0, 0, 0)),
                  tok, tok, tok,
                  pl.BlockSpec((d // DH_QK, pp * page), lambda b, h, s, pt: (0, s)),
                  const((d // DH_QK, 1)),
                  const((1, d)),
                  pl.BlockSpec(memory_space=pl.ANY), pl.BlockSpec(memory_space=pl.ANY)],
        out_specs=[pl.BlockSpec((seq, pair), lambda b, h, s, pt: (b, h)), tok],
        scratch_shapes=[pltpu.VMEM((2, seq, (pair // DH_QK) * TQ), BF16),
                        pltpu.VMEM((d // DH_QK, 1), F32), pltpu.VMEM((d // DH_QK, 1), F32),
                        pltpu.VMEM((d // DH_QK, d), F32),
                        pltpu.VMEM((2, pp, d, page), F32), pltpu.VMEM((2, pp, d, page), F32),
                        pltpu.SemaphoreType.DMA((2, 2))],
    )
    att_p, att_s = pl.pallas_call(
        kern,
        grid_spec=grid_spec,
        out_shape=[jax.ShapeDtypeStruct((m, d_qk), BF16), jax.ShapeDtypeStruct((n_dec, 1, d), F32)],
        compiler_params=_params(("arbitrary", "arbitrary", "arbitrary")),
        name="attention",
    )(page_table.reshape(-1), cfar, lam_p, sg[:pair].reshape(-1, 1), qt, k, vt, bias_tiles,
      q.reshape(n_dec, 1, d), kn.reshape(n_dec, 1, d), vn.reshape(n_dec, 1, d), bias_t, bias0, sg.reshape(1, -1),
      cache_kt, cache_vt)
    return att_p, att_s.reshape(n_dec, d)


def _layernorm_swish(c, g, b):
    mu = jnp.mean(c, axis=-1, keepdims=True)
    xc = c - mu
    y = xc * lax.rsqrt(jnp.mean(xc * xc, axis=-1, keepdims=True) + EPS) * g + b
    return jax.nn.silu(y)


def _odd_prompt_kernel(x_ref, g_ref, w1_ref, b1_ref, cw_ref, cb_ref, lg_ref, lb_ref, w2_ref, b2_ref,
                       o_ref, st_ref, u_scr, c_scr, *, tiles_per_seq, width):
    i = pl.program_id(0)
    tm, d = x_ref.shape
    nl = d // LANE
    x = x_ref[...]
    h = _rms(x, g_ref[...]).astype(BF16)
    z = _dot(h, w1_ref[...]) + b1_ref[...]
    u = z[:, :d] * jax.nn.sigmoid(z[:, d:])

    @pl.when(i % tiles_per_seq == 0)
    def _():
        u_scr[0:HALO * nl, :] = jnp.zeros((HALO * nl, LANE), F32)

    for l in range(nl):
        u_scr[pl.ds(HALO * nl + l, tm, stride=nl), :] = u[:, l * LANE:(l + 1) * LANE]

    @pl.when(i % tiles_per_seq == tiles_per_seq - 1)
    def _():
        st_ref[...] = u[tm - (width - 1):, :]

    first = HALO - (width - 1)
    taps = [cw_ref[j] for j in range(width)]

    def chunk(r, carry):
        t0 = r * CONV_UNROLL
        slab = lambda t: pl.ds(pl.multiple_of(t * nl, nl), nl)
        acc = [None] * CONV_UNROLL
        for k in range(CONV_UNROLL + width - 1):
            row = u_scr[slab(t0 + first + k), :]
            for o in range(max(0, k - width + 1), min(CONV_UNROLL, k + 1)):
                term = taps[k - o] * row
                acc[o] = term if acc[o] is None else acc[o] + term
        for o in range(CONV_UNROLL):
            c_scr[slab(t0 + o), :] = acc[o]
        return carry

    lax.fori_loop(0, tm // CONV_UNROLL, chunk, 0)

    u_scr[0:HALO * nl, :] = u_scr[tm * nl:(tm + HALO) * nl, :]
    c = jnp.concatenate([c_scr[pl.ds(l, tm, stride=nl), :] for l in range(nl)], axis=1)
    a = _layernorm_swish(c + cb_ref[...], lg_ref[...], lb_ref[...]).astype(BF16)
    o_ref[...] = x + _dot(a, w2_ref[...]) + b2_ref[...]


def _odd_prompt(x, g, w1, w2, layer, b1, cw, cb, lg, lb, b2, n_seq, seq):
    m, d = x.shape
    width = cw.shape[0]
    tm = TM_ODD
    tps = seq // tm
    row = pl.BlockSpec((tm, d), lambda i: (i, 0))
    vec = lambda a: a.reshape(1, -1)
    nl = d // LANE
    assert tm >= width - 1 and tm % CONV_UNROLL == 0 and HALO >= width - 1
    kern = functools.partial(_odd_prompt_kernel, tiles_per_seq=tps, width=width)
    return pl.pallas_call(
        kern,
        grid=(m // tm,),
        in_specs=[row, _resident((1, d)), _resident_layer(w1, layer), _resident((1, 2 * d)),
                  _resident((width, nl, LANE)),
                  _resident((1, d)), _resident((1, d)), _resident((1, d)), _resident_layer(w2, layer),
                  _resident((1, d))],
        out_specs=[row, pl.BlockSpec((None, width - 1, d), lambda i: (i // tps, 0, 0))],
        out_shape=[jax.ShapeDtypeStruct((m, d), F32),
                   jax.ShapeDtypeStruct((n_seq, width - 1, d), F32)],
        scratch_shapes=[pltpu.VMEM(((tm + HALO) * nl, LANE), F32), pltpu.VMEM((tm * nl, LANE), F32)],
        compiler_params=_params(("arbitrary",)),
        name="odd_prompt",
    )(x, vec(g), w1, vec(b1), cw.reshape(width, nl, LANE), vec(cb), vec(lg), vec(lb), w2, vec(b2))


def _odd_sample_kernel(x_ref, g_ref, w1_ref, b1_ref, cw_ref, cb_ref, lg_ref, lb_ref, w2_ref, b2_ref,
                       st_ref, o_ref, nst_ref):
    n_hist = st_ref.shape[0]
    d = x_ref.shape[1]
    x = x_ref[...]
    h = _rms(x, g_ref[...]).astype(BF16)
    z = _dot(h, w1_ref[...]) + b1_ref[...]
    u = z[:, :d] * jax.nn.sigmoid(z[:, d:])
    c = cw_ref[n_hist:n_hist + 1, :] * u
    for j in range(n_hist):
        c = c + cw_ref[j:j + 1, :] * st_ref[j]
    for j in range(n_hist - 1):
        nst_ref[j] = st_ref[j + 1]
    nst_ref[n_hist - 1] = u
    a = _layernorm_swish(c + cb_ref[...], lg_ref[...], lb_ref[...]).astype(BF16)
    o_ref[...] = x + _dot(a, w2_ref[...]) + b2_ref[...]


def _odd_sample(x, g, w1, w2, layer, b1, cw, cb, lg, lb, b2, state_t):
    m, d = x.shape
    vec = lambda a: a.reshape(1, -1)
    whole = lambda shape: pl.BlockSpec(shape, lambda i: (0,) * len(shape))
    return pl.pallas_call(
        _odd_sample_kernel,
        grid=(1,),
        in_specs=[_resident((m, d)), _resident((1, d)), _resident_layer(w1, layer), _resident((1, 2 * d)),
                  _resident(cw.shape), _resident((1, d)), _resident((1, d)), _resident((1, d)),
                  _resident_layer(w2, layer), _resident((1, d)), _resident(state_t.shape)],
        out_specs=[whole((m, d)), whole(state_t.shape)],
        out_shape=[jax.ShapeDtypeStruct((m, d), F32), jax.ShapeDtypeStruct(state_t.shape, F32)],
        compiler_params=_params(("arbitrary",)),
        name="odd_sample",
    )(x, vec(g), w1, vec(b1), cw, vec(cb), vec(lg), vec(lb), w2, vec(b2), state_t)


def _bias_by_distance(rel_bias, dist):
    n = jnp.maximum(dist, 0)
    nf = jnp.maximum(n, 1).astype(F32)
    large = MAX_EXACT + (jnp.log(nf / MAX_EXACT) / math.log(MAX_DISTANCE / MAX_EXACT)
                         * (N_BUCKETS - MAX_EXACT)).astype(jnp.int32)
    large = jnp.minimum(large, N_BUCKETS - 1)
    return rel_bias[jnp.where(n < MAX_EXACT, n, large)]


def _prompt_bias_tiles(by_dist_nat):
    assert TQ >= FAR - 1
    n_head = by_dist_nat.shape[0]
    by_dist = by_dist_nat * LOG2E
    pad = jnp.zeros((n_head, 1), F32)
    diag = jnp.concatenate([jnp.full((n_head, TQ - 1), NEG, F32), by_dist[:, :TQ], pad], axis=1)
    near = jnp.concatenate([by_dist[:, 1:], pad], axis=1)

    def toeplitz(a):
        period = a.shape[1]
        rows = jnp.tile(a, (1, TQ))[:, :TQ * (period - 1)].reshape(n_head, TQ, period - 1)
        return rows[:, :, TQ - 1:]

    tiles = jnp.stack([toeplitz(near), toeplitz(diag)], axis=1)
    return tiles, by_dist[:, TQ + 1]


def _decode_bias(by_dist_nat, past_len):
    n_head, n_tab = by_dist_nat.shape
    assert n_tab > FAR and past_len + 1 >= n_tab
    far = jnp.broadcast_to(by_dist_nat[:, FAR:FAR + 1], (n_head, past_len + 1 - n_tab))
    return jnp.repeat(jnp.concatenate([far, by_dist_nat[:, ::-1]], axis=1), 2, axis=0)


def kernel(x_prompt, x_sample, cache_k, cache_v, state_conv_b, state_conv_c, page_table, rel_bias, norm_ffn1, ffn1_w_gate, ffn1_w_up, ffn1_w_down, norm_mix, norm_ffn2, ffn2_w_gate, ffn2_w_up, ffn2_w_down, w_in_even, w_out_even, q_norm_gain, k_norm_gain, lambda_q1, lambda_k1, lambda_q2, lambda_k2, subln_gain, conv_b_w, w_pw1, b_pw1, conv_c_w, conv_c_b, ln_c_gain, ln_c_bias, w_pw2, b_pw2):
    bp, tp, d = x_prompt.shape
    bs = x_sample.shape[0]
    depth = norm_ffn1.shape[0]
    n_even, n_phys, page, n_sub, dh_qk = cache_k.shape
    assert dh_qk == DH_QK and cache_v.shape[-1] == DH_V and x_sample.shape[1] == 1
    d_qk = n_sub * dh_qk
    d_a = cache_v.shape[3] * DH_V
    past_len = page_table.shape[1] * page
    assert tp % TM_ROWS == 0 and tp % TM_ODD == 0 and tp % TQ == 0 and tp >= 2 * TQ

    bf = lambda w: w.astype(BF16)
    xp = x_prompt.reshape(bp * tp, d)
    xs = x_sample.reshape(bs, d)

    cache_kt = jnp.transpose(cache_k, (0, 1, 3, 4, 2)).reshape(n_even, n_phys, d_qk, page)
    cache_vt = jnp.transpose(cache_v, (0, 1, 3, 4, 2)).reshape(n_even, n_phys, d_a, page)
    state_c_t = jnp.transpose(state_conv_c, (0, 2, 1, 3))

    p_avg = jnp.kron(jnp.eye(d_qk // DH_QK, dtype=F32), jnp.full((DH_QK, DH_QK), 1.0 / DH_QK, F32)).astype(BF16)
    by_dist_nat = _bias_by_distance(rel_bias, jnp.arange(2 * TQ)).T
    bias_tiles, bias_far = _prompt_bias_tiles(by_dist_nat)
    bias_dec = _decode_bias(by_dist_nat, past_len)
    bias_dec_past, bias_dec_new = bias_dec[:, :past_len], bias_dec[:, past_len:]

    w_in, w_out, w1, w2 = (w.astype(BF16) for w in (w_in_even, w_out_even, w_pw1, w_pw2))
    ffn1_f32 = (ffn1_w_gate, ffn1_w_up, ffn1_w_down)
    ffn2_f32 = (ffn2_w_gate, ffn2_w_up, ffn2_w_down)
    ffn_w = tuple(w[0].astype(BF16) for w in ffn1_f32)

    kv_prompt = ()
    ks_l, vs_l, cbp_l, cbs_l, ccp_l, ccs_l = [], [], [], [], [], []
    for li in range(depth):
        xp, xs, ffn_w = _ffn(xp, xs, norm_ffn1[li], ffn_w, cast=(*ffn2_f32, li))
        if li % 2 == 0:
            e = li // 2
            lam_init = 0.8 - 0.6 * math.exp(-0.3 * li)
            qg = jnp.tile(q_norm_gain[e], d_qk // DH_QK).reshape(1, d_qk)
            kg = jnp.tile(k_norm_gain[e], d_qk // DH_QK).reshape(1, d_qk)
            lam_p = jnp.stack([lambda_q1[e], lambda_k1[e], lambda_q2[e], lambda_k2[e]])
            sg = jnp.tile(subln_gain[e], d_a // DH_V)

            *kv_prompt, qtb, kb, vtb, gcp, cbp = _even_in_prompt(
                xp, norm_mix[li], w_in, e, n_even, tuple(kv_prompt), p_avg, qg, kg, conv_b_w[e], bp, tp)
            qs, kn, vn, gcs, us = _even_in_sample(
                xs, norm_mix[li], w_in, e, p_avg, qg, kg, conv_b_w[e],
                state_conv_b[e, :, 0], state_conv_b[e, :, 1])
            attp, atts = _attention(page_table, bias_far, lam_p, sg, qtb, kb, vtb, bias_tiles,
                                    qs, kn, vn, bias_dec_past, bias_dec_new, cache_kt, cache_vt, e,
                                    bp, tp, lam_init)
            mixer = (attp, gcp, atts, gcs, w_out, e)

            ks_l.append(kn)
            vs_l.append(vn)
            cbp_l.append(cbp)
            cbs_l.append(jnp.stack([state_conv_b[e, :, 1], us], axis=1))
        else:
            o = li // 2
            prm = (norm_mix[li], w1, w2, o, b_pw1[o], conv_c_w[o], conv_c_b[o], ln_c_gain[o], ln_c_bias[o], b_pw2[o])
            xp, ccp = _odd_prompt(xp, *prm, bp, tp)
            xs, ccs = _odd_sample(xs, *prm, state_c_t[o])
            ccp_l.append(ccp)
            ccs_l.append(ccs)
            mixer = ()
        xp, xs, ffn_w = _ffn(xp, xs, norm_ffn2[li], ffn_w, mixer=mixer,
                             cast=(*ffn1_f32, li + 1) if li + 1 < depth else ())

    n_h = d_a // DH_V
    kt_all, vt_all = kv_prompt
    new_k_prompt = jnp.transpose(kt_all.reshape(n_even, bp, n_sub, DH_QK, tp), (0, 1, 4, 2, 3))
    new_v_prompt = jnp.transpose(vt_all.reshape(n_even, bp, n_h, DH_V, tp), (0, 1, 4, 2, 3))
    return (xp.reshape(bp, tp, d), xs.reshape(bs, 1, d),
            new_k_prompt, new_v_prompt,
            jnp.stack(ks_l).reshape(n_even, bs, 1, n_sub, DH_QK),
            jnp.stack(vs_l).reshape(n_even, bs, 1, n_h, DH_V),
            jnp.stack(cbp_l), jnp.stack(cbs_l), jnp.stack(ccp_l),
            jnp.transpose(jnp.stack(ccs_l), (0, 2, 1, 3)))
```

```python
import functools
import math

import jax
import jax.numpy as jnp
from jax import lax
from jax.experimental import pallas as pl
from jax.experimental.pallas import tpu as pltpu

F32 = jnp.float32
BF16 = jnp.bfloat16
EPS = 1e-6
NEG = -1e30

LANE = 128
SUBLANE = 8
MXU_DEPTH = 256
VMEM_LIMIT_BYTES = 56 * 1024 * 1024

DH_QK = 32
DH_V = 64
N_BUCKETS = 32
MAX_EXACT = N_BUCKETS // 2
MAX_DISTANCE = 128
FAR = MAX_DISTANCE

TM_ROWS = 1024
TM_FFN = 512
FFN_CHUNKS = 2
CAST_STEPS = 16
TM_ODD = 1024
TQ = 256
SOFTMAX_SUM_ROWS = 16
HALO = 32
CONV_UNROLL = 16
LOG2E = math.log2(math.e)


def _params(sem):
    return pltpu.CompilerParams(dimension_semantics=sem, vmem_limit_bytes=VMEM_LIMIT_BYTES)


def _resident(shape):
    n = len(shape)
    return pl.BlockSpec(shape, lambda *_: (0,) * n, pipeline_mode=pl.Buffered(1))


def _resident_layer(stacked, layer):
    n = stacked.ndim - 1
    return pl.BlockSpec((None,) + stacked.shape[1:], lambda *_: (layer,) + (0,) * n,
                        pipeline_mode=pl.Buffered(1))


def _rms(x, g):
    return x * lax.rsqrt(jnp.mean(x * x, axis=-1, keepdims=True) + EPS) * g


def _dot(a, b):
    return jnp.dot(a, b, preferred_element_type=F32)


def _group_rms(z, p_ref, gain):
    sq = z * z
    hi = sq.astype(BF16)
    lo = (sq - hi.astype(F32)).astype(BF16)
    ms = _dot(hi, p_ref[...]) + _dot(lo, p_ref[...])
    return z * lax.rsqrt(ms + EPS) * gain


def _swiglu_rows(x, mix, g_ref, wg_ref, wu_ref, wd_ref, wo_ref, n_chunks):
    if mix:
        att, gc = mix
        d_a = att.shape[1]
        x = x + _dot(att.astype(BF16), wo_ref[0:d_a, :]) + _dot(gc, wo_ref[d_a:, :])
    h = _rms(x, g_ref[...]).astype(BF16)
    n_pass = wg_ref.shape[1] // MXU_DEPTH
    cuts = [MXU_DEPTH * ((n_pass * c + n_chunks - 1) // n_chunks) for c in range(n_chunks)] + [wg_ref.shape[1]]
    y = jnp.zeros(x.shape, F32)
    for lo, hi in zip(cuts[:-1], cuts[1:]):
        gate = _dot(h, wg_ref[:, lo:hi])
        up = _dot(h, wu_ref[:, lo:hi])
        a = (jax.nn.silu(gate) * up).astype(BF16)
        y = y + _dot(a, wd_ref[lo:hi, :])
    return x + 0.5 * y


def _ffn_kernel(x_ref, g_ref, wg_ref, wu_ref, wd_ref, *rest, n_chunks, mixer, n_cast):
    rest = list(rest)
    take = lambda n: [rest.pop(0) for _ in range(n)]
    a_ref, gc_ref, wo_ref = take(3) if mixer else (None, None, None)
    xs_ref, = take(1)
    as_ref, gcs_ref = take(2) if mixer else (None, None)
    cast_src = take(n_cast)
    o_ref, os_ref = take(2)
    cast_dst = take(n_cast)
    step = pl.program_id(0)
    if n_cast:
        @pl.when(step < CAST_STEPS)
        def _():
            for src, dst in zip(cast_src, cast_dst):
                dst[...] = src[...].astype(BF16)

    @pl.when(step == 0)
    def _():
        mix_s = (as_ref[...], gcs_ref[...]) if mixer else ()
        os_ref[...] = _swiglu_rows(xs_ref[...], mix_s, g_ref, wg_ref, wu_ref, wd_ref, wo_ref, n_chunks)

    mix = (a_ref[...], gc_ref[...]) if mixer else ()
    o_ref[...] = _swiglu_rows(x_ref[...], mix, g_ref, wg_ref, wu_ref, wd_ref, wo_ref, n_chunks)


def _ffn(x, xs, g, weights, mixer=(), cast=()):
    m, d = x.shape
    tm = TM_FFN
    steps = m // tm
    row = lambda width: pl.BlockSpec((tm, width), lambda i: (i, 0))
    if mixer:
        att, gc, att_s, gc_s, w_out, out_layer = mixer
        mixer_ops = [att, gc, w_out]
        mixer_specs = [row(att.shape[1]), row(gc.shape[1]), _resident_layer(w_out, out_layer)]
        side_ops = [xs, att_s, gc_s]
    else:
        mixer_ops, mixer_specs, side_ops = [], [], [xs]
    side_specs = [_resident(a.shape) for a in side_ops]
    *stacks, cast_layer = cast if cast else (None,)
    assert not stacks or steps >= CAST_STEPS
    slab = lambda i: jnp.minimum(i, CAST_STEPS - 1)
    cast_in = [pl.BlockSpec((None, w.shape[1] // CAST_STEPS, w.shape[2]), lambda i: (cast_layer, slab(i), 0))
               for w in stacks]
    cast_out = [pl.BlockSpec((w.shape[1] // CAST_STEPS, w.shape[2]), lambda i: (slab(i), 0)) for w in stacks]
    out = pl.pallas_call(
        functools.partial(_ffn_kernel, n_chunks=FFN_CHUNKS, mixer=bool(mixer), n_cast=len(stacks)),
        grid=(steps,),
        in_specs=[row(d), _resident((1, d))] + [_resident(w.shape) for w in weights]
                 + mixer_specs + side_specs + cast_in,
        out_specs=[row(d), pl.BlockSpec(xs.shape, lambda i: (0, 0))] + cast_out,
        out_shape=[jax.ShapeDtypeStruct((m, d), F32), jax.ShapeDtypeStruct(xs.shape, F32)]
                  + [jax.ShapeDtypeStruct(w.shape[1:], BF16) for w in stacks],
        compiler_params=_params(("arbitrary",)),
        name="ffn_mixer" if mixer else "ffn",
    )(x, g.reshape(1, d), *weights, *mixer_ops, *side_ops, *stacks)
    return out[0], out[1], tuple(out[2:])


def _even_in_prompt_kernel(x_ref, g_ref, w_ref, p_ref, qg_ref, kg_ref, cw_ref, *rest,
                           tiles_per_seq, d_qk, d_a, d_b, scale, layer):
    kt_ref, vt_ref, qtb_ref, kb_ref, vtb_ref, gc_ref, st_ref, u_scr = rest[-8:]
    i = pl.program_id(0)
    tm = x_ref.shape[0]
    h = _rms(x_ref[...], g_ref[...]).astype(BF16)
    z = _dot(h, w_ref[...])
    q = _group_rms(z[:, :d_qk], p_ref, qg_ref[...]) * scale
    k = _group_rms(z[:, d_qk:2 * d_qk], p_ref, kg_ref[...])
    o = 2 * d_qk
    v = z[:, o:o + d_a]
    o += d_a
    gate_b = z[:, o:o + d_b]
    u = z[:, o + d_b:o + 2 * d_b] * z[:, o + 2 * d_b:o + 3 * d_b]

    kt = k.T
    vt = v.T
    for ref, val in ((kt_ref, kt), (vt_ref, vt)):
        if len(ref.shape) == 2:
            ref[...] = val
        else:
            for slot in range(ref.shape[0]):
                ref[slot] = val if slot == layer else jnp.zeros_like(val)
    qtb_ref[...] = q.T.astype(BF16)
    kb_ref[...] = k.astype(BF16)
    vtb_ref[...] = vt.astype(BF16)

    @pl.when(i % tiles_per_seq == 0)
    def _():
        u_scr[0:SUBLANE, :] = jnp.zeros((SUBLANE, d_b), F32)

    u_scr[SUBLANE:SUBLANE + tm, :] = u
    conv = (cw_ref[0:1, :] * u_scr[SUBLANE - 2:SUBLANE - 2 + tm, :]
            + cw_ref[1:2, :] * u_scr[SUBLANE - 1:SUBLANE - 1 + tm, :]
            + cw_ref[2:3, :] * u)
    gc_ref[...] = (gate_b * conv).astype(BF16)
    st_ref[...] = u_scr[tm + SUBLANE - 2:tm + SUBLANE, :]
    u_scr[0:SUBLANE, :] = u_scr[tm:tm + SUBLANE, :]


def _even_in_prompt(x, g, w_in, layer, n_layers, kv_prev, p_avg, qg, kg, cw, n_seq, seq):
    m, d = x.shape
    d_qk = d_a = d_b = p_avg.shape[0]
    tm = TM_ROWS
    tps = seq // tm
    row = lambda width, dt=None: pl.BlockSpec((tm, width), lambda i: (i, 0))
    col = pl.BlockSpec((d_qk, tm), lambda i: (0, i))
    if kv_prev:
        seq_t = pl.BlockSpec((None, None, d_qk, tm), lambda i: (layer, i // tps, 0, i % tps))
    else:
        seq_t = pl.BlockSpec((n_layers, None, d_qk, tm), lambda i: (0, i // tps, 0, i % tps))
    n_in = 7
    kern = functools.partial(_even_in_prompt_kernel, tiles_per_seq=tps, d_qk=d_qk, d_a=d_a, d_b=d_b, layer=layer,
                             scale=DH_QK ** -0.5 * LOG2E)
    return pl.pallas_call(
        kern,
        grid=(m // tm,),
        in_specs=[row(d), _resident((1, d)), _resident_layer(w_in, layer), _resident(p_avg.shape),
                  _resident((1, d_qk)), _resident((1, d_qk)), _resident(cw.shape)]
                 + [pl.BlockSpec(memory_space=pl.ANY)] * len(kv_prev),
        out_specs=[seq_t, seq_t, col, row(d_qk), col, row(d_b),
                   pl.BlockSpec((None, 2, d_b), lambda i: (i // tps, 0, 0))],
        out_shape=[jax.ShapeDtypeStruct((n_layers, n_seq, d_qk, seq), F32),
                   jax.ShapeDtypeStruct((n_layers, n_seq, d_a, seq), F32),
                   jax.ShapeDtypeStruct((d_qk, m), BF16),
                   jax.ShapeDtypeStruct((m, d_qk), BF16),
                   jax.ShapeDtypeStruct((d_a, m), BF16),
                   jax.ShapeDtypeStruct((m, d_b), BF16),
                   jax.ShapeDtypeStruct((n_seq, 2, d_b), F32)],
        scratch_shapes=[pltpu.VMEM((tm + SUBLANE, d_b), F32)],
        input_output_aliases={n_in + j: j for j in range(len(kv_prev))},
        compiler_params=_params(("arbitrary",)),
        name="even_in_prompt",
    )(x, g.reshape(1, d), w_in, p_avg, qg, kg, cw, *kv_prev)


def _even_in_sample_kernel(x_ref, g_ref, w_ref, p_ref, qg_ref, kg_ref, cw_ref, s0_ref, s1_ref,
                           q_ref, k_ref, v_ref, gc_ref, u_ref, *, d_qk, d_a, d_b, scale):
    h = _rms(x_ref[...], g_ref[...]).astype(BF16)
    z = _dot(h, w_ref[...])
    q_ref[...] = _group_rms(z[:, :d_qk], p_ref, qg_ref[...]) * scale
    k_ref[...] = _group_rms(z[:, d_qk:2 * d_qk], p_ref, kg_ref[...])
    o = 2 * d_qk
    v_ref[...] = z[:, o:o + d_a]
    o += d_a
    gate_b = z[:, o:o + d_b]
    u = z[:, o + d_b:o + 2 * d_b] * z[:, o + 2 * d_b:o + 3 * d_b]
    conv = cw_ref[0:1, :] * s0_ref[...] + cw_ref[1:2, :] * s1_ref[...] + cw_ref[2:3, :] * u
    gc_ref[...] = (gate_b * conv).astype(BF16)
    u_ref[...] = u


def _even_in_sample(x, g, w_in, layer, p_avg, qg, kg, cw, s0, s1):
    m, d = x.shape
    d_qk = d_a = d_b = p_avg.shape[0]
    kern = functools.partial(_even_in_sample_kernel, d_qk=d_qk, d_a=d_a, d_b=d_b, scale=DH_QK ** -0.5)
    f = lambda dt: jax.ShapeDtypeStruct((m, d_qk), dt)
    blk = _resident((m, d_qk))
    return pl.pallas_call(
        kern,
        grid=(1,),
        in_specs=[_resident((m, d)), _resident((1, d)), _resident_layer(w_in, layer), _resident(p_avg.shape),
                  _resident((1, d_qk)), _resident((1, d_qk)), _resident(cw.shape), blk, blk],
        out_specs=[pl.BlockSpec((m, d_qk), lambda i: (0, 0))] * 5,
        out_shape=[f(F32), f(F32), f(F32), f(BF16), f(F32)],
        compiler_params=_params(("arbitrary",)),
        name="even_in_sample",
    )(x, g.reshape(1, d), w_in, p_avg, qg, kg, cw, s0, s1)


def _lambda(lam_ref, lam_init):
    lam_p = lam_ref[...]
    return (jnp.exp(jnp.sum(lam_p[0:1] * lam_p[1:2], axis=1, keepdims=True))
            - jnp.exp(jnp.sum(lam_p[2:3] * lam_p[3:4], axis=1, keepdims=True)) + lam_init)


def _prompt_tiles(tiles, hp, lam, cfar_ref, sg_ref, qt_ref, k_ref, vt_ref, bias_ref, o_ref, p_scr, *,
                  lam_init, tq):
    n_sub = qt_ref.shape[0] // DH_QK
    n_head = n_sub // 2
    cfar = jnp.concatenate([jnp.full((1, 2 * tq), cfar_ref[hp * n_head + h], F32) for h in range(n_head)],
                           axis=1)
    grp = lax.broadcasted_iota(jnp.int32, (qt_ref.shape[0], tq), 0) // DH_QK

    def scores(n):
        qt = qt_ref[:, n * tq:(n + 1) * tq]
        q_blk = jnp.concatenate([jnp.where(grp == c, qt, jnp.zeros_like(qt)) for c in range(n_sub)], axis=1)
        return _dot(k_ref[0:(n + 1) * tq, :], q_blk)

    s_next = scores(tiles[0])
    for idx, n in enumerate(tiles):
        s_all = s_next
        if idx + 1 < len(tiles):
            s_next = scores(tiles[idx + 1])
        p_buf = p_scr.at[idx % 2]

        m = None
        s_tiles = []
        for t in range(n + 1):
            s = s_all[t * tq:(t + 1) * tq, :]
            if t >= n - 1:
                kind = 1 if t == n else 0
                s = jnp.concatenate([s[:, c * tq:(c + 1) * tq] + bias_ref[c // 2, kind] for c in range(n_sub)],
                                    axis=1)
                smax = jnp.max(s, axis=0, keepdims=True)
            else:
                smax = jnp.max(s, axis=0, keepdims=True) + cfar
            m = smax if m is None else jnp.maximum(m, smax)
            s_tiles.append(s)

        m_far = m - cfar
        for t in range(n + 1):
            shift = m if t >= n - 1 else m_far
            p_buf[t * tq:(t + 1) * tq, :] = jnp.exp2(s_tiles[t] - shift).astype(BF16)

        keys = (n + 1) * tq
        ones = jnp.ones((SOFTMAX_SUM_ROWS, keys), BF16)
        outs = []
        for h in range(n_head):
            vte = jnp.concatenate([vt_ref[h * DH_V:(h + 1) * DH_V, 0:keys], ones], axis=0)
            pv = _dot(vte, p_buf[0:keys, h * 2 * tq:(h + 1) * 2 * tq])
            sub = [pv[0:DH_V, c * tq:(c + 1) * tq] / pv[DH_V:DH_V + 1, c * tq:(c + 1) * tq] for c in range(2)]
            o = sub[0] - lam * sub[1]
            ms = jnp.mean(o * o, axis=0, keepdims=True)
            outs.append(o * lax.rsqrt(ms + EPS))
        o = jnp.concatenate(outs, axis=0) * sg_ref[...] * (1.0 - lam_init)
        o_ref[n * tq:(n + 1) * tq, :] = o.T.astype(BF16)


def _decode_pages(first, last, lam, q_ref, kn_ref, vn_ref, bias, b0_ref, sg_ref, k_refs, v_refs, o_ref,
                  m_scr, l_scr, acc_scr, *, lam_init):
    pp = len(k_refs)
    n_sub, d = acc_scr.shape
    page = k_refs[0].shape[1]

    row = lax.broadcasted_iota(jnp.int32, (n_sub, d), 0)
    col = lax.broadcasted_iota(jnp.int32, (n_sub, d), 1)
    qexp = jnp.where(col // DH_QK == row, jnp.broadcast_to(q_ref[...], (n_sub, d)), 0.0)

    if first:
        m_scr[...] = jnp.sum(qexp * kn_ref[...], axis=1, keepdims=True) + b0_ref[...]
        l_scr[...] = jnp.ones(l_scr.shape, F32)
        acc_scr[...] = jnp.broadcast_to(vn_ref[...], (n_sub, d))

    qb = qexp.astype(BF16)
    s = jnp.concatenate([_dot(qb, k_refs[i][...].astype(BF16)) for i in range(pp)], axis=1)
    s = s + bias
    m_old = m_scr[...]
    m_new = jnp.maximum(m_old, jnp.max(s, axis=1, keepdims=True))
    p = jnp.exp(s - m_new)
    alpha = jnp.exp(m_old - m_new)
    l_scr[...] = alpha * l_scr[...] + jnp.sum(p, axis=1, keepdims=True)
    pb = p.astype(BF16)
    pv = jnp.zeros((n_sub, d), F32)
    for i in range(pp):
        pv = pv + lax.dot_general(pb[:, i * page:(i + 1) * page], v_refs[i][...].astype(BF16),
                                  (((1,), (1,)), ((), ())), preferred_element_type=F32)
    acc_scr[...] = alpha * acc_scr[...] + pv
    m_scr[...] = m_new

    if last:
        o = acc_scr[...] / l_scr[...]
        head = col // DH_V
        a0 = jnp.sum(jnp.where(row == 2 * head, o, 0.0), axis=0, keepdims=True)
        a1 = jnp.sum(jnp.where(row == 2 * head + 1, o, 0.0), axis=0, keepdims=True)
        att = a0 - lam * a1
        sq = att * att
        head1 = head[0:1]
        r = jnp.zeros_like(att)
        for hh in range(d // DH_V):
            ms = jnp.sum(jnp.where(head1 == hh, sq, 0.0), axis=1, keepdims=True) * (1.0 / DH_V)
            r = jnp.where(head1 == hh, lax.rsqrt(ms + EPS), r)
        o_ref[...] = att * r * sg_ref[...] * (1.0 - lam_init)


def _attention_kernel(pt_ref, cfar_ref, lam_ref, sgc_ref, qt_ref, k_ref, vt_ref, bias_ref,
                      q_ref, kn_ref, vn_ref, bt_ref, b0_ref, sgr_ref, *rest, pp, tiles, lam_init, tq, layer):
    ck_hbm, cv_hbm, o_ref, od_ref, p_scr, m_scr, l_scr, acc_scr, kbuf, vbuf, sem = rest
    hp = pl.program_id(1)
    half = pl.program_id(2)
    step = (pl.program_id(0) * pl.num_programs(1) + hp) * pl.num_programs(2) + half
    last_step = pl.num_programs(0) * pl.num_programs(1) * pl.num_programs(2) - 1
    slot = step % 2

    def page_copies(of_step, into):
        copies = []
        for i in range(pp):
            pid = pt_ref[of_step * pp + i]
            copies.append(pltpu.make_async_copy(ck_hbm.at[layer, pid], kbuf.at[into, i], sem.at[into, 0]))
            copies.append(pltpu.make_async_copy(cv_hbm.at[layer, pid], vbuf.at[into, i], sem.at[into, 1]))
        return copies

    @pl.when(step == 0)
    def _():
        for c in page_copies(0, 0):
            c.start()

    lam = _lambda(lam_ref, lam_init)
    nxt = jnp.minimum(step + 1, last_step)
    for hv in range(2):
        @pl.when(half == hv)
        def _(hv=hv):
            for c in page_copies(nxt, 1 - slot):
                c.start()
            _prompt_tiles(tiles[hv], hp, lam, cfar_ref, sgc_ref, qt_ref, k_ref, vt_ref, bias_ref, o_ref, p_scr,
                          lam_init=lam_init, tq=tq)
            for c in page_copies(step, slot):
                c.wait()
            k_refs = [kbuf.at[slot, i] for i in range(pp)]
            v_refs = [vbuf.at[slot, i] for i in range(pp)]
            _decode_pages(hv == 0, hv == 1, lam, q_ref, kn_ref, vn_ref, bt_ref[...], b0_ref, sgr_ref,
                          k_refs, v_refs, od_ref, m_scr, l_scr, acc_scr, lam_init=lam_init)

    @pl.when(step == last_step)
    def _():
        for c in page_copies(last_step, 1 - slot):
            c.wait()


def _attention(page_table, cfar, lam_p, sg, qt, k, vt, bias_tiles, q, kn, vn, bias_t, bias0,
               cache_kt, cache_vt, layer, n_seq, seq, lam_init):
    d_qk, m = qt.shape
    n_dec, n_pages = page_table.shape
    d = q.shape[-1]
    page = cache_kt.shape[-1]
    pair = 2 * 2 * DH_QK
    n_pair = d_qk // pair
    nq = seq // TQ
    order = sorted(range(nq), reverse=True)
    tiles = ([n for i, n in enumerate(order) if i % 4 in (0, 3)], [n for i, n in enumerate(order) if i % 4 in (1, 2)])
    halves = 2
    pp = n_pages // halves
    assert n_pages % halves == 0 and n_seq * n_pair == n_dec, "one sample sequence per (prompt sequence, head pair)"
    kern = functools.partial(_attention_kernel, pp=pp, tiles=tiles, lam_init=lam_init, tq=TQ, layer=layer)
    const = lambda shape: pl.BlockSpec(shape, lambda b, h, s, pt: (0,) * len(shape))
    tok = pl.BlockSpec((None, 1, d), lambda b, h, s, pt: (b * n_pair + h, 0, 0))
    grid_spec = pltpu.PrefetchScalarGridSpec(
        num_scalar_prefetch=1,
        grid=(n_seq, n_pair, halves),
        in_specs=[pl.BlockSpec(memory_space=pltpu.SMEM),
                  const(lam_p.shape),
                  const((pair, 1)),
                  pl.BlockSpec((pair, seq), lambda b, h, s, pt: (h, b)),
                  pl.BlockSpec((seq, pair), lambda b, h, s, pt: (b, h)),
                  pl.BlockSpec((pair, seq), lambda b, h, s, pt: (h, b)),
                  pl.BlockSpec((2, 2, TQ, TQ), lambda b, h, s, pt: (h, 0, 0, 0)),
                  tok, tok, tok,
                  pl.BlockSpec((d // DH_QK, pp * page), lambda b, h, s, pt: (0, s)),
                  const((d // DH_QK, 1)),
                  const((1, d)),
                  pl.BlockSpec(memory_space=pl.ANY), pl.BlockSpec(memory_space=pl.ANY)],
        out_specs=[pl.BlockSpec((seq, pair), lambda b, h, s, pt: (b, h)), tok],
        scratch_shapes=[pltpu.VMEM((2, seq, (pair // DH_QK) * TQ), BF16),
                        pltpu.VMEM((d // DH_QK, 1), F32), pltpu.VMEM((d // DH_QK, 1), F32),
                        pltpu.VMEM((d // DH_QK, d), F32),
                        pltpu.VMEM((2, pp, d, page), F32), pltpu.VMEM((2, pp, d, page), F32),
                        pltpu.SemaphoreType.DMA((2, 2))],
    )
    att_p, att_s = pl.pallas_call(
        kern,
        grid_spec=grid_spec,
        out_shape=[jax.ShapeDtypeStruct((m, d_qk), BF16), jax.ShapeDtypeStruct((n_dec, 1, d), F32)],
        compiler_params=_params(("arbitrary", "arbitrary", "arbitrary")),
        name="attention",
    )(page_table.reshape(-1), cfar, lam_p, sg[:pair].reshape(-1, 1), qt, k, vt, bias_tiles,
      q.reshape(n_dec, 1, d), kn.reshape(n_dec, 1, d), vn.reshape(n_dec, 1, d), bias_t, bias0, sg.reshape(1, -1),
      cache_kt, cache_vt)
    return att_p, att_s.reshape(n_dec, d)


def _layernorm_swish(c, g, b):
    mu = jnp.mean(c, axis=-1, keepdims=True)
    xc = c - mu
    y = xc * lax.rsqrt(jnp.mean(xc * xc, axis=-1, keepdims=True) + EPS) * g + b
    return jax.nn.silu(y)


def _odd_prompt_kernel(x_ref, g_ref, w1_ref, b1_ref, cw_ref, cb_ref, lg_ref, lb_ref, w2_ref, b2_ref,
                       o_ref, st_ref, u_scr, c_scr, *, tiles_per_seq, width):
    i = pl.program_id(0)
    tm, d = x_ref.shape
    nl = d // LANE
    x = x_ref[...]
    h = _rms(x, g_ref[...]).astype(BF16)
    z = _dot(h, w1_ref[...]) + b1_ref[...]
    u = z[:, :d] * jax.nn.sigmoid(z[:, d:])

    @pl.when(i % tiles_per_seq == 0)
    def _():
        u_scr[0:HALO * nl, :] = jnp.zeros((HALO * nl, LANE), F32)

    for l in range(nl):
        u_scr[pl.ds(HALO * nl + l, tm, stride=nl), :] = u[:, l * LANE:(l + 1) * LANE]

    @pl.when(i % tiles_per_seq == tiles_per_seq - 1)
    def _():
        st_ref[...] = u[tm - (width - 1):, :]

    first = HALO - (width - 1)
    taps = [cw_ref[j] for j in range(width)]

    def chunk(r, carry):
        t0 = r * CONV_UNROLL
        slab = lambda t: pl.ds(pl.multiple_of(t * nl, nl), nl)
        acc = [None] * CONV_UNROLL
        for k in range(CONV_UNROLL + width - 1):
            row = u_scr[slab(t0 + first + k), :]
            for o in range(max(0, k - width + 1), min(CONV_UNROLL, k + 1)):
                term = taps[k - o] * row
                acc[o] = term if acc[o] is None else acc[o] + term
        for o in range(CONV_UNROLL):
            c_scr[slab(t0 + o), :] = acc[o]
        return carry

    lax.fori_loop(0, tm // CONV_UNROLL, chunk, 0)

    u_scr[0:HALO * nl, :] = u_scr[tm * nl:(tm + HALO) * nl, :]
    c = jnp.concatenate([c_scr[pl.ds(l, tm, stride=nl), :] for l in range(nl)], axis=1)
    a = _layernorm_swish(c + cb_ref[...], lg_ref[...], lb_ref[...]).astype(BF16)
    o_ref[...] = x + _dot(a, w2_ref[...]) + b2_ref[...]


def _odd_prompt(x, g, w1, w2, layer, b1, cw, cb, lg, lb, b2, n_seq, seq):
    m, d = x.shape
    width = cw.shape[0]
    tm = TM_ODD
    tps = seq // tm
    row = pl.BlockSpec((tm, d), lambda i: (i, 0))
    vec = lambda a: a.reshape(1, -1)
    nl = d // LANE
    assert tm >= width - 1 and tm % CONV_UNROLL == 0 and HALO >= width - 1
    kern = functools.partial(_odd_prompt_kernel, tiles_per_seq=tps, width=width)
    return pl.pallas_call(
        kern,
        grid=(m // tm,),
        in_specs=[row, _resident((1, d)), _resident_layer(w1, layer), _resident((1, 2 * d)),
                  _resident((width, nl, LANE)),
                  _resident((1, d)), _resident((1, d)), _resident((1, d)), _resident_layer(w2, layer),
                  _resident((1, d))],
        out_specs=[row, pl.BlockSpec((None, width - 1, d), lambda i: (i // tps, 0, 0))],
        out_shape=[jax.ShapeDtypeStruct((m, d), F32),
                   jax.ShapeDtypeStruct((n_seq, width - 1, d), F32)],
        scratch_shapes=[pltpu.VMEM(((tm + HALO) * nl, LANE), F32), pltpu.VMEM((tm * nl, LANE), F32)],
        compiler_params=_params(("arbitrary",)),
        name="odd_prompt",
    )(x, vec(g), w1, vec(b1), cw.reshape(width, nl, LANE), vec(cb), vec(lg), vec(lb), w2, vec(b2))


def _odd_sample_kernel(x_ref, g_ref, w1_ref, b1_ref, cw_ref, cb_ref, lg_ref, lb_ref, w2_ref, b2_ref,
                       st_ref, o_ref, nst_ref):
    n_hist = st_ref.shape[0]
    d = x_ref.shape[1]
    x = x_ref[...]
    h = _rms(x, g_ref[...]).astype(BF16)
    z = _dot(h, w1_ref[...]) + b1_ref[...]
    u = z[:, :d] * jax.nn.sigmoid(z[:, d:])
    c = cw_ref[n_hist:n_hist + 1, :] * u
    for j in range(n_hist):
        c = c + cw_ref[j:j + 1, :] * st_ref[j]
    for j in range(n_hist - 1):
        nst_ref[j] = st_ref[j + 1]
    nst_ref[n_hist - 1] = u
    a = _layernorm_swish(c + cb_ref[...], lg_ref[...], lb_ref[...]).astype(BF16)
    o_ref[...] = x + _dot(a, w2_ref[...]) + b2_ref[...]


def _odd_sample(x, g, w1, w2, layer, b1, cw, cb, lg, lb, b2, state_t):
    m, d = x.shape
    vec = lambda a: a.reshape(1, -1)
    whole = lambda shape: pl.BlockSpec(shape, lambda i: (0,) * len(shape))
    return pl.pallas_call(
        _odd_sample_kernel,
        grid=(1,),
        in_specs=[_resident((m, d)), _resident((1, d)), _resident_layer(w1, layer), _resident((1, 2 * d)),
                  _resident(cw.shape), _resident((1, d)), _resident((1, d)), _resident((1, d)),
                  _resident_layer(w2, layer), _resident((1, d)), _resident(state_t.shape)],
        out_specs=[whole((m, d)), whole(state_t.shape)],
        out_shape=[jax.ShapeDtypeStruct((m, d), F32), jax.ShapeDtypeStruct(state_t.shape, F32)],
        compiler_params=_params(("arbitrary",)),
        name="odd_sample",
    )(x, vec(g), w1, vec(b1), cw, vec(cb), vec(lg), vec(lb), w2, vec(b2), state_t)


def _bias_by_distance(rel_bias, dist):
    n = jnp.maximum(dist, 0)
    nf = jnp.maximum(n, 1).astype(F32)
    large = MAX_EXACT + (jnp.log(nf / MAX_EXACT) / math.log(MAX_DISTANCE / MAX_EXACT)
                         * (N_BUCKETS - MAX_EXACT)).astype(jnp.int32)
    large = jnp.minimum(large, N_BUCKETS - 1)
    return rel_bias[jnp.where(n < MAX_EXACT, n, large)]


def _prompt_bias_tiles(by_dist_nat):
    assert TQ >= FAR - 1
    n_head = by_dist_nat.shape[0]
    by_dist = by_dist_nat * LOG2E
    pad = jnp.zeros((n_head, 1), F32)
    diag = jnp.concatenate([jnp.full((n_head, TQ - 1), NEG, F32), by_dist[:, :TQ], pad], axis=1)
    near = jnp.concatenate([by_dist[:, 1:], pad], axis=1)

    def toeplitz(a):
        period = a.shape[1]
        rows = jnp.tile(a, (1, TQ))[:, :TQ * (period - 1)].reshape(n_head, TQ, period - 1)
        return rows[:, :, TQ - 1:]

    tiles = jnp.stack([toeplitz(near), toeplitz(diag)], axis=1)
    return tiles, by_dist[:, TQ + 1]


def _decode_bias(by_dist_nat, past_len):
    n_head, n_tab = by_dist_nat.shape
    assert n_tab > FAR and past_len + 1 >= n_tab
    far = jnp.broadcast_to(by_dist_nat[:, FAR:FAR + 1], (n_head, past_len + 1 - n_tab))
    return jnp.repeat(jnp.concatenate([far, by_dist_nat[:, ::-1]], axis=1), 2, axis=0)


def kernel(x_prompt, x_sample, cache_k, cache_v, state_conv_b, state_conv_c, page_table, rel_bias, norm_ffn1, ffn1_w_gate, ffn1_w_up, ffn1_w_down, norm_mix, norm_ffn2, ffn2_w_gate, ffn2_w_up, ffn2_w_down, w_in_even, w_out_even, q_norm_gain, k_norm_gain, lambda_q1, lambda_k1, lambda_q2, lambda_k2, subln_gain, conv_b_w, w_pw1, b_pw1, conv_c_w, conv_c_b, ln_c_gain, ln_c_bias, w_pw2, b_pw2):
    bp, tp, d = x_prompt.shape
    bs = x_sample.shape[0]
    depth = norm_ffn1.shape[0]
    n_even, n_phys, page, n_sub, dh_qk = cache_k.shape
    assert dh_qk == DH_QK and cache_v.shape[-1] == DH_V and x_sample.shape[1] == 1
    d_qk = n_sub * dh_qk
    d_a = cache_v.shape[3] * DH_V
    past_len = page_table.shape[1] * page
    assert tp % TM_ROWS == 0 and tp % TM_ODD == 0 and tp % TQ == 0 and tp >= 2 * TQ

    bf = lambda w: w.astype(BF16)
    xp = x_prompt.reshape(bp * tp, d)
    xs = x_sample.reshape(bs, d)

    cache_kt = jnp.transpose(cache_k, (0, 1, 3, 4, 2)).reshape(n_even, n_phys, d_qk, page)
    cache_vt = jnp.transpose(cache_v, (0, 1, 3, 4, 2)).reshape(n_even, n_phys, d_a, page)
    state_c_t = jnp.transpose(state_conv_c, (0, 2, 1, 3))

    p_avg = jnp.kron(jnp.eye(d_qk // DH_QK, dtype=F32), jnp.full((DH_QK, DH_QK), 1.0 / DH_QK, F32)).astype(BF16)
    by_dist_nat = _bias_by_distance(rel_bias, jnp.arange(2 * TQ)).T
    bias_tiles, bias_far = _prompt_bias_tiles(by_dist_nat)
    bias_dec = _decode_bias(by_dist_nat, past_len)
    bias_dec_past, bias_dec_new = bias_dec[:, :past_len], bias_dec[:, past_len:]

    w_in, w_out, w1, w2 = (w.astype(BF16) for w in (w_in_even, w_out_even, w_pw1, w_pw2))
    ffn1_f32 = (ffn1_w_gate, ffn1_w_up, ffn1_w_down)
    ffn2_f32 = (ffn2_w_gate, ffn2_w_up, ffn2_w_down)
    ffn_w = tuple(w[0].astype(BF16) for w in ffn1_f32)

    kv_prompt = ()
    ks_l, vs_l, cbp_l, cbs_l, ccp_l, ccs_l = [], [], [], [], [], []
    for li in range(depth):
        xp, xs, ffn_w = _ffn(xp, xs, norm_ffn1[li], ffn_w, cast=(*ffn2_f32, li))
        if li % 2 == 0:
            e = li // 2
            lam_init = 0.8 - 0.6 * math.exp(-0.3 * li)
            qg = jnp.tile(q_norm_gain[e], d_qk // DH_QK).reshape(1, d_qk)
            kg = jnp.tile(k_norm_gain[e], d_qk // DH_QK).reshape(1, d_qk)
            lam_p = jnp.stack([lambda_q1[e], lambda_k1[e], lambda_q2[e], lambda_k2[e]])
            sg = jnp.tile(subln_gain[e], d_a // DH_V)

            *kv_prompt, qtb, kb, vtb, gcp, cbp = _even_in_prompt(
                xp, norm_mix[li], w_in, e, n_even, tuple(kv_prompt), p_avg, qg, kg, conv_b_w[e], bp, tp)
            qs, kn, vn, gcs, us = _even_in_sample(
                xs, norm_mix[li], w_in, e, p_avg, qg, kg, conv_b_w[e],
                state_conv_b[e, :, 0], state_conv_b[e, :, 1])
            attp, atts = _attention(page_table, bias_far, lam_p, sg, qtb, kb, vtb, bias_tiles,
                                    qs, kn, vn, bias_dec_past, bias_dec_new, cache_kt, cache_vt, e,
                                    bp, tp, lam_init)
            mixer = (attp, gcp, atts, gcs, w_out, e)

            ks_l.append(kn)
            vs_l.append(vn)
            cbp_l.append(cbp)
            cbs_l.append(jnp.stack([state_conv_b[e, :, 1], us], axis=1))
        else:
            o = li // 2
            prm = (norm_mix[li], w1, w2, o, b_pw1[o], conv_c_w[o], conv_c_b[o], ln_c_gain[o], ln_c_bias[o], b_pw2[o])
            xp, ccp = _odd_prompt(xp, *prm, bp, tp)
            xs, ccs = _odd_sample(xs, *prm, state_c_t[o])
            ccp_l.append(ccp)
            ccs_l.append(ccs)
            mixer = ()
        xp, xs, ffn_w = _ffn(xp, xs, norm_ffn2[li], ffn_w, mixer=mixer,
                             cast=(*ffn1_f32, li + 1) if li + 1 < depth else ())

    n_h = d_a // DH_V
    kt_all, vt_all = kv_prompt
    new_k_prompt = jnp.transpose(kt_all.reshape(n_even, bp, n_sub, DH_QK, tp), (0, 1, 4, 2, 3))
    new_v_prompt = jnp.transpose(vt_all.reshape(n_even, bp, n_h, DH_V, tp), (0, 1, 4, 2, 3))
    return (xp.reshape(bp, tp, d), xs.reshape(bs, 1, d),
            new_k_prompt, new_v_prompt,
            jnp.stack(ks_l).reshape(n_even, bs, 1, n_sub, DH_QK),
            jnp.stack(vs_l).reshape(n_even, bs, 1, n_h, DH_V),
            jnp.stack(cbp_l), jnp.stack(cbs_l), jnp.stack(ccp_l),
            jnp.transpose(jnp.stack(ccs_l), (0, 2, 1, 3)))
```

```python
import functools
import math

import jax
import jax.numpy as jnp
from jax import lax
from jax.experimental import pallas as pl
from jax.experimental.pallas import tpu as pltpu

F32 = jnp.float32
BF16 = jnp.bfloat16
EPS = 1e-6
NEG = -1e30

LANE = 128
SUBLANE = 8
MXU_DEPTH = 256
VMEM_LIMIT_BYTES = 56 * 1024 * 1024

DH_QK = 32
DH_V = 64
N_BUCKETS = 32
MAX_EXACT = N_BUCKETS // 2
MAX_DISTANCE = 128
FAR = MAX_DISTANCE

TM_ROWS = 1024
TM_FFN = 512
FFN_CHUNKS = 2
CAST_STEPS = 16
TM_ODD = 1024
TQ = 256
SOFTMAX_SUM_ROWS = 16
HALO = 32
CONV_UNROLL = 16
LOG2E = math.log2(math.e)


def _params(sem):
    return pltpu.CompilerParams(dimension_semantics=sem, vmem_limit_bytes=VMEM_LIMIT_BYTES)


def _resident(shape):
    n = len(shape)
    return pl.BlockSpec(shape, lambda *_: (0,) * n, pipeline_mode=pl.Buffered(1))


def _resident_layer(stacked, layer):
    n = stacked.ndim - 1
    return pl.BlockSpec((None,) + stacked.shape[1:], lambda *_: (layer,) + (0,) * n,
                        pipeline_mode=pl.Buffered(1))


def _rms(x, g):
    return x * lax.rsqrt(jnp.mean(x * x, axis=-1, keepdims=True) + EPS) * g


def _dot(a, b):
    return jnp.dot(a, b, preferred_element_type=F32)


def _group_rms(z, p_ref, gain):
    sq = z * z
    hi = sq.astype(BF16)
    lo = (sq - hi.astype(F32)).astype(BF16)
    ms = _dot(hi, p_ref[...]) + _dot(lo, p_ref[...])
    return z * lax.rsqrt(ms + EPS) * gain


def _swiglu_rows(x, mix, g_ref, wg_ref, wu_ref, wd_ref, wo_ref, n_chunks):
    if mix:
        att, gc = mix
        d_a = att.shape[1]
        x = x + _dot(att.astype(BF16), wo_ref[0:d_a, :]) + _dot(gc, wo_ref[d_a:, :])
    h = _rms(x, g_ref[...]).astype(BF16)
    n_pass = wg_ref.shape[1] // MXU_DEPTH
    cuts = [MXU_DEPTH * ((n_pass * c + n_chunks - 1) // n_chunks) for c in range(n_chunks)] + [wg_ref.shape[1]]
    y = jnp.zeros(x.shape, F32)
    for lo, hi in zip(cuts[:-1], cuts[1:]):
        gate = _dot(h, wg_ref[:, lo:hi])
        up = _dot(h, wu_ref[:, lo:hi])
        a = (jax.nn.silu(gate) * up).astype(BF16)
        y = y + _dot(a, wd_ref[lo:hi, :])
    return x + 0.5 * y


def _ffn_kernel(x_ref, g_ref, wg_ref, wu_ref, wd_ref, *rest, n_chunks, mixer, n_cast):
    rest = list(rest)
    take = lambda n: [rest.pop(0) for _ in range(n)]
    a_ref, gc_ref, wo_ref = take(3) if mixer else (None, None, None)
    xs_ref, = take(1)
    as_ref, gcs_ref = take(2) if mixer else (None, None)
    cast_src = take(n_cast)
    o_ref, os_ref = take(2)
    cast_dst = take(n_cast)
    step = pl.program_id(0)
    if n_cast:
        @pl.when(step < CAST_STEPS)
        def _():
            for src, dst in zip(cast_src, cast_dst):
                dst[...] = src[...].astype(BF16)

    @pl.when(step == 0)
    def _():
        mix_s = (as_ref[...], gcs_ref[...]) if mixer else ()
        os_ref[...] = _swiglu_rows(xs_ref[...], mix_s, g_ref, wg_ref, wu_ref, wd_ref, wo_ref, n_chunks)

    mix = (a_ref[...], gc_ref[...]) if mixer else ()
    o_ref[...] = _swiglu_rows(x_ref[...], mix, g_ref, wg_ref, wu_ref, wd_ref, wo_ref, n_chunks)


def _ffn(x, xs, g, weights, mixer=(), cast=()):
    m, d = x.shape
    tm = TM_FFN
    steps = m // tm
    row = lambda width: pl.BlockSpec((tm, width), lambda i: (i, 0))
    if mixer:
        att, gc, att_s, gc_s, w_out, out_layer = mixer
        mixer_ops = [att, gc, w_out]
        mixer_specs = [row(att.shape[1]), row(gc.shape[1]), _resident_layer(w_out, out_layer)]
        side_ops = [xs, att_s, gc_s]
    else:
        mixer_ops, mixer_specs, side_ops = [], [], [xs]
    side_specs = [_resident(a.shape) for a in side_ops]
    *stacks, cast_layer = cast if cast else (None,)
    assert not stacks or steps >= CAST_STEPS
    slab = lambda i: jnp.minimum(i, CAST_STEPS - 1)
    cast_in = [pl.BlockSpec((None, w.shape[1] // CAST_STEPS, w.shape[2]), lambda i: (cast_layer, slab(i), 0))
               for w in stacks]
    cast_out = [pl.BlockSpec((w.shape[1] // CAST_STEPS, w.shape[2]), lambda i: (slab(i), 0)) for w in stacks]
    out = pl.pallas_call(
        functools.partial(_ffn_kernel, n_chunks=FFN_CHUNKS, mixer=bool(mixer), n_cast=len(stacks)),
        grid=(steps,),
        in_specs=[row(d), _resident((1, d))] + [_resident(w.shape) for w in weights]
                 + mixer_specs + side_specs + cast_in,
        out_specs=[row(d), pl.BlockSpec(xs.shape, lambda i: (0, 0))] + cast_out,
        out_shape=[jax.ShapeDtypeStruct((m, d), F32), jax.ShapeDtypeStruct(xs.shape, F32)]
                  + [jax.ShapeDtypeStruct(w.shape[1:], BF16) for w in stacks],
        compiler_params=_params(("arbitrary",)),
        name="ffn_mixer" if mixer else "ffn",
    )(x, g.reshape(1, d), *weights, *mixer_ops, *side_ops, *stacks)
    return out[0], out[1], tuple(out[2:])


def _even_in_prompt_kernel(x_ref, g_ref, w_ref, p_ref, qg_ref, kg_ref, cw_ref, *rest,
                           tiles_per_seq, d_qk, d_a, d_b, scale, layer):
    kt_ref, vt_ref, qtb_ref, kb_ref, vtb_ref, gc_ref, st_ref, u_scr = rest[-8:]
    i = pl.program_id(0)
    tm = x_ref.shape[0]
    h = _rms(x_ref[...], g_ref[...]).astype(BF16)
    z = _dot(h, w_ref[...])
    q = _group_rms(z[:, :d_qk], p_ref, qg_ref[...]) * scale
    k = _group_rms(z[:, d_qk:2 * d_qk], p_ref, kg_ref[...])
    o = 2 * d_qk
    v = z[:, o:o + d_a]
    o += d_a
    gate_b = z[:, o:o + d_b]
    u = z[:, o + d_b:o + 2 * d_b] * z[:, o + 2 * d_b:o + 3 * d_b]

    kt = k.T
    vt = v.T
    for ref, val in ((kt_ref, kt), (vt_ref, vt)):
        if len(ref.shape) == 2:
            ref[...] = val
        else:
            for slot in range(ref.shape[0]):
                ref[slot] = val if slot == layer else jnp.zeros_like(val)
    qtb_ref[...] = q.T.astype(BF16)
    kb_ref[...] = k.astype(BF16)
    vtb_ref[...] = vt.astype(BF16)

    @pl.when(i % tiles_per_seq == 0)
    def _():
        u_scr[0:SUBLANE, :] = jnp.zeros((SUBLANE, d_b), F32)

    u_scr[SUBLANE:SUBLANE + tm, :] = u
    conv = (cw_ref[0:1, :] * u_scr[SUBLANE - 2:SUBLANE - 2 + tm, :]
            + cw_ref[1:2, :] * u_scr[SUBLANE - 1:SUBLANE - 1 + tm, :]
            + cw_ref[2:3, :] * u)
    gc_ref[...] = (gate_b * conv).astype(BF16)
    st_ref[...] = u_scr[tm + SUBLANE - 2:tm + SUBLANE, :]
    u_scr[0:SUBLANE, :] = u_scr[tm:tm + SUBLANE, :]


def _even_in_prompt(x, g, w_in, layer, n_layers, kv_prev, p_avg, qg, kg, cw, n_seq, seq):
    m, d = x.shape
    d_qk = d_a = d_b = p_avg.shape[0]
    tm = TM_ROWS
    tps = seq // tm
    row = lambda width, dt=None: pl.BlockSpec((tm, width), lambda i: (i, 0))
    col = pl.BlockSpec((d_qk, tm), lambda i: (0, i))
    if kv_prev:
        seq_t = pl.BlockSpec((None, None, d_qk, tm), lambda i: (layer, i // tps, 0, i % tps))
    else:
        seq_t = pl.BlockSpec((n_layers, None, d_qk, tm), lambda i: (0, i // tps, 0, i % tps))
    n_in = 7
    kern = functools.partial(_even_in_prompt_kernel, tiles_per_seq=tps, d_qk=d_qk, d_a=d_a, d_b=d_b, layer=layer,
                             scale=DH_QK ** -0.5 * LOG2E)
    return pl.pallas_call(
        kern,
        grid=(m // tm,),
        in_specs=[row(d), _resident((1, d)), _resident_layer(w_in, layer), _resident(p_avg.shape),
                  _resident((1, d_qk)), _resident((1, d_qk)), _resident(cw.shape)]
                 + [pl.BlockSpec(memory_space=pl.ANY)] * len(kv_prev),
        out_specs=[seq_t, seq_t, col, row(d_qk), col, row(d_b),
                   pl.BlockSpec((None, 2, d_b), lambda i: (i // tps, 0, 0))],
        out_shape=[jax.ShapeDtypeStruct((n_layers, n_seq, d_qk, seq), F32),
                   jax.ShapeDtypeStruct((n_layers, n_seq, d_a, seq), F32),
                   jax.ShapeDtypeStruct((d_qk, m), BF16),
                   jax.ShapeDtypeStruct((m, d_qk), BF16),
                   jax.ShapeDtypeStruct((d_a, m), BF16),
                   jax.ShapeDtypeStruct((m, d_b), BF16),
                   jax.ShapeDtypeStruct((n_seq, 2, d_b), F32)],
        scratch_shapes=[pltpu.VMEM((tm + SUBLANE, d_b), F32)],
        input_output_aliases={n_in + j: j for j in range(len(kv_prev))},
        compiler_params=_params(("arbitrary",)),
        name="even_in_prompt",
    )(x, g.reshape(1, d), w_in, p_avg, qg, kg, cw, *kv_prev)


def _even_in_sample_kernel(x_ref, g_ref, w_ref, p_ref, qg_ref, kg_ref, cw_ref, s0_ref, s1_ref,
                           q_ref, k_ref, v_ref, gc_ref, u_ref, *, d_qk, d_a, d_b, scale):
    h = _rms(x_ref[...], g_ref[...]).astype(BF16)
    z = _dot(h, w_ref[...])
    q_ref[...] = _group_rms(z[:, :d_qk], p_ref, qg_ref[...]) * scale
    k_ref[...] = _group_rms(z[:, d_qk:2 * d_qk], p_ref, kg_ref[...])
    o = 2 * d_qk
    v_ref[...] = z[:, o:o + d_a]
    o += d_a
    gate_b = z[:, o:o + d_b]
    u = z[:, o + d_b:o + 2 * d_b] * z[:, o + 2 * d_b:o + 3 * d_b]
    conv = cw_ref[0:1, :] * s0_ref[...] + cw_ref[1:2, :] * s1_ref[...] + cw_ref[2:3, :] * u
    gc_ref[...] = (gate_b * conv).astype(BF16)
    u_ref[...] = u


def _even_in_sample(x, g, w_in, layer, p_avg, qg, kg, cw, s0, s1):
    m, d = x.shape
    d_qk = d_a = d_b = p_avg.shape[0]
    kern = functools.partial(_even_in_sample_kernel, d_qk=d_qk, d_a=d_a, d_b=d_b, scale=DH_QK ** -0.5)
    f = lambda dt: jax.ShapeDtypeStruct((m, d_qk), dt)
    blk = _resident((m, d_qk))
    return pl.pallas_call(
        kern,
        grid=(1,),
        in_specs=[_resident((m, d)), _resident((1, d)), _resident_layer(w_in, layer), _resident(p_avg.shape),
                  _resident((1, d_qk)), _resident((1, d_qk)), _resident(cw.shape), blk, blk],
        out_specs=[pl.BlockSpec((m, d_qk), lambda i: (0, 0))] * 5,
        out_shape=[f(F32), f(F32), f(F32), f(BF16), f(F32)],
        compiler_params=_params(("arbitrary",)),
        name="even_in_sample",
    )(x, g.reshape(1, d), w_in, p_avg, qg, kg, cw, s0, s1)


def _lambda(lam_ref, lam_init):
    lam_p = lam_ref[...]
    return (jnp.exp(jnp.sum(lam_p[0:1] * lam_p[1:2], axis=1, keepdims=True))
            - jnp.exp(jnp.sum(lam_p[2:3] * lam_p[3:4], axis=1, keepdims=True)) + lam_init)


def _prompt_tiles(tiles, hp, lam, cfar_ref, sg_ref, qt_ref, k_ref, vt_ref, bias_ref, o_ref, p_scr, *,
                  lam_init, tq):
    n_sub = qt_ref.shape[0] // DH_QK
    n_head = n_sub // 2
    cfar = jnp.concatenate([jnp.full((1, 2 * tq), cfar_ref[hp * n_head + h], F32) for h in range(n_head)],
                           axis=1)
    grp = lax.broadcasted_iota(jnp.int32, (qt_ref.shape[0], tq), 0) // DH_QK

    def scores(n):
        qt = qt_ref[:, n * tq:(n + 1) * tq]
        q_blk = jnp.concatenate([jnp.where(grp == c, qt, jnp.zeros_like(qt)) for c in range(n_sub)], axis=1)
        return _dot(k_ref[0:(n + 1) * tq, :], q_blk)

    s_next = scores(tiles[0])
    for idx, n in enumerate(tiles):
        s_all = s_next
        if idx + 1 < len(tiles):
            s_next = scores(tiles[idx + 1])
        p_buf = p_scr.at[idx % 2]

        m = None
        s_tiles = []
        for t in range(n + 1):
            s = s_all[t * tq:(t + 1) * tq, :]
            if t >= n - 1:
                kind = 1 if t == n else 0
                s = jnp.concatenate([s[:, c * tq:(c + 1) * tq] + bias_ref[c // 2, kind] for c in range(n_sub)],
                                    axis=1)
                smax = jnp.max(s, axis=0, keepdims=True)
            else:
                smax = jnp.max(s, axis=0, keepdims=True) + cfar
            m = smax if m is None else jnp.maximum(m, smax)
            s_tiles.append(s)

        m_far = m - cfar
        for t in range(n + 1):
            shift = m if t >= n - 1 else m_far
            if t < n:
                p_buf[t * tq:(t + 1) * tq, :] = jnp.exp2(s_tiles[t] - shift).astype(BF16)
                continue
            half = tq // 2
            lo = t * tq
            p_buf[lo:lo + half, :] = jnp.exp2(s_tiles[t][:half] - shift).astype(BF16)
            for c in range(n_sub):
                dead = slice(c * tq, c * tq + half)
                live = slice(c * tq + half, (c + 1) * tq)
                p_buf[lo + half:lo + tq, dead] = jnp.zeros((half, half), BF16)
                p_buf[lo + half:lo + tq, live] = jnp.exp2(s_tiles[t][half:, live] - shift[:, live]).astype(BF16)

        keys = (n + 1) * tq
        ones = jnp.ones((SOFTMAX_SUM_ROWS, keys), BF16)
        outs = []
        for h in range(n_head):
            vte = jnp.concatenate([vt_ref[h * DH_V:(h + 1) * DH_V, 0:keys], ones], axis=0)
            pv = _dot(vte, p_buf[0:keys, h * 2 * tq:(h + 1) * 2 * tq])
            sub = [pv[0:DH_V, c * tq:(c + 1) * tq] / pv[DH_V:DH_V + 1, c * tq:(c + 1) * tq] for c in range(2)]
            o = sub[0] - lam * sub[1]
            ms = jnp.mean(o * o, axis=0, keepdims=True)
            outs.append(o * lax.rsqrt(ms + EPS))
        o = jnp.concatenate(outs, axis=0) * sg_ref[...] * (1.0 - lam_init)
        o_ref[n * tq:(n + 1) * tq, :] = o.T.astype(BF16)


def _decode_pages(first, last, lam, q_ref, kn_ref, vn_ref, bias, b0_ref, sg_ref, k_refs, v_refs, o_ref,
                  m_scr, l_scr, acc_scr, *, lam_init):
    pp = len(k_refs)
    n_sub, d = acc_scr.shape
    page = k_refs[0].shape[1]

    row = lax.broadcasted_iota(jnp.int32, (n_sub, d), 0)
    col = lax.broadcasted_iota(jnp.int32, (n_sub, d), 1)
    qexp = jnp.where(col // DH_QK == row, jnp.broadcast_to(q_ref[...], (n_sub, d)), 0.0)

    if first:
        m_scr[...] = jnp.sum(qexp * kn_ref[...], axis=1, keepdims=True) + b0_ref[...]
        l_scr[...] = jnp.ones(l_scr.shape, F32)
        acc_scr[...] = jnp.broadcast_to(vn_ref[...], (n_sub, d))

    qb = qexp.astype(BF16)
    s = jnp.concatenate([_dot(qb, k_refs[i][...].astype(BF16)) for i in range(pp)], axis=1)
    s = s + bias
    m_old = m_scr[...]
    m_new = jnp.maximum(m_old, jnp.max(s, axis=1, keepdims=True))
    p = jnp.exp(s - m_new)
    alpha = jnp.exp(m_old - m_new)
    l_scr[...] = alpha * l_scr[...] + jnp.sum(p, axis=1, keepdims=True)
    pb = p.astype(BF16)
    pv = jnp.zeros((n_sub, d), F32)
    for i in range(pp):
        pv = pv + lax.dot_general(pb[:, i * page:(i + 1) * page], v_refs[i][...].astype(BF16),
                                  (((1,), (1,)), ((), ())), preferred_element_type=F32)
    acc_scr[...] = alpha * acc_scr[...] + pv
    m_scr[...] = m_new

    if last:
        o = acc_scr[...] / l_scr[...]
        head = col // DH_V
        a0 = jnp.sum(jnp.where(row == 2 * head, o, 0.0), axis=0, keepdims=True)
        a1 = jnp.sum(jnp.where(row == 2 * head + 1, o, 0.0), axis=0, keepdims=True)
        att = a0 - lam * a1
        sq = att * att
        head1 = head[0:1]
        r = jnp.zeros_like(att)
        for hh in range(d // DH_V):
            ms = jnp.sum(jnp.where(head1 == hh, sq, 0.0), axis=1, keepdims=True) * (1.0 / DH_V)
            r = jnp.where(head1 == hh, lax.rsqrt(ms + EPS), r)
        o_ref[...] = att * r * sg_ref[...] * (1.0 - lam_init)


def _attention_kernel(pt_ref, cfar_ref, lam_ref, sgc_ref, qt_ref, k_ref, vt_ref, bias_ref,
                      q_ref, kn_ref, vn_ref, bt_ref, b0_ref, sgr_ref, *rest, pp, tiles, lam_init, tq, layer):
    ck_hbm, cv_hbm, o_ref, od_ref, p_scr, m_scr, l_scr, acc_scr, kbuf, vbuf, sem = rest
    hp = pl.program_id(1)
    half = pl.program_id(2)
    step = (pl.program_id(0) * pl.num_programs(1) + hp) * pl.num_programs(2) + half
    last_step = pl.num_programs(0) * pl.num_programs(1) * pl.num_programs(2) - 1
    slot = step % 2

    def page_copies(of_step, into):
        copies = []
        for i in range(pp):
            pid = pt_ref[of_step * pp + i]
            copies.append(pltpu.make_async_copy(ck_hbm.at[layer, pid], kbuf.at[into, i], sem.at[into, 0]))
            copies.append(pltpu.make_async_copy(cv_hbm.at[layer, pid], vbuf.at[into, i], sem.at[into, 1]))
        return copies

    @pl.when(step == 0)
    def _():
        for c in page_copies(0, 0):
            c.start()

    lam = _lambda(lam_ref, lam_init)
    nxt = jnp.minimum(step + 1, last_step)
    for hv in range(2):
        @pl.when(half == hv)
        def _(hv=hv):
            for c in page_copies(nxt, 1 - slot):
                c.start()
            _prompt_tiles(tiles[hv], hp, lam, cfar_ref, sgc_ref, qt_ref, k_ref, vt_ref, bias_ref, o_ref, p_scr,
                          lam_init=lam_init, tq=tq)
            for c in page_copies(step, slot):
                c.wait()
            k_refs = [kbuf.at[slot, i] for i in range(pp)]
            v_refs = [vbuf.at[slot, i] for i in range(pp)]
            _decode_pages(hv == 0, hv == 1, lam, q_ref, kn_ref, vn_ref, bt_ref[...], b0_ref, sgr_ref,
                          k_refs, v_refs, od_ref, m_scr, l_scr, acc_scr, lam_init=lam_init)

    @pl.when(step == last_step)
    def _():
        for c in page_copies(last_step, 1 - slot):
            c.wait()


def _attention(page_table, cfar, lam_p, sg, qt, k, vt, bias_tiles, q, kn, vn, bias_t, bias0,
               cache_kt, cache_vt, layer, n_seq, seq, lam_init):
    d_qk, m = qt.shape
    n_dec, n_pages = page_table.shape
    d = q.shape[-1]
    page = cache_kt.shape[-1]
    pair = 2 * 2 * DH_QK
    n_pair = d_qk // pair
    nq = seq // TQ
    order = sorted(range(nq), reverse=True)
    tiles = ([n for i, n in enumerate(order) if i % 4 in (0, 3)], [n for i, n in enumerate(order) if i % 4 in (1, 2)])
    halves = 2
    pp = n_pages // halves
    assert n_pages % halves == 0 and n_seq * n_pair == n_dec, "one sample sequence per (prompt sequence, head pair)"
    kern = functools.partial(_attention_kernel, pp=pp, tiles=tiles, lam_init=lam_init, tq=TQ, layer=layer)
    const = lambda shape: pl.BlockSpec(shape, lambda b, h, s, pt: (0,) * len(shape))
    tok = pl.BlockSpec((None, 1, d), lambda b, h, s, pt: (b * n_pair + h, 0, 0))
    grid_spec = pltpu.PrefetchScalarGridSpec(
        num_scalar_prefetch=1,
        grid=(n_seq, n_pair, halves),
        in_specs=[pl.BlockSpec(memory_space=pltpu.SMEM),
                  const(lam_p.shape),
                  const((pair, 1)),
                  pl.BlockSpec((pair, seq), lambda b, h, s, pt: (h, b)),
                  pl.BlockSpec((seq, pair), lambda b, h, s, pt: (b, h)),
                  pl.BlockSpec((pair, seq), lambda b, h, s, pt: (h, b)),
                  pl.BlockSpec((2, 2, TQ, TQ), lambda b, h, s, pt: (h, 0, 0, 0)),
                  tok, tok, tok,
                  pl.BlockSpec((d // DH_QK, pp * page), lambda b, h, s, pt: (0, s)),
                  const((d // DH_QK, 1)),
                  const((1, d)),
                  pl.BlockSpec(memory_space=pl.ANY), pl.BlockSpec(memory_space=pl.ANY)],
        out_specs=[pl.BlockSpec((seq, pair), lambda b, h, s, pt: (b, h)), tok],
        scratch_shapes=[pltpu.VMEM((2, seq, (pair // DH_QK) * TQ), BF16),
                        pltpu.VMEM((d // DH_QK, 1), F32), pltpu.VMEM((d // DH_QK, 1), F32),
                        pltpu.VMEM((d // DH_QK, d), F32),
                        pltpu.VMEM((2, pp, d, page), F32), pltpu.VMEM((2, pp, d, page), F32),
                        pltpu.SemaphoreType.DMA((2, 2))],
    )
    att_p, att_s = pl.pallas_call(
        kern,
        grid_spec=grid_spec,
        out_shape=[jax.ShapeDtypeStruct((m, d_qk), BF16), jax.ShapeDtypeStruct((n_dec, 1, d), F32)],
        compiler_params=_params(("arbitrary", "arbitrary", "arbitrary")),
        name="attention",
    )(page_table.reshape(-1), cfar, lam_p, sg[:pair].reshape(-1, 1), qt, k, vt, bias_tiles,
      q.reshape(n_dec, 1, d), kn.reshape(n_dec, 1, d), vn.reshape(n_dec, 1, d), bias_t, bias0, sg.reshape(1, -1),
      cache_kt, cache_vt)
    return att_p, att_s.reshape(n_dec, d)


def _layernorm_swish(c, g, b):
    mu = jnp.mean(c, axis=-1, keepdims=True)
    xc = c - mu
    y = xc * lax.rsqrt(jnp.mean(xc * xc, axis=-1, keepdims=True) + EPS) * g + b
    return jax.nn.silu(y)


def _odd_prompt_kernel(x_ref, g_ref, w1_ref, b1_ref, cw_ref, cb_ref, lg_ref, lb_ref, w2_ref, b2_ref,
                       o_ref, st_ref, u_scr, c_scr, *, tiles_per_seq, width):
    i = pl.program_id(0)
    tm, d = x_ref.shape
    nl = d // LANE
    x = x_ref[...]
    h = _rms(x, g_ref[...]).astype(BF16)
    z = _dot(h, w1_ref[...]) + b1_ref[...]
    u = z[:, :d] * jax.nn.sigmoid(z[:, d:])

    @pl.when(i % tiles_per_seq == 0)
    def _():
        u_scr[0:HALO * nl, :] = jnp.zeros((HALO * nl, LANE), F32)

    for l in range(nl):
        u_scr[pl.ds(HALO * nl + l, tm, stride=nl), :] = u[:, l * LANE:(l + 1) * LANE]

    @pl.when(i % tiles_per_seq == tiles_per_seq - 1)
    def _():
        st_ref[...] = u[tm - (width - 1):, :]

    first = HALO - (width - 1)
    taps = [cw_ref[j] for j in range(width)]

    def chunk(r, carry):
        t0 = r * CONV_UNROLL
        slab = lambda t: pl.ds(pl.multiple_of(t * nl, nl), nl)
        acc = [None] * CONV_UNROLL
        for k in range(CONV_UNROLL + width - 1):
            row = u_scr[slab(t0 + first + k), :]
            for o in range(max(0, k - width + 1), min(CONV_UNROLL, k + 1)):
                term = taps[k - o] * row
                acc[o] = term if acc[o] is None else acc[o] + term
        for o in range(CONV_UNROLL):
            c_scr[slab(t0 + o), :] = acc[o]
        return carry

    lax.fori_loop(0, tm // CONV_UNROLL, chunk, 0)

    u_scr[0:HALO * nl, :] = u_scr[tm * nl:(tm + HALO) * nl, :]
    c = jnp.concatenate([c_scr[pl.ds(l, tm, stride=nl), :] for l in range(nl)], axis=1)
    a = _layernorm_swish(c + cb_ref[...], lg_ref[...], lb_ref[...]).astype(BF16)
    o_ref[...] = x + _dot(a, w2_ref[...]) + b2_ref[...]


def _odd_prompt(x, g, w1, w2, layer, b1, cw, cb, lg, lb, b2, n_seq, seq):
    m, d = x.shape
    width = cw.shape[0]
    tm = TM_ODD
    tps = seq // tm
    row = pl.BlockSpec((tm, d), lambda i: (i, 0))
    vec = lambda a: a.reshape(1, -1)
    nl = d // LANE
    assert tm >= width - 1 and tm % CONV_UNROLL == 0 and HALO >= width - 1
    kern = functools.partial(_odd_prompt_kernel, tiles_per_seq=tps, width=width)
    return pl.pallas_call(
        kern,
        grid=(m // tm,),
        in_specs=[row, _resident((1, d)), _resident_layer(w1, layer), _resident((1, 2 * d)),
                  _resident((width, nl, LANE)),
                  _resident((1, d)), _resident((1, d)), _resident((1, d)), _resident_layer(w2, layer),
                  _resident((1, d))],
        out_specs=[row, pl.BlockSpec((None, width - 1, d), lambda i: (i // tps, 0, 0))],
        out_shape=[jax.ShapeDtypeStruct((m, d), F32),
                   jax.ShapeDtypeStruct((n_seq, width - 1, d), F32)],
        scratch_shapes=[pltpu.VMEM(((tm + HALO) * nl, LANE), F32), pltpu.VMEM((tm * nl, LANE), F32)],
        compiler_params=_params(("arbitrary",)),
        name="odd_prompt",
    )(x, vec(g), w1, vec(b1), cw.reshape(width, nl, LANE), vec(cb), vec(lg), vec(lb), w2, vec(b2))


def _odd_sample_kernel(x_ref, g_ref, w1_ref, b1_ref, cw_ref, cb_ref, lg_ref, lb_ref, w2_ref, b2_ref,
                       st_ref, o_ref, nst_ref):
    n_hist = st_ref.shape[0]
    d = x_ref.shape[1]
    x = x_ref[...]
    h = _rms(x, g_ref[...]).astype(BF16)
    z = _dot(h, w1_ref[...]) + b1_ref[...]
    u = z[:, :d] * jax.nn.sigmoid(z[:, d:])
    c = cw_ref[n_hist:n_hist + 1, :] * u
    for j in range(n_hist):
        c = c + cw_ref[j:j + 1, :] * st_ref[j]
    for j in range(n_hist - 1):
        nst_ref[j] = st_ref[j + 1]
    nst_ref[n_hist - 1] = u
    a = _layernorm_swish(c + cb_ref[...], lg_ref[...], lb_ref[...]).astype(BF16)
    o_ref[...] = x + _dot(a, w2_ref[...]) + b2_ref[...]


def _odd_sample(x, g, w1, w2, layer, b1, cw, cb, lg, lb, b2, state_t):
    m, d = x.shape
    vec = lambda a: a.reshape(1, -1)
    whole = lambda shape: pl.BlockSpec(shape, lambda i: (0,) * len(shape))
    return pl.pallas_call(
        _odd_sample_kernel,
        grid=(1,),
        in_specs=[_resident((m, d)), _resident((1, d)), _resident_layer(w1, layer), _resident((1, 2 * d)),
                  _resident(cw.shape), _resident((1, d)), _resident((1, d)), _resident((1, d)),
                  _resident_layer(w2, layer), _resident((1, d)), _resident(state_t.shape)],
        out_specs=[whole((m, d)), whole(state_t.shape)],
        out_shape=[jax.ShapeDtypeStruct((m, d), F32), jax.ShapeDtypeStruct(state_t.shape, F32)],
        compiler_params=_params(("arbitrary",)),
        name="odd_sample",
    )(x, vec(g), w1, vec(b1), cw, vec(cb), vec(lg), vec(lb), w2, vec(b2), state_t)


def _bias_by_distance(rel_bias, dist):
    n = jnp.maximum(dist, 0)
    nf = jnp.maximum(n, 1).astype(F32)
    large = MAX_EXACT + (jnp.log(nf / MAX_EXACT) / math.log(MAX_DISTANCE / MAX_EXACT)
                         * (N_BUCKETS - MAX_EXACT)).astype(jnp.int32)
    large = jnp.minimum(large, N_BUCKETS - 1)
    return rel_bias[jnp.where(n < MAX_EXACT, n, large)]


def _prompt_bias_tiles(by_dist_nat):
    assert TQ >= FAR - 1
    n_head = by_dist_nat.shape[0]
    by_dist = by_dist_nat * LOG2E
    pad = jnp.zeros((n_head, 1), F32)
    diag = jnp.concatenate([jnp.full((n_head, TQ - 1), NEG, F32), by_dist[:, :TQ], pad], axis=1)
    near = jnp.concatenate([by_dist[:, 1:], pad], axis=1)

    def toeplitz(a):
        period = a.shape[1]
        rows = jnp.tile(a, (1, TQ))[:, :TQ * (period - 1)].reshape(n_head, TQ, period - 1)
        return rows[:, :, TQ - 1:]

    tiles = jnp.stack([toeplitz(near), toeplitz(diag)], axis=1)
    return tiles, by_dist[:, TQ + 1]


def _decode_bias(by_dist_nat, past_len):
    n_head, n_tab = by_dist_nat.shape
    assert n_tab > FAR and past_len + 1 >= n_tab
    far = jnp.broadcast_to(by_dist_nat[:, FAR:FAR + 1], (n_head, past_len + 1 - n_tab))
    return jnp.repeat(jnp.concatenate([far, by_dist_nat[:, ::-1]], axis=1), 2, axis=0)


def kernel(x_prompt, x_sample, cache_k, cache_v, state_conv_b, state_conv_c, page_table, rel_bias, norm_ffn1, ffn1_w_gate, ffn1_w_up, ffn1_w_down, norm_mix, norm_ffn2, ffn2_w_gate, ffn2_w_up, ffn2_w_down, w_in_even, w_out_even, q_norm_gain, k_norm_gain, lambda_q1, lambda_k1, lambda_q2, lambda_k2, subln_gain, conv_b_w, w_pw1, b_pw1, conv_c_w, conv_c_b, ln_c_gain, ln_c_bias, w_pw2, b_pw2):
    bp, tp, d = x_prompt.shape
    bs = x_sample.shape[0]
    depth = norm_ffn1.shape[0]
    n_even, n_phys, page, n_sub, dh_qk = cache_k.shape
    assert dh_qk == DH_QK and cache_v.shape[-1] == DH_V and x_sample.shape[1] == 1
    d_qk = n_sub * dh_qk
    d_a = cache_v.shape[3] * DH_V
    past_len = page_table.shape[1] * page
    assert tp % TM_ROWS == 0 and tp % TM_ODD == 0 and tp % TQ == 0 and tp >= 2 * TQ

    bf = lambda w: w.astype(BF16)
    xp = x_prompt.reshape(bp * tp, d)
    xs = x_sample.reshape(bs, d)

    cache_kt = jnp.transpose(cache_k, (0, 1, 3, 4, 2)).reshape(n_even, n_phys, d_qk, page)
    cache_vt = jnp.transpose(cache_v, (0, 1, 3, 4, 2)).reshape(n_even, n_phys, d_a, page)
    state_c_t = jnp.transpose(state_conv_c, (0, 2, 1, 3))

    p_avg = jnp.kron(jnp.eye(d_qk // DH_QK, dtype=F32), jnp.full((DH_QK, DH_QK), 1.0 / DH_QK, F32)).astype(BF16)
    by_dist_nat = _bias_by_distance(rel_bias, jnp.arange(2 * TQ)).T
    bias_tiles, bias_far = _prompt_bias_tiles(by_dist_nat)
    bias_dec = _decode_bias(by_dist_nat, past_len)
    bias_dec_past, bias_dec_new = bias_dec[:, :past_len], bias_dec[:, past_len:]

    w_in, w_out, w1, w2 = (w.astype(BF16) for w in (w_in_even, w_out_even, w_pw1, w_pw2))
    ffn1_f32 = (ffn1_w_gate, ffn1_w_up, ffn1_w_down)
    ffn2_f32 = (ffn2_w_gate, ffn2_w_up, ffn2_w_down)
    ffn_w = tuple(w[0].astype(BF16) for w in ffn1_f32)

    kv_prompt = ()
    ks_l, vs_l, cbp_l, cbs_l, ccp_l, ccs_l = [], [], [], [], [], []
    for li in range(depth):
        xp, xs, ffn_w = _ffn(xp, xs, norm_ffn1[li], ffn_w, cast=(*ffn2_f32, li))
        if li % 2 == 0:
            e = li // 2
            lam_init = 0.8 - 0.6 * math.exp(-0.3 * li)
            qg = jnp.tile(q_norm_gain[e], d_qk // DH_QK).reshape(1, d_qk)
            kg = jnp.tile(k_norm_gain[e], d_qk // DH_QK).reshape(1, d_qk)
            lam_p = jnp.stack([lambda_q1[e], lambda_k1[e], lambda_q2[e], lambda_k2[e]])
            sg = jnp.tile(subln_gain[e], d_a // DH_V)

            *kv_prompt, qtb, kb, vtb, gcp, cbp = _even_in_prompt(
                xp, norm_mix[li], w_in, e, n_even, tuple(kv_prompt), p_avg, qg, kg, conv_b_w[e], bp, tp)
            qs, kn, vn, gcs, us = _even_in_sample(
                xs, norm_mix[li], w_in, e, p_avg, qg, kg, conv_b_w[e],
                state_conv_b[e, :, 0], state_conv_b[e, :, 1])
            attp, atts = _attention(page_table, bias_far, lam_p, sg, qtb, kb, vtb, bias_tiles,
                                    qs, kn, vn, bias_dec_past, bias_dec_new, cache_kt, cache_vt, e,
                                    bp, tp, lam_init)
            mixer = (attp, gcp, atts, gcs, w_out, e)

            ks_l.append(kn)
            vs_l.append(vn)
            cbp_l.append(cbp)
            cbs_l.append(jnp.stack([state_conv_b[e, :, 1], us], axis=1))
        else:
            o = li // 2
            prm = (norm_mix[li], w1, w2, o, b_pw1[o], conv_c_w[o], conv_c_b[o], ln_c_gain[o], ln_c_bias[o], b_pw2[o])
            xp, ccp = _odd_prompt(xp, *prm, bp, tp)
            xs, ccs = _odd_sample(xs, *prm, state_c_t[o])
            ccp_l.append(ccp)
            ccs_l.append(ccs)
            mixer = ()
        xp, xs, ffn_w = _ffn(xp, xs, norm_ffn2[li], ffn_w, mixer=mixer,
                             cast=(*ffn1_f32, li + 1) if li + 1 < depth else ())

    n_h = d_a // DH_V
    kt_all, vt_all = kv_prompt
    new_k_prompt = jnp.transpose(kt_all.reshape(n_even, bp, n_sub, DH_QK, tp), (0, 1, 4, 2, 3))
    new_v_prompt = jnp.transpose(vt_all.reshape(n_even, bp, n_h, DH_V, tp), (0, 1, 4, 2, 3))
    return (xp.reshape(bp, tp, d), xs.reshape(bs, 1, d),
            new_k_prompt, new_v_prompt,
            jnp.stack(ks_l).reshape(n_even, bs, 1, n_sub, DH_QK),
            jnp.stack(vs_l).reshape(n_even, bs, 1, n_h, DH_V),
            jnp.stack(cbp_l), jnp.stack(cbs_l), jnp.stack(ccp_l),
            jnp.transpose(jnp.stack(ccs_l), (0, 2, 1, 3)))
```

```python
import functools
import math

import jax
import jax.numpy as jnp
from jax import lax
from jax.experimental import pallas as pl
from jax.experimental.pallas import tpu as pltpu

F32 = jnp.float32
BF16 = jnp.bfloat16
EPS = 1e-6
NEG = -1e30

LANE = 128
SUBLANE = 8
MXU_DEPTH = 256
VMEM_LIMIT_BYTES = 56 * 1024 * 1024

DH_QK = 32
DH_V = 64
N_BUCKETS = 32
MAX_EXACT = N_BUCKETS // 2
MAX_DISTANCE = 128
FAR = MAX_DISTANCE

TM_ROWS = 1024
TM_FFN = 512
FFN_CHUNKS = 2
CAST_STEPS = 16
TM_ODD = 1024
TQ = 256
SOFTMAX_SUM_ROWS = 16
HALO = 32
CONV_UNROLL = 16
LOG2E = math.log2(math.e)


def _params(sem):
    return pltpu.CompilerParams(dimension_semantics=sem, vmem_limit_bytes=VMEM_LIMIT_BYTES)


def _resident(shape):
    n = len(shape)
    return pl.BlockSpec(shape, lambda *_: (0,) * n, pipeline_mode=pl.Buffered(1))


def _resident_layer(stacked, layer):
    n = stacked.ndim - 1
    return pl.BlockSpec((None,) + stacked.shape[1:], lambda *_: (layer,) + (0,) * n,
                        pipeline_mode=pl.Buffered(1))


def _rms(x, g):
    return x * lax.rsqrt(jnp.mean(x * x, axis=-1, keepdims=True) + EPS) * g


def _dot(a, b):
    return jnp.dot(a, b, preferred_element_type=F32)


def _group_rms(z, p_ref, gain):
    sq = z * z
    hi = sq.astype(BF16)
    lo = (sq - hi.astype(F32)).astype(BF16)
    ms = _dot(hi, p_ref[...]) + _dot(lo, p_ref[...])
    return z * lax.rsqrt(ms + EPS) * gain


def _swiglu_rows(x, mix, g_ref, wg_ref, wu_ref, wd_ref, wo_ref, n_chunks):
    if mix:
        att, gc = mix
        d_a = att.shape[1]
        x = x + _dot(att.astype(BF16), wo_ref[0:d_a, :]) + _dot(gc, wo_ref[d_a:, :])
    h = _rms(x, g_ref[...]).astype(BF16)
    n_pass = wg_ref.shape[1] // MXU_DEPTH
    cuts = [MXU_DEPTH * ((n_pass * c + n_chunks - 1) // n_chunks) for c in range(n_chunks)] + [wg_ref.shape[1]]
    y = jnp.zeros(x.shape, F32)
    for lo, hi in zip(cuts[:-1], cuts[1:]):
        gate = _dot(h, wg_ref[:, lo:hi])
        up = _dot(h, wu_ref[:, lo:hi])
        a = (jax.nn.silu(gate) * up).astype(BF16)
        y = y + _dot(a, wd_ref[lo:hi, :])
    return x + 0.5 * y


def _ffn_kernel(x_ref, g_ref, wg_ref, wu_ref, wd_ref, *rest, n_chunks, mixer, n_cast):
    rest = list(rest)
    take = lambda n: [rest.pop(0) for _ in range(n)]
    a_ref, gc_ref, wo_ref = take(3) if mixer else (None, None, None)
    xs_ref, = take(1)
    as_ref, gcs_ref = take(2) if mixer else (None, None)
    cast_src = take(n_cast)
    o_ref, os_ref = take(2)
    cast_dst = take(n_cast)
    step = pl.program_id(0)
    if n_cast:
        @pl.when(step < CAST_STEPS)
        def _():
            for src, dst in zip(cast_src, cast_dst):
                dst[...] = src[...].astype(BF16)

    @pl.when(step == 0)
    def _():
        mix_s = (as_ref[...], gcs_ref[...]) if mixer else ()
        os_ref[...] = _swiglu_rows(xs_ref[...], mix_s, g_ref, wg_ref, wu_ref, wd_ref, wo_ref, n_chunks)

    mix = (a_ref[...], gc_ref[...]) if mixer else ()
    o_ref[...] = _swiglu_rows(x_ref[...], mix, g_ref, wg_ref, wu_ref, wd_ref, wo_ref, n_chunks)


def _ffn(x, xs, g, weights, mixer=(), cast=()):
    m, d = x.shape
    tm = TM_FFN
    steps = m // tm
    row = lambda width: pl.BlockSpec((tm, width), lambda i: (i, 0))
    if mixer:
        att, gc, att_s, gc_s, w_out, out_layer = mixer
        mixer_ops = [att, gc, w_out]
        mixer_specs = [row(att.shape[1]), row(gc.shape[1]), _resident_layer(w_out, out_layer)]
        side_ops = [xs, att_s, gc_s]
    else:
        mixer_ops, mixer_specs, side_ops = [], [], [xs]
    side_specs = [_resident(a.shape) for a in side_ops]
    *stacks, cast_layer = cast if cast else (None,)
    assert not stacks or steps >= CAST_STEPS
    slab = lambda i: jnp.minimum(i, CAST_STEPS - 1)
    cast_in = [pl.BlockSpec((None, w.shape[1] // CAST_STEPS, w.shape[2]), lambda i: (cast_layer, slab(i), 0))
               for w in stacks]
    cast_out = [pl.BlockSpec((w.shape[1] // CAST_STEPS, w.shape[2]), lambda i: (slab(i), 0)) for w in stacks]
    out = pl.pallas_call(
        functools.partial(_ffn_kernel, n_chunks=FFN_CHUNKS, mixer=bool(mixer), n_cast=len(stacks)),
        grid=(steps,),
        in_specs=[row(d), _resident((1, d))] + [_resident(w.shape) for w in weights]
                 + mixer_specs + side_specs + cast_in,
        out_specs=[row(d), pl.BlockSpec(xs.shape, lambda i: (0, 0))] + cast_out,
        out_shape=[jax.ShapeDtypeStruct((m, d), F32), jax.ShapeDtypeStruct(xs.shape, F32)]
                  + [jax.ShapeDtypeStruct(w.shape[1:], BF16) for w in stacks],
        compiler_params=_params(("arbitrary",)),
        name="ffn_mixer" if mixer else "ffn",
    )(x, g.reshape(1, d), *weights, *mixer_ops, *side_ops, *stacks)
    return out[0], out[1], tuple(out[2:])


def _even_in_prompt_kernel(x_ref, g_ref, w_ref, p_ref, qg_ref, kg_ref, cw_ref, *rest,
                           tiles_per_seq, d_qk, d_a, d_b, scale, layer):
    kt_ref, vt_ref, qtb_ref, kb_ref, vtb_ref, gc_ref, st_ref, u_scr = rest[-8:]
    i = pl.program_id(0)
    tm = x_ref.shape[0]
    h = _rms(x_ref[...], g_ref[...]).astype(BF16)
    z = _dot(h, w_ref[...])
    q = _group_rms(z[:, :d_qk], p_ref, qg_ref[...]) * scale
    k = _group_rms(z[:, d_qk:2 * d_qk], p_ref, kg_ref[...])
    o = 2 * d_qk
    v = z[:, o:o + d_a]
    o += d_a
    gate_b = z[:, o:o + d_b]
    u = z[:, o + d_b:o + 2 * d_b] * z[:, o + 2 * d_b:o + 3 * d_b]

    kt = k.T
    vt = v.T
    for ref, val in ((kt_ref, kt), (vt_ref, vt)):
        if len(ref.shape) == 2:
            ref[...] = val
        else:
            for slot in range(ref.shape[0]):
                ref[slot] = val if slot == layer else jnp.zeros_like(val)
    qtb_ref[...] = q.T.astype(BF16)
    kb_ref[...] = k.astype(BF16)
    vtb_ref[...] = vt.astype(BF16)

    @pl.when(i % tiles_per_seq == 0)
    def _():
        u_scr[0:SUBLANE, :] = jnp.zeros((SUBLANE, d_b), F32)

    u_scr[SUBLANE:SUBLANE + tm, :] = u
    conv = (cw_ref[0:1, :] * u_scr[SUBLANE - 2:SUBLANE - 2 + tm, :]
            + cw_ref[1:2, :] * u_scr[SUBLANE - 1:SUBLANE - 1 + tm, :]
            + cw_ref[2:3, :] * u)
    gc_ref[...] = (gate_b * conv).astype(BF16)
    st_ref[...] = u_scr[tm + SUBLANE - 2:tm + SUBLANE, :]
    u_scr[0:SUBLANE, :] = u_scr[tm:tm + SUBLANE, :]


def _even_in_prompt(x, g, w_in, layer, n_layers, kv_prev, p_avg, qg, kg, cw, n_seq, seq):
    m, d = x.shape
    d_qk = d_a = d_b = p_avg.shape[0]
    tm = TM_ROWS
    tps = seq // tm
    row = lambda width, dt=None: pl.BlockSpec((tm, width), lambda i: (i, 0))
    col = pl.BlockSpec((d_qk, tm), lambda i: (0, i))
    if kv_prev:
        seq_t = pl.BlockSpec((None, None, d_qk, tm), lambda i: (layer, i // tps, 0, i % tps))
    else:
        seq_t = pl.BlockSpec((n_layers, None, d_qk, tm), lambda i: (0, i // tps, 0, i % tps))
    n_in = 7
    kern = functools.partial(_even_in_prompt_kernel, tiles_per_seq=tps, d_qk=d_qk, d_a=d_a, d_b=d_b, layer=layer,
                             scale=DH_QK ** -0.5 * LOG2E)
    return pl.pallas_call(
        kern,
        grid=(m // tm,),
        in_specs=[row(d), _resident((1, d)), _resident_layer(w_in, layer), _resident(p_avg.shape),
                  _resident((1, d_qk)), _resident((1, d_qk)), _resident(cw.shape)]
                 + [pl.BlockSpec(memory_space=pl.ANY)] * len(kv_prev),
        out_specs=[seq_t, seq_t, col, row(d_qk), col, row(d_b),
                   pl.BlockSpec((None, 2, d_b), lambda i: (i // tps, 0, 0))],
        out_shape=[jax.ShapeDtypeStruct((n_layers, n_seq, d_qk, seq), F32),
                   jax.ShapeDtypeStruct((n_layers, n_seq, d_a, seq), F32),
                   jax.ShapeDtypeStruct((d_qk, m), BF16),
                   jax.ShapeDtypeStruct((m, d_qk), BF16),
                   jax.ShapeDtypeStruct((d_a, m), BF16),
                   jax.ShapeDtypeStruct((m, d_b), BF16),
                   jax.ShapeDtypeStruct((n_seq, 2, d_b), F32)],
        scratch_shapes=[pltpu.VMEM((tm + SUBLANE, d_b), F32)],
        input_output_aliases={n_in + j: j for j in range(len(kv_prev))},
        compiler_params=_params(("arbitrary",)),
        name="even_in_prompt",
    )(x, g.reshape(1, d), w_in, p_avg, qg, kg, cw, *kv_prev)


def _even_in_sample_kernel(x_ref, g_ref, w_ref, p_ref, qg_ref, kg_ref, cw_ref, s0_ref, s1_ref,
                           q_ref, k_ref, v_ref, gc_ref, u_ref, *, d_qk, d_a, d_b, scale):
    h = _rms(x_ref[...], g_ref[...]).astype(BF16)
    z = _dot(h, w_ref[...])
    q_ref[...] = _group_rms(z[:, :d_qk], p_ref, qg_ref[...]) * scale
    k_ref[...] = _group_rms(z[:, d_qk:2 * d_qk], p_ref, kg_ref[...])
    o = 2 * d_qk
    v_ref[...] = z[:, o:o + d_a]
    o += d_a
    gate_b = z[:, o:o + d_b]
    u = z[:, o + d_b:o + 2 * d_b] * z[:, o + 2 * d_b:o + 3 * d_b]
    conv = cw_ref[0:1, :] * s0_ref[...] + cw_ref[1:2, :] * s1_ref[...] + cw_ref[2:3, :] * u
    gc_ref[...] = (gate_b * conv).astype(BF16)
    u_ref[...] = u


def _even_in_sample(x, g, w_in, layer, p_avg, qg, kg, cw, s0, s1):
    m, d = x.shape
    d_qk = d_a = d_b = p_avg.shape[0]
    kern = functools.partial(_even_in_sample_kernel, d_qk=d_qk, d_a=d_a, d_b=d_b, scale=DH_QK ** -0.5)
    f = lambda dt: jax.ShapeDtypeStruct((m, d_qk), dt)
    blk = _resident((m, d_qk))
    return pl.pallas_call(
        kern,
        grid=(1,),
        in_specs=[_resident((m, d)), _resident((1, d)), _resident_layer(w_in, layer), _resident(p_avg.shape),
                  _resident((1, d_qk)), _resident((1, d_qk)), _resident(cw.shape), blk, blk],
        out_specs=[pl.BlockSpec((m, d_qk), lambda i: (0, 0))] * 5,
        out_shape=[f(F32), f(F32), f(F32), f(BF16), f(F32)],
        compiler_params=_params(("arbitrary",)),
        name="even_in_sample",
    )(x, g.reshape(1, d), w_in, p_avg, qg, kg, cw, s0, s1)


def _lambda(lam_ref, lam_init):
    lam_p = lam_ref[...]
    return (jnp.exp(jnp.sum(lam_p[0:1] * lam_p[1:2], axis=1, keepdims=True))
            - jnp.exp(jnp.sum(lam_p[2:3] * lam_p[3:4], axis=1, keepdims=True)) + lam_init)


def _prompt_tiles(tiles, hp, lam, cfar_ref, sg_ref, qt_ref, k_ref, vt_ref, bias_ref, o_ref, p_scr, *,
                  lam_init, tq):
    n_sub = qt_ref.shape[0] // DH_QK
    n_head = n_sub // 2
    cfar = jnp.concatenate([jnp.full((1, 2 * tq), cfar_ref[hp * n_head + h], F32) for h in range(n_head)],
                           axis=1)
    grp = lax.broadcasted_iota(jnp.int32, (qt_ref.shape[0], tq), 0) // DH_QK

    def scores(n):
        qt = qt_ref[:, n * tq:(n + 1) * tq]
        q_blk = jnp.concatenate([jnp.where(grp == c, qt, jnp.zeros_like(qt)) for c in range(n_sub)], axis=1)
        return _dot(k_ref[0:(n + 1) * tq, :], q_blk)

    s_next = scores(tiles[0])
    for idx, n in enumerate(tiles):
        s_all = s_next
        if idx + 1 < len(tiles):
            s_next = scores(tiles[idx + 1])
        p_buf = p_scr.at[idx % 2]

        m = None
        s_tiles = []
        for t in range(n + 1):
            s = s_all[t * tq:(t + 1) * tq, :]
            if t >= n - 1:
                kind = 1 if t == n else 0
                s = jnp.concatenate([s[:, c * tq:(c + 1) * tq] + bias_ref[c // 2, kind] for c in range(n_sub)],
                                    axis=1)
                smax = jnp.max(s, axis=0, keepdims=True)
            else:
                smax = jnp.max(s, axis=0, keepdims=True) + cfar
            m = smax if m is None else jnp.maximum(m, smax)
            s_tiles.append(s)

        m_far = m - cfar
        for t in range(n + 1):
            shift = m if t >= n - 1 else m_far
            if t < n:
                p_buf[t * tq:(t + 1) * tq, :] = jnp.exp2(s_tiles[t] - shift).astype(BF16)
                continue
            half = tq // 2
            lo = t * tq
            p_buf[lo:lo + half, :] = jnp.exp2(s_tiles[t][:half] - shift).astype(BF16)
            for c in range(n_sub):
                dead = slice(c * tq, c * tq + half)
                live = slice(c * tq + half, (c + 1) * tq)
                p_buf[lo + half:lo + tq, dead] = jnp.zeros((half, half), BF16)
                p_buf[lo + half:lo + tq, live] = jnp.exp2(s_tiles[t][half:, live] - shift[:, live]).astype(BF16)

        keys = (n + 1) * tq
        ones = jnp.ones((SOFTMAX_SUM_ROWS, keys), BF16)
        outs = []
        for h in range(n_head):
            vte = jnp.concatenate([vt_ref[h * DH_V:(h + 1) * DH_V, 0:keys], ones], axis=0)
            pv = _dot(vte, p_buf[0:keys, h * 2 * tq:(h + 1) * 2 * tq])
            sub = [pv[0:DH_V, c * tq:(c + 1) * tq] / pv[DH_V:DH_V + 1, c * tq:(c + 1) * tq] for c in range(2)]
            o = sub[0] - lam * sub[1]
            ms = jnp.mean(o * o, axis=0, keepdims=True)
            outs.append(o * lax.rsqrt(ms + EPS))
        o = jnp.concatenate(outs, axis=0) * sg_ref[...] * (1.0 - lam_init)
        o_ref[n * tq:(n + 1) * tq, :] = o.T.astype(BF16)


def _decode_pages(first, last, lam, q_ref, kn_ref, vn_ref, bias, b0_ref, sg_ref, k_refs, v_refs, o_ref,
                  m_scr, l_scr, acc_scr, *, lam_init):
    pp = len(k_refs)
    n_sub, d = acc_scr.shape
    page = k_refs[0].shape[1]

    row = lax.broadcasted_iota(jnp.int32, (n_sub, d), 0)
    col = lax.broadcasted_iota(jnp.int32, (n_sub, d), 1)
    qexp = jnp.where(col // DH_QK == row, jnp.broadcast_to(q_ref[...], (n_sub, d)), 0.0)

    if first:
        m_scr[...] = jnp.sum(qexp * kn_ref[...], axis=1, keepdims=True) + b0_ref[...]
        l_scr[...] = jnp.ones(l_scr.shape, F32)
        acc_scr[...] = jnp.broadcast_to(vn_ref[...], (n_sub, d))

    qb = qexp.astype(BF16)
    s = jnp.concatenate([_dot(qb, k_refs[i][...].astype(BF16)) for i in range(pp)], axis=1)
    s = s + bias
    m_old = m_scr[...]
    m_new = jnp.maximum(m_old, jnp.max(s, axis=1, keepdims=True))
    p = jnp.exp(s - m_new)
    alpha = jnp.exp(m_old - m_new)
    l_scr[...] = alpha * l_scr[...] + jnp.sum(p, axis=1, keepdims=True)
    pb = p.astype(BF16)
    pv = jnp.zeros((n_sub, d), F32)
    for i in range(pp):
        pv = pv + lax.dot_general(pb[:, i * page:(i + 1) * page], v_refs[i][...].astype(BF16),
                                  (((1,), (1,)), ((), ())), preferred_element_type=F32)
    acc_scr[...] = alpha * acc_scr[...] + pv
    m_scr[...] = m_new

    if last:
        o = acc_scr[...] / l_scr[...]
        head = col // DH_V
        a0 = jnp.sum(jnp.where(row == 2 * head, o, 0.0), axis=0, keepdims=True)
        a1 = jnp.sum(jnp.where(row == 2 * head + 1, o, 0.0), axis=0, keepdims=True)
        att = a0 - lam * a1
        sq = att * att
        head1 = head[0:1]
        r = jnp.zeros_like(att)
        for hh in range(d // DH_V):
            ms = jnp.sum(jnp.where(head1 == hh, sq, 0.0), axis=1, keepdims=True) * (1.0 / DH_V)
            r = jnp.where(head1 == hh, lax.rsqrt(ms + EPS), r)
        o_ref[...] = att * r * sg_ref[...] * (1.0 - lam_init)


def _attention_kernel(pt_ref, cfar_ref, lam_ref, sgc_ref, qt_ref, k_ref, vt_ref, bias_ref,
                      q_ref, kn_ref, vn_ref, bt_ref, b0_ref, sgr_ref, *rest, pp, tiles, lam_init, tq, layer):
    ck_hbm, cv_hbm, o_ref, od_ref, p_scr, m_scr, l_scr, acc_scr, kbuf, vbuf, sem = rest
    hp = pl.program_id(1)
    half = pl.program_id(2)
    step = (pl.program_id(0) * pl.num_programs(1) + hp) * pl.num_programs(2) + half
    last_step = pl.num_programs(0) * pl.num_programs(1) * pl.num_programs(2) - 1
    slot = step % 2

    def page_copies(of_step, into):
        copies = []
        for i in range(pp):
            pid = pt_ref[of_step * pp + i]
            copies.append(pltpu.make_async_copy(ck_hbm.at[layer, pid], kbuf.at[into, i], sem.at[into, 0]))
            copies.append(pltpu.make_async_copy(cv_hbm.at[layer, pid], vbuf.at[into, i], sem.at[into, 1]))
        return copies

    @pl.when(step == 0)
    def _():
        for c in page_copies(0, 0):
            c.start()

    lam = _lambda(lam_ref, lam_init)
    nxt = jnp.minimum(step + 1, last_step)
    for hv in range(2):
        @pl.when(half == hv)
        def _(hv=hv):
            for j, c in enumerate(page_copies(nxt, 1 - slot)):
                c.start(priority=j % 2)
            _prompt_tiles(tiles[hv], hp, lam, cfar_ref, sgc_ref, qt_ref, k_ref, vt_ref, bias_ref, o_ref, p_scr,
                          lam_init=lam_init, tq=tq)
            for c in page_copies(step, slot):
                c.wait()
            k_refs = [kbuf.at[slot, i] for i in range(pp)]
            v_refs = [vbuf.at[slot, i] for i in range(pp)]
            _decode_pages(hv == 0, hv == 1, lam, q_ref, kn_ref, vn_ref, bt_ref[...], b0_ref, sgr_ref,
                          k_refs, v_refs, od_ref, m_scr, l_scr, acc_scr, lam_init=lam_init)

    @pl.when(step == last_step)
    def _():
        for c in page_copies(last_step, 1 - slot):
            c.wait()


def _attention(page_table, cfar, lam_p, sg, qt, k, vt, bias_tiles, q, kn, vn, bias_t, bias0,
               cache_kt, cache_vt, layer, n_seq, seq, lam_init):
    d_qk, m = qt.shape
    n_dec, n_pages = page_table.shape
    d = q.shape[-1]
    page = cache_kt.shape[-1]
    pair = 2 * 2 * DH_QK
    n_pair = d_qk // pair
    nq = seq // TQ
    order = sorted(range(nq), reverse=True)
    tiles = ([n for i, n in enumerate(order) if i % 4 in (0, 3)], [n for i, n in enumerate(order) if i % 4 in (1, 2)])
    halves = 2
    pp = n_pages // halves
    assert n_pages % halves == 0 and n_seq * n_pair == n_dec, "one sample sequence per (prompt sequence, head pair)"
    kern = functools.partial(_attention_kernel, pp=pp, tiles=tiles, lam_init=lam_init, tq=TQ, layer=layer)
    const = lambda shape: pl.BlockSpec(shape, lambda b, h, s, pt: (0,) * len(shape))
    tok = pl.BlockSpec((None, 1, d), lambda b, h, s, pt: (b * n_pair + h, 0, 0))
    grid_spec = pltpu.PrefetchScalarGridSpec(
        num_scalar_prefetch=1,
        grid=(n_seq, n_pair, halves),
        in_specs=[pl.BlockSpec(memory_space=pltpu.SMEM),
                  const(lam_p.shape),
                  const((pair, 1)),
                  pl.BlockSpec((pair, seq), lambda b, h, s, pt: (h, b)),
                  pl.BlockSpec((seq, pair), lambda b, h, s, pt: (b, h)),
                  pl.BlockSpec((pair, seq), lambda b, h, s, pt: (h, b)),
                  pl.BlockSpec((2, 2, TQ, TQ), lambda b, h, s, pt: (h, 0, 0, 0)),
                  tok, tok, tok,
                  pl.BlockSpec((d // DH_QK, pp * page), lambda b, h, s, pt: (0, s)),
                  const((d // DH_QK, 1)),
                  const((1, d)),
                  pl.BlockSpec(memory_space=pl.ANY), pl.BlockSpec(memory_space=pl.ANY)],
        out_specs=[pl.BlockSpec((seq, pair), lambda b, h, s, pt: (b, h)), tok],
        scratch_shapes=[pltpu.VMEM((2, seq, (pair // DH_QK) * TQ), BF16),
                        pltpu.VMEM((d // DH_QK, 1), F32), pltpu.VMEM((d // DH_QK, 1), F32),
                        pltpu.VMEM((d // DH_QK, d), F32),
                        pltpu.VMEM((2, pp, d, page), F32), pltpu.VMEM((2, pp, d, page), F32),
                        pltpu.SemaphoreType.DMA((2, 2))],
    )
    att_p, att_s = pl.pallas_call(
        kern,
        grid_spec=grid_spec,
        out_shape=[jax.ShapeDtypeStruct((m, d_qk), BF16), jax.ShapeDtypeStruct((n_dec, 1, d), F32)],
        compiler_params=_params(("arbitrary", "arbitrary", "arbitrary")),
        name="attention",
    )(page_table.reshape(-1), cfar, lam_p, sg[:pair].reshape(-1, 1), qt, k, vt, bias_tiles,
      q.reshape(n_dec, 1, d), kn.reshape(n_dec, 1, d), vn.reshape(n_dec, 1, d), bias_t, bias0, sg.reshape(1, -1),
      cache_kt, cache_vt)
    return att_p, att_s.reshape(n_dec, d)


def _layernorm_swish(c, g, b):
    mu = jnp.mean(c, axis=-1, keepdims=True)
    xc = c - mu
    y = xc * lax.rsqrt(jnp.mean(xc * xc, axis=-1, keepdims=True) + EPS) * g + b
    return jax.nn.silu(y)


def _odd_prompt_kernel(x_ref, g_ref, w1_ref, b1_ref, cw_ref, cb_ref, lg_ref, lb_ref, w2_ref, b2_ref,
                       o_ref, st_ref, u_scr, c_scr, *, tiles_per_seq, width):
    i = pl.program_id(0)
    tm, d = x_ref.shape
    nl = d // LANE
    x = x_ref[...]
    h = _rms(x, g_ref[...]).astype(BF16)
    z = _dot(h, w1_ref[...]) + b1_ref[...]
    u = z[:, :d] * jax.nn.sigmoid(z[:, d:])

    @pl.when(i % tiles_per_seq == 0)
    def _():
        u_scr[0:HALO * nl, :] = jnp.zeros((HALO * nl, LANE), F32)

    for l in range(nl):
        u_scr[pl.ds(HALO * nl + l, tm, stride=nl), :] = u[:, l * LANE:(l + 1) * LANE]

    @pl.when(i % tiles_per_seq == tiles_per_seq - 1)
    def _():
        st_ref[...] = u[tm - (width - 1):, :]

    first = HALO - (width - 1)
    taps = [cw_ref[j] for j in range(width)]

    def chunk(r, carry):
        t0 = r * CONV_UNROLL
        slab = lambda t: pl.ds(pl.multiple_of(t * nl, nl), nl)
        acc = [None] * CONV_UNROLL
        for k in range(CONV_UNROLL + width - 1):
            row = u_scr[slab(t0 + first + k), :]
            for o in range(max(0, k - width + 1), min(CONV_UNROLL, k + 1)):
                term = taps[k - o] * row
                acc[o] = term if acc[o] is None else acc[o] + term
        for o in range(CONV_UNROLL):
            c_scr[slab(t0 + o), :] = acc[o]
        return carry

    lax.fori_loop(0, tm // CONV_UNROLL, chunk, 0)

    u_scr[0:HALO * nl, :] = u_scr[tm * nl:(tm + HALO) * nl, :]
    c = jnp.concatenate([c_scr[pl.ds(l, tm, stride=nl), :] for l in range(nl)], axis=1)
    a = _layernorm_swish(c + cb_ref[...], lg_ref[...], lb_ref[...]).astype(BF16)
    o_ref[...] = x + _dot(a, w2_ref[...]) + b2_ref[...]


def _odd_prompt(x, g, w1, w2, layer, b1, cw, cb, lg, lb, b2, n_seq, seq):
    m, d = x.shape
    width = cw.shape[0]
    tm = TM_ODD
    tps = seq // tm
    row = pl.BlockSpec((tm, d), lambda i: (i, 0))
    vec = lambda a: a.reshape(1, -1)
    nl = d // LANE
    assert tm >= width - 1 and tm % CONV_UNROLL == 0 and HALO >= width - 1
    kern = functools.partial(_odd_prompt_kernel, tiles_per_seq=tps, width=width)
    return pl.pallas_call(
        kern,
        grid=(m // tm,),
        in_specs=[row, _resident((1, d)), _resident_layer(w1, layer), _resident((1, 2 * d)),
                  _resident((width, nl, LANE)),
                  _resident((1, d)), _resident((1, d)), _resident((1, d)), _resident_layer(w2, layer),
                  _resident((1, d))],
        out_specs=[row, pl.BlockSpec((None, width - 1, d), lambda i: (i // tps, 0, 0))],
        out_shape=[jax.ShapeDtypeStruct((m, d), F32),
                   jax.ShapeDtypeStruct((n_seq, width - 1, d), F32)],
        scratch_shapes=[pltpu.VMEM(((tm + HALO) * nl, LANE), F32), pltpu.VMEM((tm * nl, LANE), F32)],
        compiler_params=_params(("arbitrary",)),
        name="odd_prompt",
    )(x, vec(g), w1, vec(b1), cw.reshape(width, nl, LANE), vec(cb), vec(lg), vec(lb), w2, vec(b2))


def _odd_sample_kernel(x_ref, g_ref, w1_ref, b1_ref, cw_ref, cb_ref, lg_ref, lb_ref, w2_ref, b2_ref,
                       st_ref, o_ref, nst_ref):
    n_hist = st_ref.shape[0]
    d = x_ref.shape[1]
    x = x_ref[...]
    h = _rms(x, g_ref[...]).astype(BF16)
    z = _dot(h, w1_ref[...]) + b1_ref[...]
    u = z[:, :d] * jax.nn.sigmoid(z[:, d:])
    c = cw_ref[n_hist:n_hist + 1, :] * u
    for j in range(n_hist):
        c = c + cw_ref[j:j + 1, :] * st_ref[j]
    for j in range(n_hist - 1):
        nst_ref[j] = st_ref[j + 1]
    nst_ref[n_hist - 1] = u
    a = _layernorm_swish(c + cb_ref[...], lg_ref[...], lb_ref[...]).astype(BF16)
    o_ref[...] = x + _dot(a, w2_ref[...]) + b2_ref[...]


def _odd_sample(x, g, w1, w2, layer, b1, cw, cb, lg, lb, b2, state_t):
    m, d = x.shape
    vec = lambda a: a.reshape(1, -1)
    whole = lambda shape: pl.BlockSpec(shape, lambda i: (0,) * len(shape))
    return pl.pallas_call(
        _odd_sample_kernel,
        grid=(1,),
        in_specs=[_resident((m, d)), _resident((1, d)), _resident_layer(w1, layer), _resident((1, 2 * d)),
                  _resident(cw.shape), _resident((1, d)), _resident((1, d)), _resident((1, d)),
                  _resident_layer(w2, layer), _resident((1, d)), _resident(state_t.shape)],
        out_specs=[whole((m, d)), whole(state_t.shape)],
        out_shape=[jax.ShapeDtypeStruct((m, d), F32), jax.ShapeDtypeStruct(state_t.shape, F32)],
        compiler_params=_params(("arbitrary",)),
        name="odd_sample",
    )(x, vec(g), w1, vec(b1), cw, vec(cb), vec(lg), vec(lb), w2, vec(b2), state_t)


def _bias_by_distance(rel_bias, dist):
    n = jnp.maximum(dist, 0)
    nf = jnp.maximum(n, 1).astype(F32)
    large = MAX_EXACT + (jnp.log(nf / MAX_EXACT) / math.log(MAX_DISTANCE / MAX_EXACT)
                         * (N_BUCKETS - MAX_EXACT)).astype(jnp.int32)
    large = jnp.minimum(large, N_BUCKETS - 1)
    return rel_bias[jnp.where(n < MAX_EXACT, n, large)]


def _prompt_bias_tiles(by_dist_nat):
    assert TQ >= FAR - 1
    n_head = by_dist_nat.shape[0]
    by_dist = by_dist_nat * LOG2E
    pad = jnp.zeros((n_head, 1), F32)
    diag = jnp.concatenate([jnp.full((n_head, TQ - 1), NEG, F32), by_dist[:, :TQ], pad], axis=1)
    near = jnp.concatenate([by_dist[:, 1:], pad], axis=1)

    def toeplitz(a):
        period = a.shape[1]
        rows = jnp.tile(a, (1, TQ))[:, :TQ * (period - 1)].reshape(n_head, TQ, period - 1)
        return rows[:, :, TQ - 1:]

    tiles = jnp.stack([toeplitz(near), toeplitz(diag)], axis=1)
    return tiles, by_dist[:, TQ + 1]


def _decode_bias(by_dist_nat, past_len):
    n_head, n_tab = by_dist_nat.shape
    assert n_tab > FAR and past_len + 1 >= n_tab
    far = jnp.broadcast_to(by_dist_nat[:, FAR:FAR + 1], (n_head, past_len + 1 - n_tab))
    return jnp.repeat(jnp.concatenate([far, by_dist_nat[:, ::-1]], axis=1), 2, axis=0)


def kernel(x_prompt, x_sample, cache_k, cache_v, state_conv_b, state_conv_c, page_table, rel_bias, norm_ffn1, ffn1_w_gate, ffn1_w_up, ffn1_w_down, norm_mix, norm_ffn2, ffn2_w_gate, ffn2_w_up, ffn2_w_down, w_in_even, w_out_even, q_norm_gain, k_norm_gain, lambda_q1, lambda_k1, lambda_q2, lambda_k2, subln_gain, conv_b_w, w_pw1, b_pw1, conv_c_w, conv_c_b, ln_c_gain, ln_c_bias, w_pw2, b_pw2):
    bp, tp, d = x_prompt.shape
    bs = x_sample.shape[0]
    depth = norm_ffn1.shape[0]
    n_even, n_phys, page, n_sub, dh_qk = cache_k.shape
    assert dh_qk == DH_QK and cache_v.shape[-1] == DH_V and x_sample.shape[1] == 1
    d_qk = n_sub * dh_qk
    d_a = cache_v.shape[3] * DH_V
    past_len = page_table.shape[1] * page
    assert tp % TM_ROWS == 0 and tp % TM_ODD == 0 and tp % TQ == 0 and tp >= 2 * TQ

    bf = lambda w: w.astype(BF16)
    xp = x_prompt.reshape(bp * tp, d)
    xs = x_sample.reshape(bs, d)

    cache_kt = jnp.transpose(cache_k, (0, 1, 3, 4, 2)).reshape(n_even, n_phys, d_qk, page)
    cache_vt = jnp.transpose(cache_v, (0, 1, 3, 4, 2)).reshape(n_even, n_phys, d_a, page)
    state_c_t = jnp.transpose(state_conv_c, (0, 2, 1, 3))

    p_avg = jnp.kron(jnp.eye(d_qk // DH_QK, dtype=F32), jnp.full((DH_QK, DH_QK), 1.0 / DH_QK, F32)).astype(BF16)
    by_dist_nat = _bias_by_distance(rel_bias, jnp.arange(2 * TQ)).T
    bias_tiles, bias_far = _prompt_bias_tiles(by_dist_nat)
    bias_dec = _decode_bias(by_dist_nat, past_len)
    bias_dec_past, bias_dec_new = bias_dec[:, :past_len], bias_dec[:, past_len:]

    w_in, w_out, w1, w2 = (w.astype(BF16) for w in (w_in_even, w_out_even, w_pw1, w_pw2))
    ffn1_f32 = (ffn1_w_gate, ffn1_w_up, ffn1_w_down)
    ffn2_f32 = (ffn2_w_gate, ffn2_w_up, ffn2_w_down)
    ffn_w = tuple(w[0].astype(BF16) for w in ffn1_f32)

    kv_prompt = ()
    ks_l, vs_l, cbp_l, cbs_l, ccp_l, ccs_l = [], [], [], [], [], []
    for li in range(depth):
        xp, xs, ffn_w = _ffn(xp, xs, norm_ffn1[li], ffn_w, cast=(*ffn2_f32, li))
        if li % 2 == 0:
            e = li // 2
            lam_init = 0.8 - 0.6 * math.exp(-0.3 * li)
            qg = jnp.tile(q_norm_gain[e], d_qk // DH_QK).reshape(1, d_qk)
            kg = jnp.tile(k_norm_gain[e], d_qk // DH_QK).reshape(1, d_qk)
            lam_p = jnp.stack([lambda_q1[e], lambda_k1[e], lambda_q2[e], lambda_k2[e]])
            sg = jnp.tile(subln_gain[e], d_a // DH_V)

            *kv_prompt, qtb, kb, vtb, gcp, cbp = _even_in_prompt(
                xp, norm_mix[li], w_in, e, n_even, tuple(kv_prompt), p_avg, qg, kg, conv_b_w[e], bp, tp)
            qs, kn, vn, gcs, us = _even_in_sample(
                xs, norm_mix[li], w_in, e, p_avg, qg, kg, conv_b_w[e],
                state_conv_b[e, :, 0], state_conv_b[e, :, 1])
            attp, atts = _attention(page_table, bias_far, lam_p, sg, qtb, kb, vtb, bias_tiles,
                                    qs, kn, vn, bias_dec_past, bias_dec_new, cache_kt, cache_vt, e,
                                    bp, tp, lam_init)
            mixer = (attp, gcp, atts, gcs, w_out, e)

            ks_l.append(kn)
            vs_l.append(vn)
            cbp_l.append(cbp)
            cbs_l.append(jnp.stack([state_conv_b[e, :, 1], us], axis=1))
        else:
            o = li // 2
            prm = (norm_mix[li], w1, w2, o, b_pw1[o], conv_c_w[o], conv_c_b[o], ln_c_gain[o], ln_c_bias[o], b_pw2[o])
            xp, ccp = _odd_prompt(xp, *prm, bp, tp)
            xs, ccs = _odd_sample(xs, *prm, state_c_t[o])
            ccp_l.append(ccp)
            ccs_l.append(ccs)
            mixer = ()
        xp, xs, ffn_w = _ffn(xp, xs, norm_ffn2[li], ffn_w, mixer=mixer,
                             cast=(*ffn1_f32, li + 1) if li + 1 < depth else ())

    n_h = d_a // DH_V
    kt_all, vt_all = kv_prompt
    new_k_prompt = jnp.transpose(kt_all.reshape(n_even, bp, n_sub, DH_QK, tp), (0, 1, 4, 2, 3))
    new_v_prompt = jnp.transpose(vt_all.reshape(n_even, bp, n_h, DH_V, tp), (0, 1, 4, 2, 3))
    return (xp.reshape(bp, tp, d), xs.reshape(bs, 1, d),
            new_k_prompt, new_v_prompt,
            jnp.stack(ks_l).reshape(n_even, bs, 1, n_sub, DH_QK),
            jnp.stack(vs_l).reshape(n_even, bs, 1, n_h, DH_V),
            jnp.stack(cbp_l), jnp.stack(cbs_l), jnp.stack(ccp_l),
            jnp.transpose(jnp.stack(ccs_l), (0, 2, 1, 3)))
```
